```python
import jax, jax.numpy as jnp
from jax import lax
import numpy as np

D_MODEL = 2048
BATCH = 2
SEQ = 4096
DEPTH = 4
DEC_BATCH = 32
DEC_SEQ = 4
PAST_LEN = 16384
PAGE_SIZE = 128

HEAD_DIM = 128
ATTN_SCALE = HEAD_DIM ** -0.5
ROPE_THETA = 10000.0
NORM_EPS = 1e-6
Q_BLOCK = 128
N_EVEN = (DEPTH + 1) // 2
N_ODD = DEPTH // 2

D_A = D_MODEL // 2
N_BLK_A = D_A // 128
BLK_A = D_A // N_BLK_A
CONV_W = 4
LRU_C = 8.0
D_B = D_MODEL // 2
H_B = D_B // HEAD_DIM
KVH_B = 2
G_B = H_B // KVH_B
WIN_B = 128

D_C = D_MODEL // 4
POOL_WINDOWS = (2, 4, 8, 16)
N_GRP_C = len(POOL_WINDOWS)
GRP_C = D_C // N_GRP_C
POOL_MAX = max(POOL_WINDOWS)
H_D = (D_MODEL - D_C) // HEAD_DIM
D_D = H_D * HEAD_DIM
KVH_D = 2
G_D = H_D // KVH_D
CMP_BLOCK = 64
N_SEL = 16
WIN_D = 512
FORCE_SCORE = 1e6
DENOM_FLOOR = 1e-30

KV_B = KVH_B * HEAD_DIM
KV_D = KVH_D * HEAD_DIM
EVEN_SIZES = (D_A, D_A, D_B, KV_B, KV_B, D_B)
ODD_SIZES = (D_C, D_C, D_D, KV_D, KV_D, KV_D, KV_D, KV_D, KV_D, D_D, 3 * H_D)
P_EVEN = sum(EVEN_SIZES)
P_ODD = sum(ODD_SIZES)

kernel_name = 'hybrid_rglru_swa_pool_nsa_step'


def rmsnorm(x, g):
    xf = x.astype(jnp.float32)
    y = xf * lax.rsqrt(jnp.mean(xf * xf, axis=-1, keepdims=True) + NORM_EPS)
    return (y * g.astype(jnp.float32)).astype(x.dtype)


def split_cols(z, sizes):
    return jnp.split(z, [int(c) for c in np.cumsum(sizes)[:-1]], axis=-1)


def rope(x, pos):
    half = HEAD_DIM // 2
    inv = ROPE_THETA ** (-jnp.arange(half, dtype=jnp.float32) / half)
    ang = pos.astype(jnp.float32)[:, None] * inv[None, :]
    cos = jnp.cos(ang)[None, :, None, :]
    sin = jnp.sin(ang)[None, :, None, :]
    xf = x.astype(jnp.float32)
    x1, x2 = xf[..., :half], xf[..., half:]
    return jnp.concatenate([x1 * cos - x2 * sin, x2 * cos + x1 * sin], axis=-1).astype(x.dtype)


def attend(q, k, v, mask, sink):
    kvh, g = q.shape[-3], q.shape[-2]
    s = jnp.einsum('...qhgd,...khd->...hgqk', q, k).astype(jnp.float32) * ATTN_SCALE
    s = jnp.where(mask, s, -jnp.inf)
    m = jnp.max(s, axis=-1, keepdims=True)
    if sink is not None:
        sk = sink.astype(jnp.float32).reshape(kvh, g, 1, 1)
        m = jnp.maximum(m, sk)
    m = jnp.where(jnp.isfinite(m), m, 0.0)
    p = jnp.exp(s - m)
    denom = jnp.sum(p, axis=-1, keepdims=True)
    if sink is not None:
        denom = denom + jnp.exp(sk - m)
    p = p / jnp.maximum(denom, DENOM_FLOOR)
    o = jnp.einsum('...hgqk,...khd->...qhgd', p.astype(v.dtype), v)
    return o, p


def band_keys(x, n_prev):
    bsz, t = x.shape[:2]
    nb = t // Q_BLOCK
    pad = jnp.zeros((bsz, n_prev * Q_BLOCK) + x.shape[2:], x.dtype)
    xp = jnp.concatenate([pad, x], axis=1).reshape((bsz, nb + n_prev, Q_BLOCK) + x.shape[2:])
    return jnp.concatenate([xp[:, d:d + nb] for d in range(n_prev + 1)], axis=2)


def banded_attention(q, k, v, window, sink):
    bsz, t = q.shape[:2]
    nb = t // Q_BLOCK
    n_prev = (window + Q_BLOCK - 2) // Q_BLOCK
    qb = q.reshape((bsz, nb, Q_BLOCK) + q.shape[2:])
    kb = band_keys(k, n_prev)
    vb = band_keys(v, n_prev)
    qpos = jnp.arange(t).reshape(nb, Q_BLOCK)
    kpos = (jnp.arange(nb)[:, None] - n_prev) * Q_BLOCK + jnp.arange((n_prev + 1) * Q_BLOCK)[None, :]
    diff = qpos[:, :, None] - kpos[:, None, :]
    mask = (kpos[:, None, :] >= 0) & (diff >= 0) & (diff < window)
    o, _ = attend(qb, kb, vb, mask[:, None, None], sink)
    return o.reshape(q.shape)


def window_step(q, k, v, k_buf, v_buf, window, sink):
    s = q.shape[1]
    w = k_buf.shape[1]
    ka = jnp.concatenate([k_buf, k], axis=1)
    va = jnp.concatenate([v_buf, v], axis=1)
    qpos = PAST_LEN + jnp.arange(s)
    kpos = jnp.concatenate([PAST_LEN - w + jnp.arange(w), qpos])
    diff = qpos[:, None] - kpos[None, :]
    mask = (diff >= 0) & (diff < window)
    o, _ = attend(q, ka, va, mask, sink)
    return o, ka[:, -w:], va[:, -w:]


def causal_conv(x, buf, w, b):
    t = x.shape[1]
    xe = jnp.concatenate([buf, x], axis=1)
    y = b + sum(xe[:, k:k + t] * w[k] for k in range(CONV_W))
    return y, xe[:, -(CONV_W - 1):]


def _lru_combine(left, right):
    a1, b1 = left
    a2, b2 = right
    return a1 * a2, a2 * b1 + b2


def rglru(x, h0, w_r, b_r, w_i, b_i, lam):
    bsz, t, _ = x.shape
    xf = x.astype(jnp.float32)
    xb = xf.reshape(bsz, t, N_BLK_A, BLK_A)
    r = jax.nn.sigmoid(jnp.einsum('btnc,ncd->btnd', xb, w_r.astype(jnp.float32)).reshape(bsz, t, D_A) + b_r.astype(jnp.float32))
    i = jax.nn.sigmoid(jnp.einsum('btnc,ncd->btnd', xb, w_i.astype(jnp.float32)).reshape(bsz, t, D_A) + b_i.astype(jnp.float32))
    log_a = -LRU_C * r * jax.nn.softplus(-lam.astype(jnp.float32))
    a = jnp.exp(log_a)
    u = jnp.sqrt(-jnp.expm1(2.0 * log_a)) * (i * xf)
    u = u.at[:, 0].add(a[:, 0] * h0.astype(jnp.float32))
    _, h = lax.associative_scan(_lru_combine, (a, u), axis=1)
    return h.astype(x.dtype), h[:, -1].astype(x.dtype)


def pool_mix(xc, buf, start_pos, w_pool, scale):
    bsz, t, _ = xc.shape
    p = buf.shape[1]
    xe = jnp.concatenate([buf, xc], axis=1)
    xf = xe.astype(jnp.float32)
    cs = jnp.concatenate([jnp.zeros((bsz, 1, D_C), jnp.float32), jnp.cumsum(xf, axis=1)], axis=1)
    pos = (start_pos + jnp.arange(t)).astype(jnp.float32)[None, :, None]
    end = cs[:, p + 1:]
    means = []
    for g, w in enumerate(POOL_WINDOWS):
        sl = slice(g * GRP_C, (g + 1) * GRP_C)
        win_sum = end[..., sl] - cs[:, p + 1 - w:p + 1 - w + t, sl]
        means.append(win_sum / jnp.minimum(float(w), pos + 1.0))
    pooled = jnp.concatenate(means, axis=-1) - xf[:, p:]
    y = jnp.einsum('btgc,gcd->btgd', pooled.reshape(bsz, t, N_GRP_C, GRP_C), w_pool.astype(jnp.float32)).reshape(bsz, t, D_C)
    y = y * scale.astype(jnp.float32)
    return y.astype(xc.dtype), xe[:, -(POOL_MAX - 1):]


def compress(x, w):
    bsz, t = x.shape[:2]
    xb = x.reshape(bsz, t // CMP_BLOCK, CMP_BLOCK, KVH_D, HEAD_DIM)
    return jnp.einsum('bnlhd,lh->bnhd', xb, w)


def cmp_attention(q, kcmp, vcmp, qpos):
    nb = kcmp.shape[1]
    bend = (jnp.arange(nb) + 1) * CMP_BLOCK - 1
    mask = bend[None, :] <= qpos[:, None]
    return attend(q, kcmp, vcmp, mask, None)


def select_blocks(p_cmp, qpos, nb):
    imp = jnp.sum(p_cmp, axis=2)
    blk = jnp.arange(nb)[None, :]
    cur = (qpos // CMP_BLOCK)[:, None]
    forced = (blk == 0) | (blk == cur) | (blk == cur - 1)
    score = jnp.where(blk > cur, -jnp.inf, jnp.where(forced, FORCE_SCORE, imp))
    val, idx = lax.top_k(score, min(N_SEL, nb))
    return idx, jnp.isfinite(val)


def sel_attention(q, idx, valid, qpos, gather):
    kg, vg = gather(idx)
    bsz, kvh, tq, ns = idx.shape
    kg = kg.reshape(bsz, kvh, tq, ns * CMP_BLOCK, HEAD_DIM)
    vg = vg.reshape(bsz, kvh, tq, ns * CMP_BLOCK, HEAD_DIM)
    kpos = idx[..., None] * CMP_BLOCK + jnp.arange(CMP_BLOCK)
    mask = (kpos <= qpos[None, None, :, None, None]) & valid[..., None]
    mask = mask.reshape(bsz, kvh, tq, ns * CMP_BLOCK)
    s = jnp.einsum('bqhgd,bhqkd->bhgqk', q, kg).astype(jnp.float32) * ATTN_SCALE
    p = jax.nn.softmax(jnp.where(mask[:, :, None], s, -jnp.inf), axis=-1)
    return jnp.einsum('bhgqk,bhqkd->bqhgd', p.astype(vg.dtype), vg)


def nsa_prompt(q, kc, vc, ks, vs, kw, vw, cwk, cwv):
    bsz, t = q.shape[:2]
    nb = t // CMP_BLOCK
    qpos = jnp.arange(t)
    o_cmp, p_cmp = cmp_attention(q, compress(kc, cwk), compress(vc, cwv), qpos)
    idx, valid = select_blocks(p_cmp, qpos, nb)
    ks_bh = ks.reshape(bsz, nb, CMP_BLOCK, KVH_D, HEAD_DIM).transpose(0, 3, 1, 2, 4)
    vs_bh = vs.reshape(bsz, nb, CMP_BLOCK, KVH_D, HEAD_DIM).transpose(0, 3, 1, 2, 4)
    bi = jnp.arange(bsz)[:, None, None, None]
    hi = jnp.arange(KVH_D)[None, :, None, None]

    def gather(ib):
        return ks_bh[bi, hi, ib], vs_bh[bi, hi, ib]

    nc = t // Q_BLOCK
    ns = idx.shape[-1]
    qc = q.reshape(bsz, nc, Q_BLOCK, KVH_D, G_D, HEAD_DIM).transpose(1, 0, 2, 3, 4, 5)
    ic = idx.reshape(bsz, KVH_D, nc, Q_BLOCK, ns).transpose(2, 0, 1, 3, 4)
    valc = valid.reshape(bsz, KVH_D, nc, Q_BLOCK, ns).transpose(2, 0, 1, 3, 4)
    pc = qpos.reshape(nc, Q_BLOCK)
    o_sel = lax.map(lambda a: sel_attention(a[0], a[1], a[2], a[3], gather), (qc, ic, valc, pc))
    o_sel = o_sel.transpose(1, 0, 2, 3, 4, 5).reshape(q.shape)
    o_win = banded_attention(q, kw, vw, WIN_D, None)
    keep = min(WIN_D, t)
    return o_cmp, o_sel, o_win, kw[:, t - keep:], vw[:, t - keep:]


def nsa_step(q, kc, vc, ks, vs, kw, vw, ck_pool, cv_pool, sk_pool, sv_pool, wk_buf, wv_buf, page_table, cwk, cwv):
    db, s = q.shape[:2]
    past = page_table.shape[1] * PAGE_SIZE
    nb = -(-(past + s) // CMP_BLOCK)
    n_past = past // CMP_BLOCK
    n_new = nb - n_past
    pad = nb * CMP_BLOCK - past - s

    def full_rows(pool, new):
        rows = pool[page_table].reshape(db, past, KVH_D, HEAD_DIM)
        return jnp.concatenate([rows, new, jnp.zeros((db, pad, KVH_D, HEAD_DIM), new.dtype)], axis=1)

    qpos = past + jnp.arange(s)
    o_cmp, p_cmp = cmp_attention(q, compress(full_rows(ck_pool, kc), cwk), compress(full_rows(cv_pool, vc), cwv), qpos)
    idx, valid = select_blocks(p_cmp, qpos, nb)
    bpp = PAGE_SIZE // CMP_BLOCK

    def new_blocks(x):
        xp = jnp.concatenate([x, jnp.zeros((db, n_new * CMP_BLOCK - s, KVH_D, HEAD_DIM), x.dtype)], axis=1)
        return xp.reshape(db, n_new, CMP_BLOCK, KVH_D, HEAD_DIM).transpose(0, 3, 1, 2, 4)

    ks_new = new_blocks(ks)
    vs_new = new_blocks(vs)
    sk_r = sk_pool.reshape(-1, bpp, CMP_BLOCK, KVH_D, HEAD_DIM)
    sv_r = sv_pool.reshape(-1, bpp, CMP_BLOCK, KVH_D, HEAD_DIM)
    bi = jnp.arange(db)[:, None, None, None]
    hi = jnp.arange(KVH_D)[None, :, None, None]

    def gather(ib):
        ip = jnp.minimum(ib, n_past - 1)
        phys = page_table[bi, ip // bpp]
        sub = ip % bpp
        inew = jnp.clip(ib - n_past, 0, n_new - 1)
        in_new = (ib >= n_past)[..., None, None]
        kg = jnp.where(in_new, ks_new[bi, hi, inew], sk_r[phys, sub, :, hi])
        vg = jnp.where(in_new, vs_new[bi, hi, inew], sv_r[phys, sub, :, hi])
        return kg, vg

    o_sel = sel_attention(q, idx, valid, qpos, gather)
    o_win, nwk, nwv = window_step(q, kw, vw, wk_buf, wv_buf, WIN_D, None)
    return o_cmp, o_sel, o_win, nwk, nwv


def even_mixer(h, pos, conv_buf, h0, k_buf, v_buf, w_in, cw, cb, w_r, b_r, w_i, b_i, lam, sinks, w_out):
    bsz, t, _ = h.shape
    xa, ga, q, k, v, gb = split_cols(h @ w_in, EVEN_SIZES)
    xa, new_conv = causal_conv(xa, conv_buf, cw, cb)
    ya, h_last = rglru(xa, h0, w_r, b_r, w_i, b_i, lam)
    ya = ya * jax.nn.silu(ga)
    q = rope(q.reshape(bsz, t, H_B, HEAD_DIM), pos).reshape(bsz, t, KVH_B, G_B, HEAD_DIM)
    k = rope(k.reshape(bsz, t, KVH_B, HEAD_DIM), pos)
    v = v.reshape(bsz, t, KVH_B, HEAD_DIM)
    if k_buf is None:
        yb = banded_attention(q, k, v, WIN_B, sinks)
        keep = min(WIN_B, t)
        new_k, new_v = k[:, t - keep:], v[:, t - keep:]
    else:
        yb, new_k, new_v = window_step(q, k, v, k_buf, v_buf, WIN_B, sinks)
    yb = yb.reshape(bsz, t, D_B) * jax.nn.silu(gb)
    out = jnp.concatenate([ya, yb], axis=-1) @ w_out
    return out, h_last, new_conv, new_k, new_v


def odd_mixer(h, pos, pool_buf, start_pos, nsa_cache, page_table, w_in, w_pool, p_scale, cwk, cwv, g_b, w_out):
    bsz, t, _ = h.shape
    xc, gc, q, kc, vc, ks, vs, kw, vw, gd, gl = split_cols(h @ w_in, ODD_SIZES)
    yc, new_pool = pool_mix(xc, pool_buf, start_pos, w_pool, p_scale)
    yc = yc * jax.nn.silu(gc)
    q = rope(q.reshape(bsz, t, H_D, HEAD_DIM), pos).reshape(bsz, t, KVH_D, G_D, HEAD_DIM)
    kc = rope(kc.reshape(bsz, t, KVH_D, HEAD_DIM), pos)
    ks = rope(ks.reshape(bsz, t, KVH_D, HEAD_DIM), pos)
    kw = rope(kw.reshape(bsz, t, KVH_D, HEAD_DIM), pos)
    vc = vc.reshape(bsz, t, KVH_D, HEAD_DIM)
    vs = vs.reshape(bsz, t, KVH_D, HEAD_DIM)
    vw = vw.reshape(bsz, t, KVH_D, HEAD_DIM)
    if nsa_cache is None:
        o_cmp, o_sel, o_win, new_wk, new_wv = nsa_prompt(q, kc, vc, ks, vs, kw, vw, cwk, cwv)
    else:
        ck, cv, sk, sv, wk, wv = nsa_cache
        o_cmp, o_sel, o_win, new_wk, new_wv = nsa_step(q, kc, vc, ks, vs, kw, vw, ck, cv, sk, sv, wk, wv, page_table, cwk, cwv)
    g = jax.nn.sigmoid((gl + g_b).astype(jnp.float32)).reshape(bsz, t, 3, KVH_D, G_D, 1).astype(q.dtype)
    yd = g[:, :, 0] * o_cmp + g[:, :, 1] * o_sel + g[:, :, 2] * o_win
    yd = yd.reshape(bsz, t, D_D) * jax.nn.silu(gd)
    out = jnp.concatenate([yc, yd], axis=-1) @ w_out
    return out, new_pool, kc, vc, ks, vs, new_wk, new_wv


def setup_inputs(seed: int = 0) -> dict:
    key = jax.random.key(seed)
    kit = iter(jax.random.split(key, 40))
    f32 = jnp.float32

    def nrm(shape, scale):
        return scale * jax.random.normal(next(kit), shape, f32)

    n_pages = PAST_LEN // PAGE_SIZE
    n_used = DEC_BATCH * n_pages
    n_phys = n_used + n_used // 4
    wb_b = min(WIN_B, PAST_LEN)
    wb_d = min(WIN_D, PAST_LEN)
    pool_shape = (N_ODD, n_phys, PAGE_SIZE, KVH_D, HEAD_DIM)
    inp = {
        'x_prompt': nrm((BATCH, SEQ, D_MODEL), 1.0),
        'x_sample': nrm((DEC_BATCH, DEC_SEQ, D_MODEL), 1.0),
        'state_lru_h': nrm((N_EVEN, DEC_BATCH, D_A), 0.5),
        'state_lru_conv': nrm((N_EVEN, DEC_BATCH, CONV_W - 1, D_A), 1.0),
        'cache_swa_k': nrm((N_EVEN, DEC_BATCH, wb_b, KVH_B, HEAD_DIM), 1.0),
        'cache_swa_v': nrm((N_EVEN, DEC_BATCH, wb_b, KVH_B, HEAD_DIM), 1.0),
        'state_pool': nrm((N_ODD, DEC_BATCH, POOL_MAX - 1, D_C), 1.0),
        'cache_nsa_cmp_k': nrm(pool_shape, 1.0),
        'cache_nsa_cmp_v': nrm(pool_shape, 1.0),
        'cache_nsa_sel_k': nrm(pool_shape, 1.0),
        'cache_nsa_sel_v': nrm(pool_shape, 1.0),
        'cache_nsa_win_k': nrm((N_ODD, DEC_BATCH, wb_d, KVH_D, HEAD_DIM), 1.0),
        'cache_nsa_win_v': nrm((N_ODD, DEC_BATCH, wb_d, KVH_D, HEAD_DIM), 1.0),
    }
    inp['page_table'] = jax.random.permutation(next(kit), n_phys)[:n_used].reshape(DEC_BATCH, n_pages).astype(jnp.int32)
    u = jax.random.uniform(next(kit), (N_EVEN, D_A), f32, 0.9, 0.999)
    a = u ** (1.0 / LRU_C)
    inp['norm_g'] = 1.0 + nrm((DEPTH, D_MODEL), 0.05)
    inp['final_g'] = 1.0 + nrm((D_MODEL,), 0.05)
    inp['w_in_even'] = nrm((N_EVEN, D_MODEL, P_EVEN), D_MODEL ** -0.5)
    inp['conv_w'] = nrm((N_EVEN, CONV_W, D_A), CONV_W ** -0.5)
    inp['conv_b'] = nrm((N_EVEN, D_A), 0.01)
    inp['w_rgate'] = nrm((N_EVEN, N_BLK_A, BLK_A, BLK_A), BLK_A ** -0.5)
    inp['b_rgate'] = nrm((N_EVEN, D_A), 0.01)
    inp['w_igate'] = nrm((N_EVEN, N_BLK_A, BLK_A, BLK_A), BLK_A ** -0.5)
    inp['b_igate'] = nrm((N_EVEN, D_A), 0.01)
    inp['lru_lambda'] = jnp.log(a) - jnp.log1p(-a)
    inp['swa_sinks'] = nrm((N_EVEN, H_B), 0.5)
    inp['w_out_even'] = nrm((N_EVEN, D_A + D_B, D_MODEL), (D_A + D_B) ** -0.5)
    inp['w_in_odd'] = nrm((N_ODD, D_MODEL, P_ODD), D_MODEL ** -0.5)
    inp['w_pool'] = nrm((N_ODD, N_GRP_C, GRP_C, GRP_C), GRP_C ** -0.5)
    inp['pool_scale'] = 1.0 + nrm((N_ODD, D_C), 0.1)
    inp['cmp_wk'] = (1.0 + nrm((N_ODD, CMP_BLOCK, KVH_D), 0.1)) / CMP_BLOCK
    inp['cmp_wv'] = (1.0 + nrm((N_ODD, CMP_BLOCK, KVH_D), 0.1)) / CMP_BLOCK
    inp['nsa_gate_b'] = nrm((N_ODD, 3 * H_D), 0.01)
    inp['w_out_odd'] = nrm((N_ODD, D_C + D_D, D_MODEL), (D_C + D_D) ** -0.5)
    return inp


def reference(x_prompt, x_sample, state_lru_h, state_lru_conv, cache_swa_k, cache_swa_v, state_pool,
              cache_nsa_cmp_k, cache_nsa_cmp_v, cache_nsa_sel_k, cache_nsa_sel_v, cache_nsa_win_k, cache_nsa_win_v,
              page_table, norm_g, final_g, w_in_even, conv_w, conv_b, w_rgate, b_rgate, w_igate, b_igate,
              lru_lambda, swa_sinks, w_out_even, w_in_odd, w_pool, pool_scale, cmp_wk, cmp_wv, nsa_gate_b, w_out_odd):
    bp, tp = x_prompt.shape[:2]
    bs, ts = x_sample.shape[:2]
    pos_p = jnp.arange(tp, dtype=jnp.int32)
    pos_s = PAST_LEN + jnp.arange(ts, dtype=jnp.int32)
    hp, hs = x_prompt, x_sample
    ev_p, ev_s, od_p, od_s = [], [], [], []
    for i in range(DEPTH):
        j = i // 2
        if i % 2 == 0:
            wts = (w_in_even[j], conv_w[j], conv_b[j], w_rgate[j], b_rgate[j], w_igate[j], b_igate[j],
                   lru_lambda[j], swa_sinks[j], w_out_even[j])
            out_p, *st_p = even_mixer(rmsnorm(hp, norm_g[i]), pos_p, jnp.zeros((bp, CONV_W - 1, D_A), hp.dtype),
                                      jnp.zeros((bp, D_A), hp.dtype), None, None, *wts)
            out_s, *st_s = even_mixer(rmsnorm(hs, norm_g[i]), pos_s, state_lru_conv[j], state_lru_h[j],
                                      cache_swa_k[j], cache_swa_v[j], *wts)
            ev_p.append(st_p)
            ev_s.append(st_s)
        else:
            wts = (w_in_odd[j], w_pool[j], pool_scale[j], cmp_wk[j], cmp_wv[j], nsa_gate_b[j], w_out_odd[j])
            out_p, *st_p = odd_mixer(rmsnorm(hp, norm_g[i]), pos_p, jnp.zeros((bp, POOL_MAX - 1, D_C), hp.dtype), 0,
                                     None, page_table, *wts)
            nsa_cache = (cache_nsa_cmp_k[j], cache_nsa_cmp_v[j], cache_nsa_sel_k[j], cache_nsa_sel_v[j],
                         cache_nsa_win_k[j], cache_nsa_win_v[j])
            out_s, *st_s = odd_mixer(rmsnorm(hs, norm_g[i]), pos_s, state_pool[j], PAST_LEN, nsa_cache, page_table, *wts)
            od_p.append(st_p)
            od_s.append(st_s)
        hp = hp + out_p
        hs = hs + out_s
    y_prompt = rmsnorm(hp, final_g)
    y_sample = rmsnorm(hs, final_g)

    def field(lst, k):
        return jnp.stack([st[k] for st in lst], axis=0)

    return (y_prompt, y_sample,
            field(ev_p, 0), field(ev_p, 1), field(ev_p, 2), field(ev_p, 3),
            field(od_p, 0), field(od_p, 1), field(od_p, 2), field(od_p, 3), field(od_p, 4), field(od_p, 5), field(od_p, 6),
            field(ev_s, 0), field(ev_s, 1), field(ev_s, 2), field(ev_s, 3),
            field(od_s, 0), field(od_s, 1), field(od_s, 2), field(od_s, 3), field(od_s, 4), field(od_s, 5), field(od_s, 6))
```

```python
import functools

import jax
import jax.numpy as jnp
import numpy as np
from jax import lax
from jax.experimental import pallas as pl
from jax.experimental.pallas import tpu as pltpu

F32 = jnp.float32
BF16 = jnp.bfloat16

LANES = 128
SUBLANES = 8
VMEM_LIMIT = 48 * 1024 * 1024

HEAD_DIM = 128
ATTN_SCALE = HEAD_DIM ** -0.5
ROPE_THETA = 10000.0
NORM_EPS = 1e-6
PAGE_SIZE = 128
CONV_W = 4
LRU_C = 8.0
KVH_B = 2
WIN_B = 128
POOL_WINDOWS = (2, 4, 8, 16)
POOL_MAX = 16
KVH_D = 2
CMP_BLOCK = 64
CMP_SHIFT = 6
N_SEL = 16
WIN_D = 512
FORCE_SCORE = 1e6
DENOM_FLOOR = 1e-30
NEG = -1e30
SROWS = 8


def _cparams(sem):
    return pltpu.CompilerParams(dimension_semantics=sem, vmem_limit_bytes=VMEM_LIMIT)


def _dot(a, b):
    return jnp.dot(a, b, preferred_element_type=F32)


def _dot_nt(a, b):
    return lax.dot_general(a, b, (((1,), (1,)), ((), ())), preferred_element_type=F32)


def _silu(x):
    return x * jax.nn.sigmoid(x)


def _rope(x, c, s):
    return x * c + pltpu.roll(x, HEAD_DIM // 2, 1) * s


def _rope_tab_kernel(inv_ref, sgn_ref, c_ref, s_ref, *, start, rows):
    i = pl.program_id(0)
    pos = start + i * rows + lax.broadcasted_iota(jnp.int32, (rows, LANES), 0)
    ang = pos.astype(F32) * inv_ref[...]
    c_ref[...] = jnp.cos(ang)
    s_ref[...] = jnp.sin(ang) * sgn_ref[...]


def rope_tables(n_rows, start):
    half = HEAD_DIM // 2
    inv = ROPE_THETA ** (-jnp.arange(half, dtype=F32) / half)
    inv2 = jnp.concatenate([inv, inv]).reshape(1, LANES)
    sgn = jnp.concatenate([-jnp.ones((half,), F32), jnp.ones((half,), F32)]).reshape(1, LANES)
    rows = min(n_rows, 512)
    assert n_rows % rows == 0
    return pl.pallas_call(
        functools.partial(_rope_tab_kernel, start=start, rows=rows),
        out_shape=(jax.ShapeDtypeStruct((n_rows, LANES), F32),) * 2,
        grid=(n_rows // rows,),
        in_specs=[pl.BlockSpec((1, LANES), lambda i: (0, 0))] * 2,
        out_specs=(pl.BlockSpec((rows, LANES), lambda i: (i, 0)),) * 2,
        compiler_params=_cparams(("arbitrary",)),
        name="rope_tables",
    )(inv2, sgn)


def _norm_proj_kernel(x_ref, g_ref, w_ref, o_ref, xn_ref):
    @pl.when(pl.program_id(1) == 0)
    def _():
        x = x_ref[...]
        ms = jnp.mean(x * x, axis=-1, keepdims=True)
        xn_ref[...] = (x * lax.rsqrt(ms + NORM_EPS) * g_ref[...]).astype(BF16)

    o_ref[...] = _dot(xn_ref[...], w_ref[...])


def norm_proj(x, g, w, tm, tn):
    m, d = x.shape
    n = w.shape[1]
    assert m % tm == 0 and n % tn == 0
    return pl.pallas_call(
        _norm_proj_kernel,
        out_shape=jax.ShapeDtypeStruct((m, n), F32),
        grid=(m // tm, n // tn),
        in_specs=[pl.BlockSpec((tm, d), lambda i, j: (i, 0)),
                  pl.BlockSpec((1, d), lambda i, j: (0, 0)),
                  pl.BlockSpec((d, tn), lambda i, j: (0, j))],
        out_specs=pl.BlockSpec((tm, tn), lambda i, j: (i, j)),
        scratch_shapes=[pltpu.VMEM((tm, d), BF16)],
        compiler_params=_cparams(("arbitrary", "arbitrary")),
        name="norm_proj",
    )(x, g.reshape(1, d), w)


def _out_proj_kernel(ya_ref, yb_ref, wa_ref, wb_ref, r_ref, o_ref):
    o_ref[...] = r_ref[...] + (_dot(ya_ref[...], wa_ref[...]) + _dot(yb_ref[...], wb_ref[...]))


def out_proj(ya, yb, wa, wb, resid, tm, tn):
    m, ka = ya.shape
    kb = yb.shape[1]
    n = wa.shape[1]
    assert m % tm == 0 and n % tn == 0
    return pl.pallas_call(
        _out_proj_kernel,
        out_shape=jax.ShapeDtypeStruct((m, n), F32),
        grid=(m // tm, n // tn),
        in_specs=[pl.BlockSpec((tm, ka), lambda i, j: (i, 0)),
                  pl.BlockSpec((tm, kb), lambda i, j: (i, 0)),
                  pl.BlockSpec((ka, tn), lambda i, j: (0, j)),
                  pl.BlockSpec((kb, tn), lambda i, j: (0, j)),
                  pl.BlockSpec((tm, tn), lambda i, j: (i, j))],
        out_specs=pl.BlockSpec((tm, tn), lambda i, j: (i, j)),
        compiler_params=_cparams(("arbitrary", "arbitrary")),
        name="out_proj",
    )(ya, yb, wa, wb, resid)


def _final_norm_kernel(x_ref, g_ref, o_ref):
    x = x_ref[...]
    ms = jnp.mean(x * x, axis=-1, keepdims=True)
    o_ref[...] = x * lax.rsqrt(ms + NORM_EPS) * g_ref[...]


def final_norm(x, g, tm):
    m, d = x.shape
    return pl.pallas_call(
        _final_norm_kernel,
        out_shape=jax.ShapeDtypeStruct((m, d), F32),
        grid=(m // tm,),
        in_specs=[pl.BlockSpec((tm, d), lambda i: (i, 0)), pl.BlockSpec((1, d), lambda i: (0, 0))],
        out_specs=pl.BlockSpec((tm, d), lambda i: (i, 0)),
        compiler_params=_cparams(("arbitrary",)),
        name="final_norm",
    )(x, g.reshape(1, d))


def _lru_gates(xs, wr, br, wi, bi, sp):
    xb = xs.astype(BF16)
    r = jax.nn.sigmoid(_dot(xb, wr) + br)
    ig = jax.nn.sigmoid(_dot(xb, wi) + bi)
    log_a = -LRU_C * r * sp
    a = jnp.exp(log_a)
    u = jnp.sqrt(-jnp.tanh(log_a) * (a * a + 1.0)) * (ig * xs)
    return a, u


def _softplus(z):
    return jnp.maximum(z, 0.0) + jnp.log1p(jnp.exp(-jnp.abs(z)))


def _lru_prompt_kernel(xa_ref, ga_ref, cw_ref, cb_ref, wr_ref, br_ref, wi_ref, bi_ref, lam_ref,
                       ya_ref, hl_ref, h_sc, halo_sc, *, tt, nblk):
    i = pl.program_id(1)

    @pl.when(i == 0)
    def _():
        h_sc[...] = jnp.zeros_like(h_sc)
        halo_sc[...] = jnp.zeros_like(halo_sc)

    x = xa_ref[...]
    xe = jnp.concatenate([halo_sc[...], x], axis=0)
    halo_sc[...] = x[tt - SUBLANES:, :]
    cw = cw_ref[...]
    y = cb_ref[...] + xe[SUBLANES:, :] * cw[CONV_W - 1:CONV_W, :]
    for k in range(CONV_W - 1):
        y = y + pltpu.roll(xe, CONV_W - 1 - k, 0)[SUBLANES:, :] * cw[k:k + 1, :]
    sp = _softplus(-lam_ref[...])
    rid = lax.broadcasted_iota(jnp.int32, (tt, LANES), 0) & (SUBLANES - 1)
    for n in range(nblk):
        sl = slice(n * LANES, (n + 1) * LANES)
        a, u = _lru_gates(y[:, sl], wr_ref[n], br_ref[:, sl], wi_ref[n], bi_ref[:, sl], sp[:, sl])
        for s in (1, 2, 4):
            keep = rid >= s
            u = jnp.where(keep, a * pltpu.roll(u, s, 0) + u, u)
            a = jnp.where(keep, a * pltpu.roll(a, s, 0), a)
        h = h_sc[:, sl]
        outs = []
        for g in range(tt // SUBLANES):
            rows = slice(g * SUBLANES, (g + 1) * SUBLANES)
            hg = u[rows, :] + a[rows, :] * h
            outs.append(hg)
            h = hg[SUBLANES - 1:, :]
        h_sc[:, sl] = h
        hs = jnp.concatenate(outs, axis=0)
        ya_ref[:, sl] = (hs * _silu(ga_ref[:, sl])).astype(BF16)
    hl_ref[0] = h_sc[...]


def lru_prompt(z, bsz, t, d_a, cw, cb, wr, br, wi, bi, lam, tt=256):
    nt = t // tt
    nblk = d_a // LANES
    row = lambda b, i: (b * nt + i, 0)
    vec = pl.BlockSpec((1, d_a), lambda b, i: (0, 0))
    wspec = pl.BlockSpec((nblk, LANES, LANES), lambda b, i: (0, 0, 0))
    return pl.pallas_call(
        functools.partial(_lru_prompt_kernel, tt=tt, nblk=nblk),
        out_shape=(jax.ShapeDtypeStruct((bsz * t, d_a), BF16), jax.ShapeDtypeStruct((bsz, 1, d_a), F32)),
        grid=(bsz, nt),
        in_specs=[pl.BlockSpec((tt, d_a), row),
                  pl.BlockSpec((tt, d_a), lambda b, i: (b * nt + i, 1)),
                  pl.BlockSpec((CONV_W, d_a), lambda b, i: (0, 0)), vec, wspec, vec, wspec, vec, vec],
        out_specs=(pl.BlockSpec((tt, d_a), row), pl.BlockSpec((1, 1, d_a), lambda b, i: (b, 0, 0))),
        scratch_shapes=[pltpu.VMEM((1, d_a), F32), pltpu.VMEM((SUBLANES, d_a), F32)],
        compiler_params=_cparams(("arbitrary", "arbitrary")),
        name="lru_prompt",
    )(z, z, cw, cb.reshape(1, d_a), wr, br.reshape(1, d_a), wi, bi.reshape(1, d_a), lam.reshape(1, d_a))


def _lru_sample_kernel(x_ref, ga_ref, c0_ref, h0_ref, cw_ref, cb_ref, wr_ref, br_ref, wi_ref, bi_ref, lam_ref,
                       ya_ref, hl_ref, *, ts, nblk):
    cw = cw_ref[...]
    sp = _softplus(-lam_ref[...])
    xe = [c0_ref[k] for k in range(CONV_W - 1)] + [x_ref[t] for t in range(ts)]
    h = h0_ref[...]
    for t in range(ts):
        y = cb_ref[...]
        for k in range(CONV_W):
            y = y + xe[t + k] * cw[k:k + 1, :]
        a_l, u_l = [], []
        for n in range(nblk):
            sl = slice(n * LANES, (n + 1) * LANES)
            a, u = _lru_gates(y[:, sl], wr_ref[n], br_ref[:, sl], wi_ref[n], bi_ref[:, sl], sp[:, sl])
            a_l.append(a)
            u_l.append(u)
        h = jnp.concatenate(a_l, axis=1) * h + jnp.concatenate(u_l, axis=1)
        ya_ref[t] = (h * _silu(ga_ref[t])).astype(BF16)
    for t in range(ts, SROWS):
        ya_ref[t] = jnp.zeros(ya_ref.shape[1:], BF16)
    hl_ref[...] = h


def lru_sample(xa_t, ga_t, conv_t, h0, cw, cb, wr, br, wi, bi, lam, ts):
    _, bsz, d_a = xa_t.shape
    nblk = d_a // LANES
    return pl.pallas_call(
        functools.partial(_lru_sample_kernel, ts=ts, nblk=nblk),
        out_shape=(jax.ShapeDtypeStruct((SROWS, bsz, d_a), BF16), jax.ShapeDtypeStruct((bsz, d_a), F32)),
        compiler_params=pltpu.CompilerParams(vmem_limit_bytes=VMEM_LIMIT),
        name="lru_sample",
    )(xa_t, ga_t, conv_t, h0, cw, cb.reshape(1, d_a), wr, br.reshape(1, d_a), wi, bi.reshape(1, d_a),
      lam.reshape(1, d_a))


def _swa_prompt_kernel(sink_ref, q_ref, kp_ref, kc_ref, vp_ref, vc_ref, gb_ref, cq_ref, sq_ref, cp_ref, sp_ref,
                       yb_ref, kout_ref, *, nq, gq, tq):
    kv = pl.program_id(1)
    i = pl.program_id(2)
    cq = cq_ref[...]
    sq = sq_ref[...]
    k_cur = _rope(kc_ref[...], cq, sq)
    k_prev = _rope(kp_ref[...], cp_ref[...], sp_ref[...])

    @pl.when(i == nq - 1)
    def _():
        kout_ref[0, 0] = k_cur

    k = jnp.concatenate([k_prev, k_cur], axis=0).astype(BF16)
    v = jnp.concatenate([vp_ref[...], vc_ref[...]], axis=0).astype(BF16)
    r = lax.broadcasted_iota(jnp.int32, (tq, 2 * tq), 0)
    c = lax.broadcasted_iota(jnp.int32, (tq, 2 * tq), 1)
    valid = (c > r) & (c <= r + WIN_B) & ((c >= tq) | (i > 0))
    for g in range(gq):
        sl = slice(g * HEAD_DIM, (g + 1) * HEAD_DIM)
        qg = _rope(q_ref[:, sl], cq, sq).astype(BF16)
        s = _dot_nt(qg, k) * ATTN_SCALE
        sink = sink_ref[kv * gq + g]
        m = jnp.maximum(jnp.max(jnp.where(valid, s, NEG), axis=1, keepdims=True), sink)
        p = jnp.where(valid, jnp.exp(s - m), 0.0)
        den = jnp.sum(p, axis=1, keepdims=True) + jnp.exp(sink - m)
        p = p * (1.0 / jnp.maximum(den, DENOM_FLOOR))
        o = _dot(p.astype(BF16), v)
        yb_ref[:, sl] = (o * _silu(gb_ref[:, sl])).astype(BF16)


def swa_prompt(z, sinks, ctab, stab, bsz, t, q_off, k_off, v_off, g_off, n_heads):
    tq = WIN_B
    nq = t // tq
    gq = n_heads // KVH_B
    wq = gq * HEAD_DIM
    assert q_off % wq == 0 and g_off % wq == 0 and k_off % HEAD_DIM == 0 and v_off % HEAD_DIM == 0
    prev = lambda i: jnp.maximum(i - 1, 0)
    in_specs = [
        pl.BlockSpec(memory_space=pltpu.SMEM),
        pl.BlockSpec((tq, wq), lambda b, h, i: (b * nq + i, q_off // wq + h)),
        pl.BlockSpec((tq, HEAD_DIM), lambda b, h, i: (b * nq + prev(i), k_off // HEAD_DIM + h)),
        pl.BlockSpec((tq, HEAD_DIM), lambda b, h, i: (b * nq + i, k_off // HEAD_DIM + h)),
        pl.BlockSpec((tq, HEAD_DIM), lambda b, h, i: (b * nq + prev(i), v_off // HEAD_DIM + h)),
        pl.BlockSpec((tq, HEAD_DIM), lambda b, h, i: (b * nq + i, v_off // HEAD_DIM + h)),
        pl.BlockSpec((tq, wq), lambda b, h, i: (b * nq + i, g_off // wq + h)),
        pl.BlockSpec((tq, LANES), lambda b, h, i: (i, 0)),
        pl.BlockSpec((tq, LANES), lambda b, h, i: (i, 0)),
        pl.BlockSpec((tq, LANES), lambda b, h, i: (prev(i), 0)),
        pl.BlockSpec((tq, LANES), lambda b, h, i: (prev(i), 0)),
    ]
    return pl.pallas_call(
        functools.partial(_swa_prompt_kernel, nq=nq, gq=gq, tq=tq),
        out_shape=(jax.ShapeDtypeStruct((bsz * t, n_heads * HEAD_DIM), BF16),
                   jax.ShapeDtypeStruct((bsz, KVH_B, tq, HEAD_DIM), F32)),
        grid=(bsz, KVH_B, nq),
        in_specs=in_specs,
        out_specs=(pl.BlockSpec((tq, wq), lambda b, h, i: (b * nq + i, h)),
                   pl.BlockSpec((1, 1, tq, HEAD_DIM), lambda b, h, i: (b, h, 0, 0))),
        compiler_params=_cparams(("arbitrary", "arbitrary", "arbitrary")),
        name="swa_prompt",
    )(sinks, z, z, z, z, z, z, ctab, stab, ctab, stab)


def _swa_sample_kernel(sink_ref, z_ref, kc_ref, vc_ref, c_ref, s_ref, yb_ref, kn_ref, *,
                       gq, q_off, k_off, v_off, g_off, ts, wlen):
    c = c_ref[...]
    s = s_ref[...]
    qi = lax.broadcasted_iota(jnp.int32, (SROWS, wlen + SROWS), 0)
    ci = lax.broadcasted_iota(jnp.int32, (SROWS, wlen + SROWS), 1)
    diff = jnp.where(ci < wlen, qi + wlen - ci, qi - (ci - wlen))
    valid = (diff >= 0) & (diff < WIN_B)
    for h in range(KVH_B):
        ks = slice(k_off + h * HEAD_DIM, k_off + (h + 1) * HEAD_DIM)
        vs = slice(v_off + h * HEAD_DIM, v_off + (h + 1) * HEAD_DIM)
        hs = slice(h * HEAD_DIM, (h + 1) * HEAD_DIM)
        k_new = _rope(z_ref[0, :, ks], c, s)
        kn_ref[0, :, hs] = k_new
        k = jnp.concatenate([kc_ref[0, :, hs], k_new], axis=0).astype(BF16)
        v = jnp.concatenate([vc_ref[0, :, hs], z_ref[0, :, vs]], axis=0).astype(BF16)
        for g in range(gq):
            hd = h * gq + g
            qg = _rope(z_ref[0, :, q_off + hd * HEAD_DIM:q_off + (hd + 1) * HEAD_DIM], c, s).astype(BF16)
            sc = _dot_nt(qg, k) * ATTN_SCALE
            sink = sink_ref[hd]
            m = jnp.maximum(jnp.max(jnp.where(valid, sc, NEG), axis=1, keepdims=True), sink)
            p = jnp.where(valid, jnp.exp(sc - m), 0.0)
            den = jnp.sum(p, axis=1, keepdims=True) + jnp.exp(sink - m)
            p = p * (1.0 / jnp.maximum(den, DENOM_FLOOR))
            o = _dot(p.astype(BF16), v)
            gate = _silu(z_ref[0, :, g_off + hd * HEAD_DIM:g_off + (hd + 1) * HEAD_DIM])
            yb_ref[0, :, hd * HEAD_DIM:(hd + 1) * HEAD_DIM] = (o * gate).astype(BF16)


def swa_sample(z3, sinks, cache_k, cache_v, ctab, stab, q_off, k_off, v_off, g_off, n_heads, ts):
    bsz, _, p = z3.shape
    wlen = cache_k.shape[1]
    kvw = KVH_B * HEAD_DIM
    return pl.pallas_call(
        functools.partial(_swa_sample_kernel, gq=n_heads // KVH_B, q_off=q_off, k_off=k_off, v_off=v_off,
                          g_off=g_off, ts=ts, wlen=wlen),
        out_shape=(jax.ShapeDtypeStruct((bsz, SROWS, n_heads * HEAD_DIM), BF16),
                   jax.ShapeDtypeStruct((bsz, SROWS, kvw), F32)),
        grid=(bsz,),
        in_specs=[pl.BlockSpec(memory_space=pltpu.SMEM),
                  pl.BlockSpec((1, SROWS, p), lambda b: (b, 0, 0)),
                  pl.BlockSpec((1, wlen, kvw), lambda b: (b, 0, 0)),
                  pl.BlockSpec((1, wlen, kvw), lambda b: (b, 0, 0)),
                  pl.BlockSpec((SROWS, LANES), lambda b: (0, 0)),
                  pl.BlockSpec((SROWS, LANES), lambda b: (0, 0))],
        out_specs=(pl.BlockSpec((1, SROWS, n_heads * HEAD_DIM), lambda b: (b, 0, 0)),
                   pl.BlockSpec((1, SROWS, kvw), lambda b: (b, 0, 0))),
        compiler_params=_cparams(("arbitrary",)),
        name="swa_sample",
    )(sinks, z3, cache_k.reshape(bsz, wlen, kvw), cache_v.reshape(bsz, wlen, kvw), ctab, stab)


def _pool_prompt_kernel(x_ref, gc_ref, w_ref, sc_ref, y_ref, halo_sc, *, tt, ngrp):
    i = pl.program_id(1)

    @pl.when(i == 0)
    def _():
        halo_sc[...] = jnp.zeros_like(halo_sc)

    x = x_ref[...]
    xe = jnp.concatenate([halo_sc[...], x], axis=0)
    halo_sc[...] = x[tt - POOL_MAX:, :]
    pos1 = (i * tt + 1 + lax.broadcasted_iota(jnp.int32, (tt, LANES), 0)).astype(F32)
    for g in range(ngrp):
        sl = slice(g * LANES, (g + 1) * LANES)
        w = POOL_WINDOWS[g]
        s = xe[:, sl]
        step = 1
        while step < w:
            s = s + pltpu.roll(s, step, 0)
            step *= 2
        pooled = s[POOL_MAX:, :] / jnp.minimum(float(w), pos1) - x[:, sl]
        y = _dot(pooled.astype(BF16), w_ref[g]) * sc_ref[:, sl]
        y_ref[:, sl] = (y * _silu(gc_ref[:, sl])).astype(BF16)


def pool_prompt(z, bsz, t, d_c, x_off, g_off, w_pool, scale, tt=256):
    nt = t // tt
    ngrp = d_c // LANES
    assert x_off % d_c == 0 and g_off % d_c == 0
    return pl.pallas_call(
        functools.partial(_pool_prompt_kernel, tt=tt, ngrp=ngrp),
        out_shape=jax.ShapeDtypeStruct((bsz * t, d_c), BF16),
        grid=(bsz, nt),
        in_specs=[pl.BlockSpec((tt, d_c), lambda b, i: (b * nt + i, x_off // d_c)),
                  pl.BlockSpec((tt, d_c), lambda b, i: (b * nt + i, g_off // d_c)),
                  pl.BlockSpec((ngrp, LANES, LANES), lambda b, i: (0, 0, 0)),
                  pl.BlockSpec((1, d_c), lambda b, i: (0, 0))],
        out_specs=pl.BlockSpec((tt, d_c), lambda b, i: (b * nt + i, 0)),
        scratch_shapes=[pltpu.VMEM((POOL_MAX, d_c), F32)],
        compiler_params=_cparams(("arbitrary", "arbitrary")),
        name="pool_prompt",
    )(z, z, w_pool, scale.reshape(1, d_c))


def _pool_sample_kernel(x_ref, gc_ref, buf_ref, w_ref, sc_ref, y_ref, *, ts, ngrp, start_pos):
    nbuf = POOL_MAX - 1
    xe = [buf_ref[k] for k in range(nbuf)] + [x_ref[t] for t in range(ts)]
    for t in range(ts):
        cols = []
        for g in range(ngrp):
            sl = slice(g * LANES, (g + 1) * LANES)
            w = POOL_WINDOWS[g]
            s = xe[nbuf + t][:, sl]
            for k in range(1, w):
                s = s + xe[nbuf + t - k][:, sl]
            pooled = s / min(float(w), float(start_pos + t + 1)) - xe[nbuf + t][:, sl]
            cols.append(_dot(pooled.astype(BF16), w_ref[g]))
        y = jnp.concatenate(cols, axis=1) * sc_ref[...]
        y_ref[t] = (y * _silu(gc_ref[t])).astype(BF16)
    for t in range(ts, SROWS):
        y_ref[t] = jnp.zeros(y_ref.shape[1:], BF16)


def pool_sample(xc_t, gc_t, buf_t, w_pool, scale, ts, start_pos):
    _, bsz, d_c = xc_t.shape
    return pl.pallas_call(
        functools.partial(_pool_sample_kernel, ts=ts, ngrp=d_c // LANES, start_pos=start_pos),
        out_shape=jax.ShapeDtypeStruct((SROWS, bsz, d_c), BF16),
        compiler_params=pltpu.CompilerParams(vmem_limit_bytes=VMEM_LIMIT),
        name="pool_sample",
    )(xc_t, gc_t, buf_t, w_pool, scale.reshape(1, d_c))


def _nsa_prep_kernel(q_ref, kc_ref, vc_ref, ks_ref, kw_ref, c_ref, s_ref, cwk_ref, cwv_ref,
                     qr_ref, kcr_ref, ksr_ref, kwr_ref, kcmp_ref, vcmp_ref, *, tt, nh):
    c = c_ref[...]
    s = s_ref[...]
    for h in range(nh):
        sl = slice(h * HEAD_DIM, (h + 1) * HEAD_DIM)
        qr_ref[:, sl] = _rope(q_ref[:, sl], c, s).astype(BF16)
    nblk = tt // CMP_BLOCK
    kv_w = KVH_D * HEAD_DIM
    for h in range(KVH_D):
        sl = slice(h * HEAD_DIM, (h + 1) * HEAD_DIM)
        kcr = _rope(kc_ref[:, sl], c, s)
        kcr_ref[:, sl] = kcr
        ksr_ref[:, sl] = _rope(ks_ref[:, sl], c, s)
        kwr_ref[:, sl] = _rope(kw_ref[:, sl], c, s)
        kcmp_ref[0, :, sl] = jnp.sum(kcr.reshape(nblk, CMP_BLOCK, HEAD_DIM) * cwk_ref[:, sl][None], axis=1)
        vcmp_ref[0, :, sl] = jnp.sum(vc_ref[:, sl].reshape(nblk, CMP_BLOCK, HEAD_DIM) * cwv_ref[:, sl][None], axis=1)
    del kv_w


def nsa_prep(z, ctab, stab, cwk2, cwv2, bsz, t, offs, n_heads, tt=512):
    nt = t // tt
    qw = n_heads * HEAD_DIM
    kvw = KVH_D * HEAD_DIM
    assert offs["q"] % qw == 0 and all(offs[k] % kvw == 0 for k in ("kc", "vc", "ks", "kw"))
    row = lambda b, i: (b * nt + i, 0)
    kvspec = lambda name: pl.BlockSpec((tt, kvw), lambda b, i: (b * nt + i, offs[name] // kvw))
    return pl.pallas_call(
        functools.partial(_nsa_prep_kernel, tt=tt, nh=n_heads),
        out_shape=(jax.ShapeDtypeStruct((bsz * t, qw), BF16),
                   jax.ShapeDtypeStruct((bsz * t, kvw), F32),
                   jax.ShapeDtypeStruct((bsz * t, kvw), F32),
                   jax.ShapeDtypeStruct((bsz * t, kvw), F32),
                   jax.ShapeDtypeStruct((bsz, t // CMP_BLOCK, kvw), F32),
                   jax.ShapeDtypeStruct((bsz, t // CMP_BLOCK, kvw), F32)),
        grid=(bsz, nt),
        in_specs=[pl.BlockSpec((tt, qw), lambda b, i: (b * nt + i, offs["q"] // qw)),
                  kvspec("kc"), kvspec("vc"), kvspec("ks"), kvspec("kw"),
                  pl.BlockSpec((tt, LANES), lambda b, i: (i, 0)),
                  pl.BlockSpec((tt, LANES), lambda b, i: (i, 0)),
                  pl.BlockSpec((CMP_BLOCK, kvw), lambda b, i: (0, 0)),
                  pl.BlockSpec((CMP_BLOCK, kvw), lambda b, i: (0, 0))],
        out_specs=(pl.BlockSpec((tt, qw), row), pl.BlockSpec((tt, kvw), row), pl.BlockSpec((tt, kvw), row),
                   pl.BlockSpec((tt, kvw), row),
                   pl.BlockSpec((1, tt // CMP_BLOCK, kvw), lambda b, i: (b, i, 0)),
                   pl.BlockSpec((1, tt // CMP_BLOCK, kvw), lambda b, i: (b, i, 0))),
        compiler_params=_cparams(("arbitrary", "arbitrary")),
        name="nsa_prep",
    )(z, z, z, z, z, ctab, stab, cwk2, cwv2)


def _rank_select(score, nsel):
    nb = score.shape[0]
    jidx = lax.broadcasted_iota(jnp.int32, score.shape, 0)
    cnt = jnp.zeros(score.shape, F32)
    for k in range(nb):
        rk = score[k:k + 1, :]
        tie = jnp.where(jidx > k, 1.0, 0.0)
        cnt = cnt + jnp.where(rk > score, 1.0, jnp.where(rk == score, tie, 0.0))
    return cnt


def _masked_softmax_rows(s, valid):
    m = jnp.max(jnp.where(valid, s, NEG), axis=1, keepdims=True)
    p = jnp.where(valid, jnp.exp(s - m), 0.0)
    den = jnp.sum(p, axis=1, keepdims=True)
    return p * (1.0 / jnp.maximum(den, DENOM_FLOOR))


def _nsa_prompt_kernel(q_ref, kcmp_ref, vcmp_ref, ks_ref, vs_ref, kw_ref, vw_ref, gd_ref, gl_ref, gbias_ref,
                       o_ref, m_sc, l_sc, acc_sc, *, t_len, tq, tk, gq, n_heads):
    kv = pl.program_id(1)
    i = pl.program_id(2)
    nb = t_len // CMP_BLOCK
    q0 = i * tq
    bpt = tk // CMP_BLOCK

    qh = [q_ref[:, g * HEAD_DIM:(g + 1) * HEAD_DIM] for g in range(gq)]
    kc = kcmp_ref[0].astype(BF16)
    vc = vcmp_ref[0].astype(BF16)

    blk_r = lax.broadcasted_iota(jnp.int32, (tq, nb), 1)
    qp_r = q0 + lax.broadcasted_iota(jnp.int32, (tq, nb), 0)
    ok_r = (blk_r + 1) * CMP_BLOCK - 1 <= qp_r
    blk_c = lax.broadcasted_iota(jnp.int32, (nb, tq), 0)
    qp_c = q0 + lax.broadcasted_iota(jnp.int32, (nb, tq), 1)
    ok_c = (blk_c + 1) * CMP_BLOCK - 1 <= qp_c
    o_cmp = []
    imp = jnp.zeros((nb, tq), F32)
    for g in range(gq):
        p = _masked_softmax_rows(_dot_nt(qh[g], kc) * ATTN_SCALE, ok_r)
        o_cmp.append(_dot(p.astype(BF16), vc))
        st = _dot_nt(kc, qh[g]) * ATTN_SCALE
        mt = jnp.max(jnp.where(ok_c, st, NEG), axis=0, keepdims=True)
        pt = jnp.where(ok_c, jnp.exp(st - mt), 0.0)
        dt = jnp.sum(pt, axis=0, keepdims=True)
        imp = imp + pt * (1.0 / jnp.maximum(dt, DENOM_FLOOR))

    cur = jnp.right_shift(qp_c, CMP_SHIFT)
    forced = (blk_c == 0) | (blk_c == cur) | (blk_c == cur - 1)
    score = jnp.where(blk_c > cur, -1.0, jnp.where(forced, FORCE_SCORE, imp))
    cnt = _rank_select(score, N_SEL)
    sel_t = jnp.where((cnt < N_SEL) & (score >= 0.0), 1.0, 0.0).astype(BF16)
    eye = jnp.where(lax.broadcasted_iota(jnp.int32, (tq, tq), 0) == lax.broadcasted_iota(jnp.int32, (tq, tq), 1),
                    1.0, 0.0).astype(BF16)
    sel = _dot_nt(eye, sel_t).astype(BF16)

    m_sc[...] = jnp.full(m_sc.shape, NEG, F32)
    l_sc[...] = jnp.zeros(l_sc.shape, F32)
    acc_sc[...] = jnp.zeros(acc_sc.shape, F32)
    n_kt = (q0 + tq + tk - 1) // tk
    qp_k = q0 + lax.broadcasted_iota(jnp.int32, (tq, tk), 0)
    col_k = lax.broadcasted_iota(jnp.int32, (tq, tk), 1)
    e_row = lax.broadcasted_iota(jnp.int32, (nb, tk), 0)
    e_col = jnp.right_shift(lax.broadcasted_iota(jnp.int32, (nb, tk), 1), CMP_SHIFT)

    def kt_body(kt, carry):
        k0 = pl.multiple_of(kt * tk, tk)
        kt_k = ks_ref[pl.ds(k0, tk), :].astype(BF16)
        kt_v = vs_ref[pl.ds(k0, tk), :].astype(BF16)
        expand = jnp.where(e_row == kt * bpt + e_col, 1.0, 0.0).astype(BF16)
        ok = (_dot(sel, expand) > 0.5) & (k0 + col_k <= qp_k)
        for g in range(gq):
            s = _dot_nt(qh[g], kt_k) * ATTN_SCALE
            m_old = m_sc[g]
            m_new = jnp.maximum(m_old, jnp.max(jnp.where(ok, s, NEG), axis=1, keepdims=True))
            p = jnp.where(ok, jnp.exp(s - m_new), 0.0)
            alpha = jnp.exp(m_old - m_new)
            l_sc[g] = alpha * l_sc[g] + jnp.sum(p, axis=1, keepdims=True)
            acc_sc[g] = alpha * acc_sc[g] + _dot(p.astype(BF16), kt_v)
            m_sc[g] = m_new
        return carry

    lax.fori_loop(0, n_kt, kt_body, 0)

    nwb = (WIN_D + tq - 2) // tq + 1
    wlen = nwb * tq
    w0 = pl.multiple_of(jnp.maximum(i - (nwb - 1), 0) * tq, tq)
    kw = kw_ref[pl.ds(w0, wlen), :].astype(BF16)
    vw = vw_ref[pl.ds(w0, wlen), :].astype(BF16)
    dw = (q0 + lax.broadcasted_iota(jnp.int32, (tq, wlen), 0)) - (w0 + lax.broadcasted_iota(jnp.int32, (tq, wlen), 1))
    ok_w = (dw >= 0) & (dw < WIN_D)

    gate = jax.nn.sigmoid(gl_ref[...] + gbias_ref[...])
    for g in range(gq):
        pw = _masked_softmax_rows(_dot_nt(qh[g], kw) * ATTN_SCALE, ok_w)
        o_win = _dot(pw.astype(BF16), vw)
        o_sel = acc_sc[g] * (1.0 / l_sc[g])
        gs = []
        for br in range(3):
            c0 = br * n_heads + g
            c1 = c0 + gq
            gs.append(jnp.where(kv == 0, gate[:, c0:c0 + 1], gate[:, c1:c1 + 1]))
        yd = gs[0] * o_cmp[g] + gs[1] * o_sel + gs[2] * o_win
        sl = slice(g * HEAD_DIM, (g + 1) * HEAD_DIM)
        o_ref[:, sl] = (yd * _silu(gd_ref[:, sl])).astype(BF16)


def nsa_prompt(z, qr, kcmp, vcmp, ksr, kwr, gbias, bsz, t, offs, n_heads, tq=128, tk=512):
    nq = t // tq
    gq = n_heads // KVH_D
    wq = gq * HEAD_DIM
    nb = t // CMP_BLOCK
    assert offs["gd"] % wq == 0 and offs["gl"] % LANES == 0 and t % tk == 0
    assert KVH_D == 2
    res = lambda off: pl.BlockSpec((t, HEAD_DIM), lambda b, h, i: (b, off // HEAD_DIM + h))
    return pl.pallas_call(
        functools.partial(_nsa_prompt_kernel, t_len=t, tq=tq, tk=tk, gq=gq, n_heads=n_heads),
        out_shape=jax.ShapeDtypeStruct((bsz * t, n_heads * HEAD_DIM), BF16),
        grid=(bsz, KVH_D, nq),
        in_specs=[pl.BlockSpec((tq, wq), lambda b, h, i: (b * nq + i, h)),
                  pl.BlockSpec((1, nb, HEAD_DIM), lambda b, h, i: (b, 0, h)),
                  pl.BlockSpec((1, nb, HEAD_DIM), lambda b, h, i: (b, 0, h)),
                  res(0), res(offs["vs"]), res(0), res(offs["vw"]),
                  pl.BlockSpec((tq, wq), lambda b, h, i: (b * nq + i, offs["gd"] // wq + h)),
                  pl.BlockSpec((tq, LANES), lambda b, h, i: (b * nq + i, offs["gl"] // LANES)),
                  pl.BlockSpec((1, LANES), lambda b, h, i: (0, 0))],
        out_specs=pl.BlockSpec((tq, wq), lambda b, h, i: (b * nq + i, h)),
        scratch_shapes=[pltpu.VMEM((gq, tq, 1), F32), pltpu.VMEM((gq, tq, 1), F32),
                        pltpu.VMEM((gq, tq, HEAD_DIM), F32)],
        compiler_params=_cparams(("arbitrary", "arbitrary", "arbitrary")),
        name="nsa_prompt",
    )(qr, kcmp, vcmp, ksr, z, kwr, z, z, z, gbias)


def _cmp_stream_kernel(pt_ref, kpool, vpool, cwk_ref, cwv_ref, kcmp_ref, vcmp_ref, kbuf, vbuf, sem, *,
                       ppc, nch, nsteps, page_base, npages):
    b = pl.program_id(0)
    c = pl.program_id(1)
    step = b * nch + c
    slot = lax.rem(step, 2)

    def copies(bb, cc, sl):
        out = []
        for p in range(ppc):
            page = page_base + pt_ref[bb * npages + cc * ppc + p]
            out.append(pltpu.make_async_copy(kpool.at[page], kbuf.at[sl, p], sem.at[0, sl]))
            out.append(pltpu.make_async_copy(vpool.at[page], vbuf.at[sl, p], sem.at[1, sl]))
        return out

    @pl.when(step == 0)
    def _():
        for cp in copies(0, 0, 0):
            cp.start()

    @pl.when(step + 1 < nsteps)
    def _():
        nxt = step + 1
        for cp in copies(nxt // nch, lax.rem(nxt, nch), 1 - slot):
            cp.start()

    for cp in copies(b, c, slot):
        cp.wait()

    bpp = PAGE_SIZE // CMP_BLOCK
    width = kbuf.shape[-1]
    xk = kbuf[slot] * cwk_ref[...][None]
    kcmp_ref[0] = jnp.sum(xk.reshape(ppc * bpp, CMP_BLOCK, width), axis=1)
    xv = vbuf[slot] * cwv_ref[...][None]
    vcmp_ref[0] = jnp.sum(xv.reshape(ppc * bpp, CMP_BLOCK, width), axis=1)


def cmp_stream(pt_flat, kpool, vpool, cwk_page, cwv_page, bsz, npages, page_base, ppc=16):
    width = kpool.shape[-1]
    nch = npages // ppc
    bpp = PAGE_SIZE // CMP_BLOCK
    grid_spec = pltpu.PrefetchScalarGridSpec(
        num_scalar_prefetch=1,
        grid=(bsz, nch),
        in_specs=[pl.BlockSpec(memory_space=pl.ANY), pl.BlockSpec(memory_space=pl.ANY),
                  pl.BlockSpec((PAGE_SIZE, width), lambda b, c, pt: (0, 0)),
                  pl.BlockSpec((PAGE_SIZE, width), lambda b, c, pt: (0, 0))],
        out_specs=(pl.BlockSpec((1, ppc * bpp, width), lambda b, c, pt: (b, c, 0)),) * 2,
        scratch_shapes=[pltpu.VMEM((2, ppc, PAGE_SIZE, width), F32), pltpu.VMEM((2, ppc, PAGE_SIZE, width), F32),
                        pltpu.SemaphoreType.DMA((2, 2))],
    )
    return pl.pallas_call(
        functools.partial(_cmp_stream_kernel, ppc=ppc, nch=nch, nsteps=bsz * nch, page_base=page_base,
                          npages=npages),
        out_shape=(jax.ShapeDtypeStruct((bsz, npages * bpp, width), F32),) * 2,
        grid_spec=grid_spec,
        compiler_params=_cparams(("arbitrary", "arbitrary")),
        name="cmp_stream",
    )(pt_flat, kpool, vpool, cwk_page, cwv_page)


def _nsa_sample_cmp_kernel(z_ref, kcp_ref, vcp_ref, c_ref, s_ref, cwk_ref, cwv_ref,
                           ocmp_ref, imp_ref, knew_ref, *, offs, gq, ts, n_past, past):
    c = c_ref[...]
    s = s_ref[...]
    nbp = n_past + SUBLANES
    blk_r = lax.broadcasted_iota(jnp.int32, (SROWS, nbp), 1)
    qp_r = past + lax.broadcasted_iota(jnp.int32, (SROWS, nbp), 0)
    ok_r = (blk_r + 1) * CMP_BLOCK - 1 <= qp_r
    blk_c = lax.broadcasted_iota(jnp.int32, (nbp, LANES), 0)
    qp_c = past + lax.broadcasted_iota(jnp.int32, (nbp, LANES), 1)
    ok_c = (blk_c + 1) * CMP_BLOCK - 1 <= qp_c
    row8 = lax.broadcasted_iota(jnp.int32, (SROWS, HEAD_DIM), 0)
    is_new = row8 < ts
    qpad = jnp.zeros((LANES - SROWS, HEAD_DIM), BF16)
    kvw = KVH_D * HEAD_DIM
    for h in range(KVH_D):
        hs = slice(h * HEAD_DIM, (h + 1) * HEAD_DIM)
        kcr = _rope(z_ref[0, :, offs["kc"] + h * HEAD_DIM:offs["kc"] + (h + 1) * HEAD_DIM], c, s)
        ksr = _rope(z_ref[0, :, offs["ks"] + h * HEAD_DIM:offs["ks"] + (h + 1) * HEAD_DIM], c, s)
        kwr = _rope(z_ref[0, :, offs["kw"] + h * HEAD_DIM:offs["kw"] + (h + 1) * HEAD_DIM], c, s)
        knew_ref[0, :, h * HEAD_DIM:(h + 1) * HEAD_DIM] = kcr
        knew_ref[0, :, kvw + h * HEAD_DIM:kvw + (h + 1) * HEAD_DIM] = ksr
        knew_ref[0, :, 2 * kvw + h * HEAD_DIM:2 * kvw + (h + 1) * HEAD_DIM] = kwr
        vcn = z_ref[0, :, offs["vc"] + h * HEAD_DIM:offs["vc"] + (h + 1) * HEAD_DIM]
        nk = jnp.sum(jnp.where(is_new, kcr * cwk_ref[0:SROWS, hs], 0.0), axis=0, keepdims=True)
        nv = jnp.sum(jnp.where(is_new, vcn * cwv_ref[0:SROWS, hs], 0.0), axis=0, keepdims=True)
        nk8 = jnp.where(row8 == 0, jnp.broadcast_to(nk, (SROWS, HEAD_DIM)), 0.0)
        nv8 = jnp.where(row8 == 0, jnp.broadcast_to(nv, (SROWS, HEAD_DIM)), 0.0)
        kall = jnp.concatenate([kcp_ref[0, :, hs], nk8], axis=0).astype(BF16)
        vall = jnp.concatenate([vcp_ref[0, :, hs], nv8], axis=0).astype(BF16)
        imp = jnp.zeros((nbp, LANES), F32)
        for g in range(gq):
            hd = h * gq + g
            q = _rope(z_ref[0, :, offs["q"] + hd * HEAD_DIM:offs["q"] + (hd + 1) * HEAD_DIM], c, s).astype(BF16)
            p = _masked_softmax_rows(_dot_nt(q, kall) * ATTN_SCALE, ok_r)
            ocmp_ref[0, :, hd * HEAD_DIM:(hd + 1) * HEAD_DIM] = _dot(p.astype(BF16), vall)
            st = _dot_nt(kall, jnp.concatenate([q, qpad], axis=0)) * ATTN_SCALE
            mt = jnp.max(jnp.where(ok_c, st, NEG), axis=0, keepdims=True)
            pt = jnp.where(ok_c, jnp.exp(st - mt), 0.0)
            dt = jnp.sum(pt, axis=0, keepdims=True)
            imp = imp + pt * (1.0 / jnp.maximum(dt, DENOM_FLOOR))
        imp_ref[0, h] = imp


def nsa_sample_cmp(z3, kcp, vcp, ctab, stab, cwk2, cwv2, offs, n_heads, ts, past):
    bsz, _, p = z3.shape
    n_past = kcp.shape[1]
    kvw = KVH_D * HEAD_DIM
    nbp = n_past + SUBLANES
    return pl.pallas_call(
        functools.partial(_nsa_sample_cmp_kernel, offs=offs, gq=n_heads // KVH_D, ts=ts, n_past=n_past, past=past),
        out_shape=(jax.ShapeDtypeStruct((bsz, SROWS, n_heads * HEAD_DIM), F32),
                   jax.ShapeDtypeStruct((bsz, KVH_D, nbp, LANES), F32),
                   jax.ShapeDtypeStruct((bsz, SROWS, 3 * kvw), F32)),
        grid=(bsz,),
        in_specs=[pl.BlockSpec((1, SROWS, p), lambda b: (b, 0, 0)),
                  pl.BlockSpec((1, n_past, kvw), lambda b: (b, 0, 0)),
                  pl.BlockSpec((1, n_past, kvw), lambda b: (b, 0, 0)),
                  pl.BlockSpec((SROWS, LANES), lambda b: (0, 0)),
                  pl.BlockSpec((SROWS, LANES), lambda b: (0, 0)),
                  pl.BlockSpec((CMP_BLOCK, kvw), lambda b: (0, 0)),
                  pl.BlockSpec((CMP_BLOCK, kvw), lambda b: (0, 0))],
        out_specs=(pl.BlockSpec((1, SROWS, n_heads * HEAD_DIM), lambda b: (b, 0, 0)),
                   pl.BlockSpec((1, KVH_D, nbp, LANES), lambda b: (b, 0, 0, 0)),
                   pl.BlockSpec((1, SROWS, 3 * kvw), lambda b: (b, 0, 0))),
        compiler_params=_cparams(("arbitrary",)),
        name="nsa_sample_cmp",
    )(z3, kcp, vcp, ctab, stab, cwk2, cwv2)


def _select_sample_kernel(imp_ref, qpos_ref, idx_ref, score_sc, cnt_sc, *, nbp, nsel):
    shape = score_sc.shape
    blk = lax.broadcasted_iota(jnp.int32, shape, 0)
    cur = jnp.right_shift(qpos_ref[...], CMP_SHIFT)
    forced = (blk == 0) | (blk == cur) | (blk == cur - 1)
    score_sc[...] = jnp.where(blk > cur, -1.0, jnp.where(forced, FORCE_SCORE, imp_ref[...]))
    cnt_sc[...] = jnp.zeros(shape, F32)

    def body(k, carry):
        rk = score_sc[pl.ds(k, 1), :]
        sc = score_sc[...]
        tie = jnp.where(blk > k, 1.0, 0.0)
        cnt_sc[...] = cnt_sc[...] + jnp.where(rk > sc, 1.0, jnp.where(rk == sc, tie, 0.0))
        return carry

    lax.fori_loop(0, nbp, body, 0)
    cnt = cnt_sc[...]
    ok = score_sc[...] >= 0.0
    blk_f = blk.astype(F32)
    for r in range(nsel):
        picked = jnp.sum(jnp.where((cnt == float(r)) & ok, blk_f, 0.0), axis=0, keepdims=True)
        idx_ref[r:r + 1, :] = picked.astype(jnp.int32)


def select_sample(imp_t, qpos):
    nbp, nc = imp_t.shape
    return pl.pallas_call(
        functools.partial(_select_sample_kernel, nbp=nbp, nsel=N_SEL),
        out_shape=jax.ShapeDtypeStruct((N_SEL, nc), jnp.int32),
        scratch_shapes=[pltpu.VMEM((nbp, nc), F32), pltpu.VMEM((nbp, nc), F32)],
        compiler_params=pltpu.CompilerParams(vmem_limit_bytes=VMEM_LIMIT),
        name="select_sample",
    )(imp_t, qpos)


def _nsa_sample_sel_kernel(idx_ref, pt_ref, z_ref, ocmp_ref, knew_ref, wk_ref, wv_ref, c_ref, s_ref, gbias_ref,
                           kpool, vpool, y_ref, kbuf, vbuf, sem, *, offs, gq, n_heads, ts, n_past, past,
                           page_base, npages, wlen):
    b = pl.program_id(0)
    c = c_ref[...]
    s = s_ref[...]
    bpp = PAGE_SIZE // CMP_BLOCK
    kvw = KVH_D * HEAD_DIM

    def slot_copies(h, t, r):
        blk = idx_ref[((b * KVH_D + h) * ts + t) * N_SEL + r]
        past_blk = jnp.minimum(blk, n_past - 1)
        page = page_base + pt_ref[b * npages + past_blk // bpp]
        row0 = pl.multiple_of(lax.rem(past_blk, bpp) * CMP_BLOCK, CMP_BLOCK)
        src_k = kpool.at[page, pl.ds(row0, CMP_BLOCK), pl.ds(h * HEAD_DIM, HEAD_DIM)]
        src_v = vpool.at[page, pl.ds(row0, CMP_BLOCK), pl.ds(h * HEAD_DIM, HEAD_DIM)]
        return blk, (pltpu.make_async_copy(src_k, kbuf.at[h, t, r], sem.at[0]),
                     pltpu.make_async_copy(src_v, vbuf.at[h, t, r], sem.at[1]))

    slots = [(h, t, r) for h in range(KVH_D) for t in range(ts) for r in range(N_SEL)]
    for h, t, r in slots:
        blk, cps = slot_copies(h, t, r)

        @pl.when(blk < n_past)
        def _():
            for cp in cps:
                cp.start()

        @pl.when(blk >= n_past)
        def _():
            kbuf[h, t, r] = jnp.zeros((CMP_BLOCK, HEAD_DIM), F32)
            vbuf[h, t, r] = jnp.zeros((CMP_BLOCK, HEAD_DIM), F32)

    row8 = lax.broadcasted_iota(jnp.int32, (SROWS, HEAD_DIM), 0)
    kpad = jnp.zeros((LANES - SROWS, HEAD_DIM), F32)
    qi = lax.broadcasted_iota(jnp.int32, (SROWS, wlen + LANES), 0)
    ci = lax.broadcasted_iota(jnp.int32, (SROWS, wlen + LANES), 1)
    dw = jnp.where(ci < wlen, qi + wlen - ci, qi - (ci - wlen))
    ok_w = (dw >= 0) & (dw < WIN_D) & ((ci < wlen) | (ci - wlen < ts))
    qr = []
    o_win = []
    for h in range(KVH_D):
        hs = slice(h * HEAD_DIM, (h + 1) * HEAD_DIM)
        kwn = knew_ref[0, :, 2 * kvw + h * HEAD_DIM:2 * kvw + (h + 1) * HEAD_DIM]
        vwn = z_ref[0, :, offs["vw"] + h * HEAD_DIM:offs["vw"] + (h + 1) * HEAD_DIM]
        kw = jnp.concatenate([wk_ref[0, :, hs], kwn, kpad], axis=0).astype(BF16)
        vw = jnp.concatenate([wv_ref[0, :, hs], vwn, kpad], axis=0).astype(BF16)
        for g in range(gq):
            hd = h * gq + g
            q = _rope(z_ref[0, :, offs["q"] + hd * HEAD_DIM:offs["q"] + (hd + 1) * HEAD_DIM], c, s)
            qr.append(q)
            pw = _masked_softmax_rows(_dot_nt(q.astype(BF16), kw) * ATTN_SCALE, ok_w)
            o_win.append(_dot(pw.astype(BF16), vw))

    for h, t, r in slots:
        blk, cps = slot_copies(h, t, r)

        @pl.when(blk < n_past)
        def _():
            for cp in cps:
                cp.wait()

    nkeys = N_SEL * CMP_BLOCK
    colk = lax.broadcasted_iota(jnp.int32, (SROWS, nkeys + LANES), 1)
    slot_of_col = jnp.right_shift(colk, CMP_SHIFT)
    in_blk = colk & (CMP_BLOCK - 1)
    o_sel = [jnp.zeros((SROWS, HEAD_DIM), F32) for _ in range(n_heads)]
    for h in range(KVH_D):
        ksn = knew_ref[0, :, kvw + h * HEAD_DIM:kvw + (h + 1) * HEAD_DIM]
        vsn = z_ref[0, :, offs["vs"] + h * HEAD_DIM:offs["vs"] + (h + 1) * HEAD_DIM]
        for t in range(ts):
            qt = jnp.zeros((SROWS, HEAD_DIM), F32)
            for g in range(gq):
                qt = jnp.where(row8 == g, jnp.broadcast_to(qr[h * gq + g][t:t + 1, :], (SROWS, HEAD_DIM)), qt)
            qt = qt.astype(BF16)
            kall = jnp.concatenate([kbuf[h, t].reshape(nkeys, HEAD_DIM), ksn, kpad], axis=0).astype(BF16)
            vall = jnp.concatenate([vbuf[h, t].reshape(nkeys, HEAD_DIM), vsn, kpad], axis=0).astype(BF16)
            far = past + ts + SROWS
            kpos = jnp.where(colk >= nkeys, past + (colk - nkeys), 0)
            kpos = jnp.where((colk >= nkeys) & (colk - nkeys >= ts), far, kpos)
            n_new = jnp.int32(0)
            for r in range(N_SEL):
                blk = idx_ref[((b * KVH_D + h) * ts + t) * N_SEL + r]
                n_new = n_new + jnp.where(blk >= n_past, 1, 0)
                base = jnp.where(blk < n_past, blk * CMP_BLOCK, far)
                kpos = jnp.where(slot_of_col == r, base + in_blk, kpos)
            kpos = jnp.where((colk >= nkeys) & (n_new == 0), far, kpos)
            ok = kpos <= past + t
            p = _masked_softmax_rows(_dot_nt(qt, kall) * ATTN_SCALE, ok)
            res = _dot(p.astype(BF16), vall)
            for g in range(gq):
                hd = h * gq + g
                o_sel[hd] = jnp.where(row8 == t, jnp.broadcast_to(res[g:g + 1, :], (SROWS, HEAD_DIM)), o_sel[hd])

    gate = jax.nn.sigmoid(z_ref[0, :, offs["gl"]:offs["gl"] + LANES] + gbias_ref[...])
    for hd in range(n_heads):
        sl = slice(hd * HEAD_DIM, (hd + 1) * HEAD_DIM)
        yd = (gate[:, hd:hd + 1] * ocmp_ref[0, :, sl]
              + gate[:, n_heads + hd:n_heads + hd + 1] * o_sel[hd]
              + gate[:, 2 * n_heads + hd:2 * n_heads + hd + 1] * o_win[hd])
        y_ref[0, :, sl] = (yd * _silu(z_ref[0, :, offs["gd"] + hd * HEAD_DIM:offs["gd"] + (hd + 1) * HEAD_DIM])
                           ).astype(BF16)


def nsa_sample_sel(idx_flat, pt_flat, z3, ocmp, knew, wk, wv, ctab, stab, gbias, kpool, vpool, offs, n_heads,
                   ts, past, page_base, npages, layer, n_layers_b):
    bsz, _, p = z3.shape
    wlen = wk.shape[1]
    kvw = KVH_D * HEAD_DIM
    qw = n_heads * HEAD_DIM
    n_past = npages * (PAGE_SIZE // CMP_BLOCK)
    grid_spec = pltpu.PrefetchScalarGridSpec(
        num_scalar_prefetch=2,
        grid=(bsz,),
        in_specs=[pl.BlockSpec((1, SROWS, p), lambda b, i, t: (b, 0, 0)),
                  pl.BlockSpec((1, SROWS, qw), lambda b, i, t: (b, 0, 0)),
                  pl.BlockSpec((1, SROWS, 3 * kvw), lambda b, i, t: (b, 0, 0)),
                  pl.BlockSpec((1, wlen, kvw), lambda b, i, t: (layer * n_layers_b + b, 0, 0)),
                  pl.BlockSpec((1, wlen, kvw), lambda b, i, t: (layer * n_layers_b + b, 0, 0)),
                  pl.BlockSpec((SROWS, LANES), lambda b, i, t: (0, 0)),
                  pl.BlockSpec((SROWS, LANES), lambda b, i, t: (0, 0)),
                  pl.BlockSpec((1, LANES), lambda b, i, t: (0, 0)),
                  pl.BlockSpec(memory_space=pl.ANY), pl.BlockSpec(memory_space=pl.ANY)],
        out_specs=pl.BlockSpec((1, SROWS, qw), lambda b, i, t: (b, 0, 0)),
        scratch_shapes=[pltpu.VMEM((KVH_D, ts, N_SEL, CMP_BLOCK, HEAD_DIM), F32),
                        pltpu.VMEM((KVH_D, ts, N_SEL, CMP_BLOCK, HEAD_DIM), F32),
                        pltpu.SemaphoreType.DMA((2,))],
    )
    return pl.pallas_call(
        functools.partial(_nsa_sample_sel_kernel, offs=offs, gq=n_heads // KVH_D, n_heads=n_heads, ts=ts,
                          n_past=n_past, past=past, page_base=page_base, npages=npages, wlen=wlen),
        out_shape=jax.ShapeDtypeStruct((bsz, SROWS, qw), BF16),
        grid_spec=grid_spec,
        compiler_params=_cparams(("arbitrary",)),
        name="nsa_sample_sel",
    )(idx_flat, pt_flat, z3, ocmp, knew, wk, wv, ctab, stab, gbias, kpool, vpool)


EVEN_OFFS = dict(xa=0, ga=1024, q=2048, k=3072, v=3328, gb=3584)


def _odd_layout(d_c, d_d, kvw, n_gate):
    offs = {}
    pos = 0
    for name, width in (("q", d_d), ("gd", d_d), ("xc", d_c), ("gc", d_c), ("kc", kvw), ("vc", kvw), ("ks", kvw),
                        ("vs", kvw), ("kw", kvw), ("vw", kvw), ("gl", n_gate)):
        offs[name] = pos
        pos += width
    total = -(-pos // LANES) * LANES
    return offs, pos, total


def _permute_odd_weight(w, d_c, d_d, kvw, n_gate, total):
    o = 0
    segs = {}
    for name, width in (("xc", d_c), ("gc", d_c), ("q", d_d), ("kc", kvw), ("vc", kvw), ("ks", kvw), ("vs", kvw),
                        ("kw", kvw), ("vw", kvw), ("gd", d_d), ("gl", n_gate)):
        segs[name] = w[:, o:o + width]
        o += width
    cols = [segs[n] for n in ("q", "gd", "xc", "gc", "kc", "vc", "ks", "vs", "kw", "vw", "gl")]
    used = sum(c.shape[1] for c in cols)
    cols.append(jnp.zeros((w.shape[0], total - used), w.dtype))
    return jnp.concatenate(cols, axis=1).astype(BF16)


def _pick_tile(n, prefs):
    for t in prefs:
        if n % t == 0:
            return t
    raise ValueError(n)


def kernel(x_prompt, x_sample, state_lru_h, state_lru_conv, cache_swa_k, cache_swa_v, state_pool, cache_nsa_cmp_k, cache_nsa_cmp_v, cache_nsa_sel_k, cache_nsa_sel_v, cache_nsa_win_k, cache_nsa_win_v, page_table, norm_g, final_g, w_in_even, conv_w, conv_b, w_rgate, b_rgate, w_igate, b_igate, lru_lambda, swa_sinks, w_out_even, w_in_odd, w_pool, pool_scale, cmp_wk, cmp_wv, nsa_gate_b, w_out_odd):
    bp, tp, d = x_prompt.shape
    bs, ts, _ = x_sample.shape
    depth = norm_g.shape[0]
    npages = page_table.shape[1]
    past = npages * PAGE_SIZE
    n_phys = cache_nsa_cmp_k.shape[1]
    d_a = state_lru_h.shape[-1]
    n_heads_b = swa_sinks.shape[1]
    d_b = n_heads_b * HEAD_DIM
    d_c = state_pool.shape[-1]
    n_heads_d = nsa_gate_b.shape[1] // 3
    d_d = n_heads_d * HEAD_DIM
    kvw_b = KVH_B * HEAD_DIM
    kvw_d = KVH_D * HEAD_DIM
    assert ts <= SROWS and past % CMP_BLOCK == 0 and (past + ts - 1) // CMP_BLOCK == past // CMP_BLOCK

    hp = x_prompt.reshape(bp * tp, d)
    hs = jnp.pad(x_sample, ((0, 0), (0, SROWS - ts), (0, 0))).reshape(bs * SROWS, d)
    ms = bs * SROWS
    ctab_p, stab_p = rope_tables(tp, 0)
    ctab_s, stab_s = rope_tables(SROWS, past)
    tm_p = _pick_tile(bp * tp, (1024, 512, 256, 128))
    pt_flat = page_table.reshape(-1)

    odd_offs, odd_used, odd_total = _odd_layout(d_c, d_d, kvw_d, 3 * n_heads_d)
    ev_p, ev_s, od_p, od_s = [], [], [], []
    for i in range(depth):
        j = i // 2
        if i % 2 == 0:
            p_even = w_in_even.shape[2]
            w_in = w_in_even[j].astype(BF16)
            tn = _pick_tile(p_even, (512, 256, 128))
            zp = norm_proj(hp, norm_g[i], w_in, tm_p, tn)
            zs = norm_proj(hs, norm_g[i], w_in, ms, tn)
            wr = w_rgate[j].astype(BF16)
            wi = w_igate[j].astype(BF16)
            lru_args = (conv_w[j], conv_b[j], wr, b_rgate[j], wi, b_igate[j], lru_lambda[j])
            eo = EVEN_OFFS
            ya_p, hl_p = lru_prompt(zp, bp, tp, d_a, *lru_args)
            yb_p, kout_p = swa_prompt(zp, swa_sinks[j], ctab_p, stab_p, bp, tp, eo["q"], eo["k"], eo["v"], eo["gb"],
                                      n_heads_b)
            wo = w_out_even[j].astype(BF16)
            tn_o = _pick_tile(d, (512, 256, 128))
            hp = out_proj(ya_p, yb_p, wo[:d_a], wo[d_a:], hp, tm_p, tn_o)
            zp3 = zp.reshape(bp, tp, p_even)
            keep = min(WIN_B, tp)
            ev_p.append((hl_p[:, 0], zp3[:, tp - (CONV_W - 1):, eo["xa"]:eo["xa"] + d_a],
                         kout_p.transpose(0, 2, 1, 3)[:, WIN_B - keep:],
                         zp3[:, tp - keep:, eo["v"]:eo["v"] + kvw_b].reshape(bp, keep, KVH_B, HEAD_DIM)))
            zs3 = zs.reshape(bs, SROWS, p_even)
            zst = zs3.transpose(1, 0, 2)
            ya_t, hl_s = lru_sample(zst[:, :, eo["xa"]:eo["xa"] + d_a], zst[:, :, eo["ga"]:eo["ga"] + d_a],
                                    state_lru_conv[j].transpose(1, 0, 2), state_lru_h[j], *lru_args, ts)
            ya_s = ya_t.transpose(1, 0, 2).reshape(ms, d_a)
            yb_s, knew = swa_sample(zs3, swa_sinks[j], cache_swa_k[j], cache_swa_v[j], ctab_s, stab_s,
                                    eo["q"], eo["k"], eo["v"], eo["gb"], n_heads_b, ts)
            hs = out_proj(ya_s, yb_s.reshape(ms, d_b), wo[:d_a], wo[d_a:], hs, ms, tn_o)
            wlen = cache_swa_k.shape[2]
            new_conv = jnp.concatenate([state_lru_conv[j], zs3[:, :ts, eo["xa"]:eo["xa"] + d_a]], axis=1)[:, -(CONV_W - 1):]
            new_k = jnp.concatenate([cache_swa_k[j], knew[:, :ts].reshape(bs, ts, KVH_B, HEAD_DIM)], axis=1)[:, -wlen:]
            new_v = jnp.concatenate([cache_swa_v[j], zs3[:, :ts, eo["v"]:eo["v"] + kvw_b].reshape(bs, ts, KVH_B, HEAD_DIM)],
                                    axis=1)[:, -wlen:]
            ev_s.append((hl_s, new_conv, new_k, new_v))
        else:
            oo = odd_offs
            w_in = _permute_odd_weight(w_in_odd[j], d_c, d_d, kvw_d, 3 * n_heads_d, odd_total)
            tn = _pick_tile(odd_total, (640, 512, 384, 256, 128))
            zp = norm_proj(hp, norm_g[i], w_in, tm_p, tn)
            zs = norm_proj(hs, norm_g[i], w_in, ms, tn)
            wp = w_pool[j].astype(BF16)
            cwk2 = jnp.repeat(cmp_wk[j], HEAD_DIM, axis=1)
            cwv2 = jnp.repeat(cmp_wv[j], HEAD_DIM, axis=1)
            gbias = jnp.pad(nsa_gate_b[j], (0, LANES - 3 * n_heads_d)).reshape(1, LANES)
            wo = w_out_odd[j].astype(BF16)
            tn_o = _pick_tile(d, (512, 256, 128))
            yc_p = pool_prompt(zp, bp, tp, d_c, oo["xc"], oo["gc"], wp, pool_scale[j])
            qr, kcr, ksr, kwr, kcmp, vcmp = nsa_prep(zp, ctab_p, stab_p, cwk2, cwv2, bp, tp, oo, n_heads_d)
            yd_p = nsa_prompt(zp, qr, kcmp, vcmp, ksr, kwr, gbias, bp, tp, oo, n_heads_d)
            hp = out_proj(yc_p, yd_p, wo[:d_c], wo[d_c:], hp, tm_p, tn_o)
            zp3 = zp.reshape(bp, tp, odd_total)
            kv4 = lambda a: a.reshape(bp, tp, KVH_D, HEAD_DIM)
            keep = min(WIN_D, tp)
            od_p.append((zp3[:, tp - (POOL_MAX - 1):, oo["xc"]:oo["xc"] + d_c],
                         kv4(kcr), kv4(zp3[:, :, oo["vc"]:oo["vc"] + kvw_d]),
                         kv4(ksr), kv4(zp3[:, :, oo["vs"]:oo["vs"] + kvw_d]),
                         kv4(kwr)[:, tp - keep:], kv4(zp3[:, :, oo["vw"]:oo["vw"] + kvw_d])[:, tp - keep:]))
            zs3 = zs.reshape(bs, SROWS, odd_total)
            zst = zs3.transpose(1, 0, 2)
            yc_t = pool_sample(zst[:, :, oo["xc"]:oo["xc"] + d_c], zst[:, :, oo["gc"]:oo["gc"] + d_c],
                               state_pool[j].transpose(1, 0, 2), wp, pool_scale[j], ts, past)
            yc_s = yc_t.transpose(1, 0, 2).reshape(ms, d_c)
            n_layers_odd = cache_nsa_cmp_k.shape[0]
            pool3 = lambda a: a.reshape(n_layers_odd * n_phys, PAGE_SIZE, kvw_d)
            cwk_page = jnp.tile(cwk2, (PAGE_SIZE // CMP_BLOCK, 1))
            cwv_page = jnp.tile(cwv2, (PAGE_SIZE // CMP_BLOCK, 1))
            kcp, vcp = cmp_stream(pt_flat, pool3(cache_nsa_cmp_k), pool3(cache_nsa_cmp_v), cwk_page, cwv_page,
                                  bs, npages, j * n_phys)
            ocmp, imp, knew = nsa_sample_cmp(zs3, kcp, vcp, ctab_s, stab_s, cwk2, cwv2, oo, n_heads_d, ts, past)
            nbp = imp.shape[2]
            assert nbp >= N_SEL
            imp_t = imp[:, :, :, :ts].transpose(2, 0, 1, 3).reshape(nbp, bs * KVH_D * ts)
            qpos = jnp.tile(past + jnp.arange(ts, dtype=jnp.int32), bs * KVH_D).reshape(1, -1)
            idx = select_sample(imp_t, qpos)
            idx_flat = idx.T.reshape(-1)
            wlen = cache_nsa_win_k.shape[2]
            win3 = lambda a: a.reshape(n_layers_odd * bs, wlen, kvw_d)
            yd_s = nsa_sample_sel(idx_flat, pt_flat, zs3, ocmp, knew, win3(cache_nsa_win_k), win3(cache_nsa_win_v),
                                  ctab_s, stab_s, gbias, pool3(cache_nsa_sel_k), pool3(cache_nsa_sel_v), oo,
                                  n_heads_d, ts, past, j * n_phys, npages, j, bs)
            hs = out_proj(yc_s, yd_s.reshape(ms, d_d), wo[:d_c], wo[d_c:], hs, ms, tn_o)
            kv4s = lambda a: a[:, :ts].reshape(bs, ts, KVH_D, HEAD_DIM)
            new_pool = jnp.concatenate([state_pool[j], zs3[:, :ts, oo["xc"]:oo["xc"] + d_c]], axis=1)[:, -(POOL_MAX - 1):]
            kwn = kv4s(knew[:, :, 2 * kvw_d:3 * kvw_d])
            vwn = kv4s(zs3[:, :, oo["vw"]:oo["vw"] + kvw_d])
            od_s.append((new_pool,
                         kv4s(knew[:, :, 0:kvw_d]), kv4s(zs3[:, :, oo["vc"]:oo["vc"] + kvw_d]),
                         kv4s(knew[:, :, kvw_d:2 * kvw_d]), kv4s(zs3[:, :, oo["vs"]:oo["vs"] + kvw_d]),
                         jnp.concatenate([cache_nsa_win_k[j], kwn], axis=1)[:, -wlen:],
                         jnp.concatenate([cache_nsa_win_v[j], vwn], axis=1)[:, -wlen:]))

    y_prompt = final_norm(hp, final_g, tm_p).reshape(bp, tp, d)
    y_sample = final_norm(hs, final_g, ms).reshape(bs, SROWS, d)[:, :ts]

    def field(lst, k):
        return jnp.stack([st[k] for st in lst], axis=0)

    return (y_prompt, y_sample,
            field(ev_p, 0), field(ev_p, 1), field(ev_p, 2), field(ev_p, 3),
            field(od_p, 0), field(od_p, 1), field(od_p, 2), field(od_p, 3), field(od_p, 4), field(od_p, 5), field(od_p, 6),
            field(ev_s, 0), field(ev_s, 1), field(ev_s, 2), field(ev_s, 3),
            field(od_s, 0), field(od_s, 1), field(od_s, 2), field(od_s, 3), field(od_s, 4), field(od_s, 5), field(od_s, 6))
```

```python
import functools

import jax
import jax.numpy as jnp
import numpy as np
from jax import lax
from jax.experimental import pallas as pl
from jax.experimental.pallas import tpu as pltpu

F32 = jnp.float32
BF16 = jnp.bfloat16

LANES = 128
SUBLANES = 8
VMEM_LIMIT = 48 * 1024 * 1024

HEAD_DIM = 128
ATTN_SCALE = HEAD_DIM ** -0.5
ROPE_THETA = 10000.0
NORM_EPS = 1e-6
PAGE_SIZE = 128
CONV_W = 4
LRU_C = 8.0
KVH_B = 2
WIN_B = 128
POOL_WINDOWS = (2, 4, 8, 16)
POOL_MAX = 16
KVH_D = 2
CMP_BLOCK = 64
CMP_SHIFT = 6
N_SEL = 16
WIN_D = 512
FORCE_SCORE = 1e6
DENOM_FLOOR = 1e-30
NEG = -1e30
SROWS = 8


def _cparams(sem):
    return pltpu.CompilerParams(dimension_semantics=sem, vmem_limit_bytes=VMEM_LIMIT)


def _dot(a, b):
    return jnp.dot(a, b, preferred_element_type=F32)


def _dot_nt(a, b):
    return lax.dot_general(a, b, (((1,), (1,)), ((), ())), preferred_element_type=F32)


def _silu(x):
    return x * jax.nn.sigmoid(x)


def _rope(x, c, s):
    return x * c + pltpu.roll(x, HEAD_DIM // 2, 1) * s


def _rope_tab_kernel(inv_ref, sgn_ref, c_ref, s_ref, *, start, rows):
    i = pl.program_id(0)
    pos = start + i * rows + lax.broadcasted_iota(jnp.int32, (rows, LANES), 0)
    ang = pos.astype(F32) * inv_ref[...]
    c_ref[...] = jnp.cos(ang)
    s_ref[...] = jnp.sin(ang) * sgn_ref[...]


def rope_tables(n_rows, start):
    half = HEAD_DIM // 2
    inv = ROPE_THETA ** (-jnp.arange(half, dtype=F32) / half)
    inv2 = jnp.concatenate([inv, inv]).reshape(1, LANES)
    sgn = jnp.concatenate([-jnp.ones((half,), F32), jnp.ones((half,), F32)]).reshape(1, LANES)
    rows = min(n_rows, 512)
    assert n_rows % rows == 0
    return pl.pallas_call(
        functools.partial(_rope_tab_kernel, start=start, rows=rows),
        out_shape=(jax.ShapeDtypeStruct((n_rows, LANES), F32),) * 2,
        grid=(n_rows // rows,),
        in_specs=[pl.BlockSpec((1, LANES), lambda i: (0, 0))] * 2,
        out_specs=(pl.BlockSpec((rows, LANES), lambda i: (i, 0)),) * 2,
        compiler_params=_cparams(("arbitrary",)),
        name="rope_tables",
    )(inv2, sgn)


def _norm_proj_kernel(x_ref, g_ref, w_ref, o_ref, xn_ref):
    @pl.when(pl.program_id(1) == 0)
    def _():
        x = x_ref[...]
        ms = jnp.mean(x * x, axis=-1, keepdims=True)
        xn_ref[...] = (x * lax.rsqrt(ms + NORM_EPS) * g_ref[...]).astype(BF16)

    o_ref[...] = _dot(xn_ref[...], w_ref[...])


def norm_proj(x, g, w, tm, tn):
    m, d = x.shape
    n = w.shape[1]
    assert m % tm == 0 and n % tn == 0
    return pl.pallas_call(
        _norm_proj_kernel,
        out_shape=jax.ShapeDtypeStruct((m, n), F32),
        grid=(m // tm, n // tn),
        in_specs=[pl.BlockSpec((tm, d), lambda i, j: (i, 0)),
                  pl.BlockSpec((1, d), lambda i, j: (0, 0)),
                  pl.BlockSpec((d, tn), lambda i, j: (0, j))],
        out_specs=pl.BlockSpec((tm, tn), lambda i, j: (i, j)),
        scratch_shapes=[pltpu.VMEM((tm, d), BF16)],
        compiler_params=_cparams(("arbitrary", "arbitrary")),
        name="norm_proj",
    )(x, g.reshape(1, d), w)


def _out_proj_kernel(ya_ref, yb_ref, wa_ref, wb_ref, r_ref, o_ref):
    o_ref[...] = r_ref[...] + (_dot(ya_ref[...], wa_ref[...]) + _dot(yb_ref[...], wb_ref[...]))


def out_proj(ya, yb, wa, wb, resid, tm, tn):
    m, ka = ya.shape
    kb = yb.shape[1]
    n = wa.shape[1]
    assert m % tm == 0 and n % tn == 0
    return pl.pallas_call(
        _out_proj_kernel,
        out_shape=jax.ShapeDtypeStruct((m, n), F32),
        grid=(m // tm, n // tn),
        in_specs=[pl.BlockSpec((tm, ka), lambda i, j: (i, 0)),
                  pl.BlockSpec((tm, kb), lambda i, j: (i, 0)),
                  pl.BlockSpec((ka, tn), lambda i, j: (0, j)),
                  pl.BlockSpec((kb, tn), lambda i, j: (0, j)),
                  pl.BlockSpec((tm, tn), lambda i, j: (i, j))],
        out_specs=pl.BlockSpec((tm, tn), lambda i, j: (i, j)),
        compiler_params=_cparams(("arbitrary", "arbitrary")),
        name="out_proj",
    )(ya, yb, wa, wb, resid)


def _final_norm_kernel(x_ref, g_ref, o_ref):
    x = x_ref[...]
    ms = jnp.mean(x * x, axis=-1, keepdims=True)
    o_ref[...] = x * lax.rsqrt(ms + NORM_EPS) * g_ref[...]


def final_norm(x, g, tm):
    m, d = x.shape
    return pl.pallas_call(
        _final_norm_kernel,
        out_shape=jax.ShapeDtypeStruct((m, d), F32),
        grid=(m // tm,),
        in_specs=[pl.BlockSpec((tm, d), lambda i: (i, 0)), pl.BlockSpec((1, d), lambda i: (0, 0))],
        out_specs=pl.BlockSpec((tm, d), lambda i: (i, 0)),
        compiler_params=_cparams(("arbitrary",)),
        name="final_norm",
    )(x, g.reshape(1, d))


def _lru_gates(xs, wr, br, wi, bi, sp):
    xb = xs.astype(BF16)
    r = jax.nn.sigmoid(_dot(xb, wr) + br)
    ig = jax.nn.sigmoid(_dot(xb, wi) + bi)
    log_a = -LRU_C * r * sp
    a = jnp.exp(log_a)
    u = jnp.sqrt(-jnp.tanh(log_a) * (a * a + 1.0)) * (ig * xs)
    return a, u


def _softplus(z):
    return jnp.maximum(z, 0.0) + jnp.log1p(jnp.exp(-jnp.abs(z)))


def _lru_prompt_kernel(xa_ref, ga_ref, cw_ref, cb_ref, wr_ref, br_ref, wi_ref, bi_ref, lam_ref,
                       ya_ref, hl_ref, h_sc, halo_sc, *, tt, nblk):
    i = pl.program_id(1)

    @pl.when(i == 0)
    def _():
        h_sc[...] = jnp.zeros_like(h_sc)
        halo_sc[...] = jnp.zeros_like(halo_sc)

    x = xa_ref[...]
    xe = jnp.concatenate([halo_sc[...], x], axis=0)
    halo_sc[...] = x[tt - SUBLANES:, :]
    cw = cw_ref[...]
    y = cb_ref[...] + xe[SUBLANES:, :] * cw[CONV_W - 1:CONV_W, :]
    for k in range(CONV_W - 1):
        y = y + pltpu.roll(xe, CONV_W - 1 - k, 0)[SUBLANES:, :] * cw[k:k + 1, :]
    sp = _softplus(-lam_ref[...])
    rid = lax.broadcasted_iota(jnp.int32, (tt, LANES), 0) & (SUBLANES - 1)
    for n in range(nblk):
        sl = slice(n * LANES, (n + 1) * LANES)
        a, u = _lru_gates(y[:, sl], wr_ref[n], br_ref[:, sl], wi_ref[n], bi_ref[:, sl], sp[:, sl])
        for s in (1, 2, 4):
            keep = rid >= s
            u = jnp.where(keep, a * pltpu.roll(u, s, 0) + u, u)
            a = jnp.where(keep, a * pltpu.roll(a, s, 0), a)
        h = h_sc[:, sl]
        outs = []
        for g in range(tt // SUBLANES):
            rows = slice(g * SUBLANES, (g + 1) * SUBLANES)
            hg = u[rows, :] + a[rows, :] * h
            outs.append(hg)
            h = hg[SUBLANES - 1:, :]
        h_sc[:, sl] = h
        hs = jnp.concatenate(outs, axis=0)
        ya_ref[:, sl] = (hs * _silu(ga_ref[:, sl])).astype(BF16)
    hl_ref[0] = h_sc[...]


def lru_prompt(z, bsz, t, d_a, cw, cb, wr, br, wi, bi, lam, tt=256):
    nt = t // tt
    nblk = d_a // LANES
    row = lambda b, i: (b * nt + i, 0)
    vec = pl.BlockSpec((1, d_a), lambda b, i: (0, 0))
    wspec = pl.BlockSpec((nblk, LANES, LANES), lambda b, i: (0, 0, 0))
    return pl.pallas_call(
        functools.partial(_lru_prompt_kernel, tt=tt, nblk=nblk),
        out_shape=(jax.ShapeDtypeStruct((bsz * t, d_a), BF16), jax.ShapeDtypeStruct((bsz, 1, d_a), F32)),
        grid=(bsz, nt),
        in_specs=[pl.BlockSpec((tt, d_a), row),
                  pl.BlockSpec((tt, d_a), lambda b, i: (b * nt + i, 1)),
                  pl.BlockSpec((CONV_W, d_a), lambda b, i: (0, 0)), vec, wspec, vec, wspec, vec, vec],
        out_specs=(pl.BlockSpec((tt, d_a), row), pl.BlockSpec((1, 1, d_a), lambda b, i: (b, 0, 0))),
        scratch_shapes=[pltpu.VMEM((1, d_a), F32), pltpu.VMEM((SUBLANES, d_a), F32)],
        compiler_params=_cparams(("arbitrary", "arbitrary")),
        name="lru_prompt",
    )(z, z, cw, cb.reshape(1, d_a), wr, br.reshape(1, d_a), wi, bi.reshape(1, d_a), lam.reshape(1, d_a))


def _lru_sample_kernel(x_ref, ga_ref, c0_ref, h0_ref, cw_ref, cb_ref, wr_ref, br_ref, wi_ref, bi_ref, lam_ref,
                       ya_ref, hl_ref, *, ts, nblk):
    cw = cw_ref[...]
    sp = _softplus(-lam_ref[...])
    xe = [c0_ref[k] for k in range(CONV_W - 1)] + [x_ref[t] for t in range(ts)]
    h = h0_ref[...]
    for t in range(ts):
        y = cb_ref[...]
        for k in range(CONV_W):
            y = y + xe[t + k] * cw[k:k + 1, :]
        a_l, u_l = [], []
        for n in range(nblk):
            sl = slice(n * LANES, (n + 1) * LANES)
            a, u = _lru_gates(y[:, sl], wr_ref[n], br_ref[:, sl], wi_ref[n], bi_ref[:, sl], sp[:, sl])
            a_l.append(a)
            u_l.append(u)
        h = jnp.concatenate(a_l, axis=1) * h + jnp.concatenate(u_l, axis=1)
        ya_ref[t] = (h * _silu(ga_ref[t])).astype(BF16)
    for t in range(ts, SROWS):
        ya_ref[t] = jnp.zeros(ya_ref.shape[1:], BF16)
    hl_ref[...] = h


def lru_sample(xa_t, ga_t, conv_t, h0, cw, cb, wr, br, wi, bi, lam, ts):
    _, bsz, d_a = xa_t.shape
    nblk = d_a // LANES
    return pl.pallas_call(
        functools.partial(_lru_sample_kernel, ts=ts, nblk=nblk),
        out_shape=(jax.ShapeDtypeStruct((SROWS, bsz, d_a), BF16), jax.ShapeDtypeStruct((bsz, d_a), F32)),
        compiler_params=pltpu.CompilerParams(vmem_limit_bytes=VMEM_LIMIT),
        name="lru_sample",
    )(xa_t, ga_t, conv_t, h0, cw, cb.reshape(1, d_a), wr, br.reshape(1, d_a), wi, bi.reshape(1, d_a),
      lam.reshape(1, d_a))


def _swa_prompt_kernel(sink_ref, q_ref, kp_ref, kc_ref, vp_ref, vc_ref, gb_ref, cq_ref, sq_ref, cp_ref, sp_ref,
                       yb_ref, kout_ref, *, nq, gq, tq):
    kv = pl.program_id(1)
    i = pl.program_id(2)
    cq = cq_ref[...]
    sq = sq_ref[...]
    k_cur = _rope(kc_ref[...], cq, sq)
    k_prev = _rope(kp_ref[...], cp_ref[...], sp_ref[...])

    @pl.when(i == nq - 1)
    def _():
        kout_ref[0, 0] = k_cur

    k = jnp.concatenate([k_prev, k_cur], axis=0).astype(BF16)
    v = jnp.concatenate([vp_ref[...], vc_ref[...]], axis=0).astype(BF16)
    r = lax.broadcasted_iota(jnp.int32, (tq, 2 * tq), 0)
    c = lax.broadcasted_iota(jnp.int32, (tq, 2 * tq), 1)
    valid = (c > r) & (c <= r + WIN_B) & ((c >= tq) | (i > 0))
    for g in range(gq):
        sl = slice(g * HEAD_DIM, (g + 1) * HEAD_DIM)
        qg = _rope(q_ref[:, sl], cq, sq).astype(BF16)
        s = _dot_nt(qg, k) * ATTN_SCALE
        sink = sink_ref[kv * gq + g]
        m = jnp.maximum(jnp.max(jnp.where(valid, s, NEG), axis=1, keepdims=True), sink)
        p = jnp.where(valid, jnp.exp(s - m), 0.0)
        den = jnp.sum(p, axis=1, keepdims=True) + jnp.exp(sink - m)
        p = p * (1.0 / jnp.maximum(den, DENOM_FLOOR))
        o = _dot(p.astype(BF16), v)
        yb_ref[:, sl] = (o * _silu(gb_ref[:, sl])).astype(BF16)


def swa_prompt(z, sinks, ctab, stab, bsz, t, q_off, k_off, v_off, g_off, n_heads):
    tq = WIN_B
    nq = t // tq
    gq = n_heads // KVH_B
    wq = gq * HEAD_DIM
    assert q_off % wq == 0 and g_off % wq == 0 and k_off % HEAD_DIM == 0 and v_off % HEAD_DIM == 0
    prev = lambda i: jnp.maximum(i - 1, 0)
    in_specs = [
        pl.BlockSpec(memory_space=pltpu.SMEM),
        pl.BlockSpec((tq, wq), lambda b, h, i: (b * nq + i, q_off // wq + h)),
        pl.BlockSpec((tq, HEAD_DIM), lambda b, h, i: (b * nq + prev(i), k_off // HEAD_DIM + h)),
        pl.BlockSpec((tq, HEAD_DIM), lambda b, h, i: (b * nq + i, k_off // HEAD_DIM + h)),
        pl.BlockSpec((tq, HEAD_DIM), lambda b, h, i: (b * nq + prev(i), v_off // HEAD_DIM + h)),
        pl.BlockSpec((tq, HEAD_DIM), lambda b, h, i: (b * nq + i, v_off // HEAD_DIM + h)),
        pl.BlockSpec((tq, wq), lambda b, h, i: (b * nq + i, g_off // wq + h)),
        pl.BlockSpec((tq, LANES), lambda b, h, i: (i, 0)),
        pl.BlockSpec((tq, LANES), lambda b, h, i: (i, 0)),
        pl.BlockSpec((tq, LANES), lambda b, h, i: (prev(i), 0)),
        pl.BlockSpec((tq, LANES), lambda b, h, i: (prev(i), 0)),
    ]
    return pl.pallas_call(
        functools.partial(_swa_prompt_kernel, nq=nq, gq=gq, tq=tq),
        out_shape=(jax.ShapeDtypeStruct((bsz * t, n_heads * HEAD_DIM), BF16),
                   jax.ShapeDtypeStruct((bsz, KVH_B, tq, HEAD_DIM), F32)),
        grid=(bsz, KVH_B, nq),
        in_specs=in_specs,
        out_specs=(pl.BlockSpec((tq, wq), lambda b, h, i: (b * nq + i, h)),
                   pl.BlockSpec((1, 1, tq, HEAD_DIM), lambda b, h, i: (b, h, 0, 0))),
        compiler_params=_cparams(("arbitrary", "arbitrary", "arbitrary")),
        name="swa_prompt",
    )(sinks, z, z, z, z, z, z, ctab, stab, ctab, stab)


def _swa_sample_kernel(sink_ref, z_ref, kc_ref, vc_ref, c_ref, s_ref, yb_ref, kn_ref, *,
                       gq, q_off, k_off, v_off, g_off, ts, wlen):
    c = c_ref[...]
    s = s_ref[...]
    qi = lax.broadcasted_iota(jnp.int32, (SROWS, wlen + SROWS), 0)
    ci = lax.broadcasted_iota(jnp.int32, (SROWS, wlen + SROWS), 1)
    diff = jnp.where(ci < wlen, qi + wlen - ci, qi - (ci - wlen))
    valid = (diff >= 0) & (diff < WIN_B)
    for h in range(KVH_B):
        ks = slice(k_off + h * HEAD_DIM, k_off + (h + 1) * HEAD_DIM)
        vs = slice(v_off + h * HEAD_DIM, v_off + (h + 1) * HEAD_DIM)
        hs = slice(h * HEAD_DIM, (h + 1) * HEAD_DIM)
        k_new = _rope(z_ref[0, :, ks], c, s)
        kn_ref[0, :, hs] = k_new
        k = jnp.concatenate([kc_ref[0, pl.ds(h, wlen, stride=KVH_B), :], k_new], axis=0).astype(BF16)
        v = jnp.concatenate([vc_ref[0, pl.ds(h, wlen, stride=KVH_B), :], z_ref[0, :, vs]], axis=0).astype(BF16)
        for g in range(gq):
            hd = h * gq + g
            qg = _rope(z_ref[0, :, q_off + hd * HEAD_DIM:q_off + (hd + 1) * HEAD_DIM], c, s).astype(BF16)
            sc = _dot_nt(qg, k) * ATTN_SCALE
            sink = sink_ref[hd]
            m = jnp.maximum(jnp.max(jnp.where(valid, sc, NEG), axis=1, keepdims=True), sink)
            p = jnp.where(valid, jnp.exp(sc - m), 0.0)
            den = jnp.sum(p, axis=1, keepdims=True) + jnp.exp(sink - m)
            p = p * (1.0 / jnp.maximum(den, DENOM_FLOOR))
            o = _dot(p.astype(BF16), v)
            gate = _silu(z_ref[0, :, g_off + hd * HEAD_DIM:g_off + (hd + 1) * HEAD_DIM])
            yb_ref[0, :, hd * HEAD_DIM:(hd + 1) * HEAD_DIM] = (o * gate).astype(BF16)


def swa_sample(z3, sinks, cache_k, cache_v, layer, ctab, stab, q_off, k_off, v_off, g_off, n_heads, ts):
    bsz, _, p = z3.shape
    wlen = cache_k.shape[1] // KVH_B
    kvw = KVH_B * HEAD_DIM
    return pl.pallas_call(
        functools.partial(_swa_sample_kernel, gq=n_heads // KVH_B, q_off=q_off, k_off=k_off, v_off=v_off,
                          g_off=g_off, ts=ts, wlen=wlen),
        out_shape=(jax.ShapeDtypeStruct((bsz, SROWS, n_heads * HEAD_DIM), BF16),
                   jax.ShapeDtypeStruct((bsz, SROWS, kvw), F32)),
        grid=(bsz,),
        in_specs=[pl.BlockSpec(memory_space=pltpu.SMEM),
                  pl.BlockSpec((1, SROWS, p), lambda b: (b, 0, 0)),
                  pl.BlockSpec((1, wlen * KVH_B, HEAD_DIM), lambda b: (layer * bsz + b, 0, 0)),
                  pl.BlockSpec((1, wlen * KVH_B, HEAD_DIM), lambda b: (layer * bsz + b, 0, 0)),
                  pl.BlockSpec((SROWS, LANES), lambda b: (0, 0)),
                  pl.BlockSpec((SROWS, LANES), lambda b: (0, 0))],
        out_specs=(pl.BlockSpec((1, SROWS, n_heads * HEAD_DIM), lambda b: (b, 0, 0)),
                   pl.BlockSpec((1, SROWS, kvw), lambda b: (b, 0, 0))),
        compiler_params=_cparams(("arbitrary",)),
        name="swa_sample",
    )(sinks, z3, cache_k, cache_v, ctab, stab)


def _pool_prompt_kernel(x_ref, gc_ref, w_ref, sc_ref, y_ref, halo_sc, *, tt, ngrp):
    i = pl.program_id(1)

    @pl.when(i == 0)
    def _():
        halo_sc[...] = jnp.zeros_like(halo_sc)

    x = x_ref[...]
    xe = jnp.concatenate([halo_sc[...], x], axis=0)
    halo_sc[...] = x[tt - POOL_MAX:, :]
    pos1 = (i * tt + 1 + lax.broadcasted_iota(jnp.int32, (tt, LANES), 0)).astype(F32)
    for g in range(ngrp):
        sl = slice(g * LANES, (g + 1) * LANES)
        w = POOL_WINDOWS[g]
        s = xe[:, sl]
        step = 1
        while step < w:
            s = s + pltpu.roll(s, step, 0)
            step *= 2
        pooled = s[POOL_MAX:, :] / jnp.minimum(float(w), pos1) - x[:, sl]
        y = _dot(pooled.astype(BF16), w_ref[g]) * sc_ref[:, sl]
        y_ref[:, sl] = (y * _silu(gc_ref[:, sl])).astype(BF16)


def pool_prompt(z, bsz, t, d_c, x_off, g_off, w_pool, scale, tt=256):
    nt = t // tt
    ngrp = d_c // LANES
    assert x_off % d_c == 0 and g_off % d_c == 0
    return pl.pallas_call(
        functools.partial(_pool_prompt_kernel, tt=tt, ngrp=ngrp),
        out_shape=jax.ShapeDtypeStruct((bsz * t, d_c), BF16),
        grid=(bsz, nt),
        in_specs=[pl.BlockSpec((tt, d_c), lambda b, i: (b * nt + i, x_off // d_c)),
                  pl.BlockSpec((tt, d_c), lambda b, i: (b * nt + i, g_off // d_c)),
                  pl.BlockSpec((ngrp, LANES, LANES), lambda b, i: (0, 0, 0)),
                  pl.BlockSpec((1, d_c), lambda b, i: (0, 0))],
        out_specs=pl.BlockSpec((tt, d_c), lambda b, i: (b * nt + i, 0)),
        scratch_shapes=[pltpu.VMEM((POOL_MAX, d_c), F32)],
        compiler_params=_cparams(("arbitrary", "arbitrary")),
        name="pool_prompt",
    )(z, z, w_pool, scale.reshape(1, d_c))


def _pool_sample_kernel(x_ref, gc_ref, buf_ref, w_ref, sc_ref, y_ref, *, ts, ngrp, start_pos):
    nbuf = POOL_MAX - 1
    xe = [buf_ref[k] for k in range(nbuf)] + [x_ref[t] for t in range(ts)]
    for t in range(ts):
        cols = []
        for g in range(ngrp):
            sl = slice(g * LANES, (g + 1) * LANES)
            w = POOL_WINDOWS[g]
            s = xe[nbuf + t][:, sl]
            for k in range(1, w):
                s = s + xe[nbuf + t - k][:, sl]
            pooled = s / min(float(w), float(start_pos + t + 1)) - xe[nbuf + t][:, sl]
            cols.append(_dot(pooled.astype(BF16), w_ref[g]))
        y = jnp.concatenate(cols, axis=1) * sc_ref[...]
        y_ref[t] = (y * _silu(gc_ref[t])).astype(BF16)
    for t in range(ts, SROWS):
        y_ref[t] = jnp.zeros(y_ref.shape[1:], BF16)


def pool_sample(xc_t, gc_t, buf_t, w_pool, scale, ts, start_pos):
    _, bsz, d_c = xc_t.shape
    return pl.pallas_call(
        functools.partial(_pool_sample_kernel, ts=ts, ngrp=d_c // LANES, start_pos=start_pos),
        out_shape=jax.ShapeDtypeStruct((SROWS, bsz, d_c), BF16),
        compiler_params=pltpu.CompilerParams(vmem_limit_bytes=VMEM_LIMIT),
        name="pool_sample",
    )(xc_t, gc_t, buf_t, w_pool, scale.reshape(1, d_c))


def _nsa_prep_kernel(q_ref, kc_ref, vc_ref, ks_ref, vs_ref, kw_ref, vw_ref, c_ref, s_ref, cwk_ref, cwv_ref,
                     qr_ref, kcr_ref, ksr_ref, kwr_ref, kvb_ref, kcmp_ref, vcmp_ref, *, tt, nh):
    c = c_ref[...]
    s = s_ref[...]
    for h in range(nh):
        sl = slice(h * HEAD_DIM, (h + 1) * HEAD_DIM)
        qr_ref[:, sl] = _rope(q_ref[:, sl], c, s).astype(BF16)
    nblk = tt // CMP_BLOCK
    kvw = KVH_D * HEAD_DIM
    for h in range(KVH_D):
        sl = slice(h * HEAD_DIM, (h + 1) * HEAD_DIM)
        kcr = _rope(kc_ref[:, sl], c, s)
        kcr_ref[:, sl] = kcr
        ksr = _rope(ks_ref[:, sl], c, s)
        ksr_ref[:, sl] = ksr
        kwr = _rope(kw_ref[:, sl], c, s)
        kwr_ref[:, sl] = kwr
        for seg, val in enumerate((ksr, vs_ref[:, sl], kwr, vw_ref[:, sl])):
            kvb_ref[:, seg * kvw + h * HEAD_DIM:seg * kvw + (h + 1) * HEAD_DIM] = val.astype(BF16)
        kcmp_ref[0, :, sl] = jnp.sum(kcr.reshape(nblk, CMP_BLOCK, HEAD_DIM) * cwk_ref[:, sl][None], axis=1)
        vcmp_ref[0, :, sl] = jnp.sum(vc_ref[:, sl].reshape(nblk, CMP_BLOCK, HEAD_DIM) * cwv_ref[:, sl][None], axis=1)


def nsa_prep(z, ctab, stab, cwk2, cwv2, bsz, t, offs, n_heads, tt=512):
    nt = t // tt
    qw = n_heads * HEAD_DIM
    kvw = KVH_D * HEAD_DIM
    assert offs["q"] % qw == 0 and all(offs[k] % kvw == 0 for k in ("kc", "vc", "ks", "vs", "kw", "vw"))
    row = lambda b, i: (b * nt + i, 0)
    kvspec = lambda name: pl.BlockSpec((tt, kvw), lambda b, i: (b * nt + i, offs[name] // kvw))
    return pl.pallas_call(
        functools.partial(_nsa_prep_kernel, tt=tt, nh=n_heads),
        out_shape=(jax.ShapeDtypeStruct((bsz * t, qw), BF16),
                   jax.ShapeDtypeStruct((bsz * t, kvw), F32),
                   jax.ShapeDtypeStruct((bsz * t, kvw), F32),
                   jax.ShapeDtypeStruct((bsz * t, kvw), F32),
                   jax.ShapeDtypeStruct((bsz * t, 4 * kvw), BF16),
                   jax.ShapeDtypeStruct((bsz, t // CMP_BLOCK, kvw), F32),
                   jax.ShapeDtypeStruct((bsz, t // CMP_BLOCK, kvw), F32)),
        grid=(bsz, nt),
        in_specs=[pl.BlockSpec((tt, qw), lambda b, i: (b * nt + i, offs["q"] // qw)),
                  kvspec("kc"), kvspec("vc"), kvspec("ks"), kvspec("vs"), kvspec("kw"), kvspec("vw"),
                  pl.BlockSpec((tt, LANES), lambda b, i: (i, 0)),
                  pl.BlockSpec((tt, LANES), lambda b, i: (i, 0)),
                  pl.BlockSpec((CMP_BLOCK, kvw), lambda b, i: (0, 0)),
                  pl.BlockSpec((CMP_BLOCK, kvw), lambda b, i: (0, 0))],
        out_specs=(pl.BlockSpec((tt, qw), row), pl.BlockSpec((tt, kvw), row), pl.BlockSpec((tt, kvw), row),
                   pl.BlockSpec((tt, kvw), row), pl.BlockSpec((tt, 4 * kvw), row),
                   pl.BlockSpec((1, tt // CMP_BLOCK, kvw), lambda b, i: (b, i, 0)),
                   pl.BlockSpec((1, tt // CMP_BLOCK, kvw), lambda b, i: (b, i, 0))),
        compiler_params=_cparams(("arbitrary", "arbitrary")),
        name="nsa_prep",
    )(z, z, z, z, z, z, z, ctab, stab, cwk2, cwv2)


def _rank_select(score, nsel):
    nb = score.shape[0]
    jidx = lax.broadcasted_iota(jnp.int32, score.shape, 0)
    cnt = jnp.zeros(score.shape, F32)
    for k in range(nb):
        rk = score[k:k + 1, :]
        tie = jnp.where(jidx > k, 1.0, 0.0)
        cnt = cnt + jnp.where(rk > score, 1.0, jnp.where(rk == score, tie, 0.0))
    return cnt


def _masked_softmax_rows(s, valid):
    m = jnp.max(jnp.where(valid, s, NEG), axis=1, keepdims=True)
    p = jnp.where(valid, jnp.exp(s - m), 0.0)
    den = jnp.sum(p, axis=1, keepdims=True)
    return p * (1.0 / jnp.maximum(den, DENOM_FLOOR))


def _nsa_prompt_kernel(q_ref, kcmp_ref, vcmp_ref, ks_ref, vs_ref, kw_ref, vw_ref, gd_ref, gl_ref, gbias_ref,
                       o_ref, s_sc, mrun_sc, m_sc, lrun_sc, acc_sc, *, t_len, tq, tk, gq, n_heads):
    kv = pl.program_id(1)
    i = pl.program_id(2)
    nb = t_len // CMP_BLOCK
    q0 = i * tq
    bpt = tk // CMP_BLOCK

    rows = gq * tq
    qs = jnp.concatenate([q_ref[:, g * HEAD_DIM:(g + 1) * HEAD_DIM] for g in range(gq)], axis=0)
    kc = kcmp_ref[0].astype(BF16)
    vc = vcmp_ref[0].astype(BF16)

    blk_r = lax.broadcasted_iota(jnp.int32, (rows, nb), 1)
    qp_r = q0 + (lax.broadcasted_iota(jnp.int32, (rows, nb), 0) & (tq - 1))
    ok_r = (blk_r + 1) * CMP_BLOCK - 1 <= qp_r
    p_cmp = _masked_softmax_rows(_dot_nt(qs, kc) * ATTN_SCALE, ok_r)
    o_cmp = _dot(p_cmp.astype(BF16), vc)
    blk_a = lax.broadcasted_iota(jnp.int32, (nb, rows), 0)
    qp_a = q0 + (lax.broadcasted_iota(jnp.int32, (nb, rows), 1) & (tq - 1))
    ok_a = (blk_a + 1) * CMP_BLOCK - 1 <= qp_a
    st = _dot_nt(kc, qs) * ATTN_SCALE
    mt = jnp.max(jnp.where(ok_a, st, NEG), axis=0, keepdims=True)
    pt = jnp.where(ok_a, jnp.exp(st - mt), 0.0)
    dt = jnp.sum(pt, axis=0, keepdims=True)
    pt = pt * (1.0 / jnp.maximum(dt, DENOM_FLOOR))
    imp = pt[:, 0:tq]
    for g in range(1, gq):
        imp = imp + pt[:, g * tq:(g + 1) * tq]
    blk_c = lax.broadcasted_iota(jnp.int32, (nb, tq), 0)
    qp_c = q0 + lax.broadcasted_iota(jnp.int32, (nb, tq), 1)

    cur = jnp.right_shift(qp_c, CMP_SHIFT)
    forced = (blk_c == 0) | (blk_c == cur) | (blk_c == cur - 1)
    score = jnp.where(blk_c > cur, -1.0, jnp.where(forced, FORCE_SCORE, imp))
    cnt = _rank_select(score, N_SEL)
    sel_t = jnp.where((cnt < N_SEL) & (score >= 0.0), 1.0, 0.0).astype(BF16)
    eye = jnp.where(lax.broadcasted_iota(jnp.int32, (tq, tq), 0) == lax.broadcasted_iota(jnp.int32, (tq, tq), 1),
                    1.0, 0.0).astype(BF16)
    sel = _dot_nt(eye, sel_t).astype(BF16)

    n_kt = (q0 + tq + tk - 1) // tk
    qp_k = q0 + lax.broadcasted_iota(jnp.int32, (tq, tk), 0)
    col_k = lax.broadcasted_iota(jnp.int32, (tq, tk), 1)
    e_row = lax.broadcasted_iota(jnp.int32, (nb, tk), 0)
    e_col = jnp.right_shift(lax.broadcasted_iota(jnp.int32, (nb, tk), 1), CMP_SHIFT)
    nlt = tk // LANES
    mrun_sc[...] = jnp.full(mrun_sc.shape, NEG, F32)

    def score_pass(kt, carry):
        k0 = pl.multiple_of(kt * tk, tk)
        s = _dot_nt(qs, ks_ref[pl.ds(k0, tk), :])
        expand = jnp.where(e_row == kt * bpt + e_col, 1.0, 0.0).astype(BF16)
        ok = (_dot(sel, expand) > 0.5) & (k0 + col_k <= qp_k)
        bias = jnp.where(ok, 0.0, NEG)
        sb = s * ATTN_SCALE + jnp.concatenate([bias] * gq, axis=0)
        s_sc[kt] = sb
        mx = sb[:, 0:LANES]
        for c in range(1, nlt):
            mx = jnp.maximum(mx, sb[:, c * LANES:(c + 1) * LANES])
        mrun_sc[...] = jnp.maximum(mrun_sc[...], mx)
        return carry

    lax.fori_loop(0, n_kt, score_pass, 0)
    m_sc[...] = jnp.broadcast_to(jnp.max(mrun_sc[...], axis=1, keepdims=True), m_sc.shape)
    lrun_sc[...] = jnp.zeros(lrun_sc.shape, F32)
    acc_sc[...] = jnp.zeros(acc_sc.shape, F32)

    def value_pass(kt, carry):
        k0 = pl.multiple_of(kt * tk, tk)
        sb = s_sc[kt]
        mrep = m_sc[...]
        ps = [jnp.exp(sb[:, c * LANES:(c + 1) * LANES] - mrep) for c in range(nlt)]
        lsum = ps[0]
        for c in range(1, nlt):
            lsum = lsum + ps[c]
        lrun_sc[...] = lrun_sc[...] + lsum
        p = jnp.concatenate(ps, axis=1).astype(BF16)
        acc_sc[...] = acc_sc[...] + _dot(p, vs_ref[pl.ds(k0, tk), :])
        return carry

    lax.fori_loop(0, n_kt, value_pass, 0)
    o_sel = acc_sc[...] * (1.0 / jnp.sum(lrun_sc[...], axis=1, keepdims=True))

    nwb = (WIN_D + tq - 2) // tq + 1
    wlen = nwb * tq
    w0 = pl.multiple_of(jnp.maximum(i - (nwb - 1), 0) * tq, tq)
    dw = (q0 + lax.broadcasted_iota(jnp.int32, (tq, wlen), 0)) - (w0 + lax.broadcasted_iota(jnp.int32, (tq, wlen), 1))
    bias_w = jnp.where((dw >= 0) & (dw < WIN_D), 0.0, NEG)
    sw = _dot_nt(qs, kw_ref[pl.ds(w0, wlen), :]) * ATTN_SCALE + jnp.concatenate([bias_w] * gq, axis=0)
    pw = jnp.exp(sw - jnp.max(sw, axis=1, keepdims=True))
    lw = jnp.sum(pw, axis=1, keepdims=True)
    o_win = _dot(pw.astype(BF16), vw_ref[pl.ds(w0, wlen), :]) * (1.0 / lw)

    gate = jax.nn.sigmoid(gl_ref[...] + gbias_ref[...])
    for g in range(gq):
        rs = slice(g * tq, (g + 1) * tq)
        gs = []
        for br in range(3):
            c0 = br * n_heads + g
            c1 = c0 + gq
            gs.append(jnp.where(kv == 0, gate[:, c0:c0 + 1], gate[:, c1:c1 + 1]))
        yd = gs[0] * o_cmp[rs, :] + gs[1] * o_sel[rs, :] + gs[2] * o_win[rs, :]
        sl = slice(g * HEAD_DIM, (g + 1) * HEAD_DIM)
        o_ref[:, sl] = (yd * _silu(gd_ref[:, sl])).astype(BF16)


def nsa_prompt(z, qr, kcmp, vcmp, kvb, gbias, bsz, t, offs, n_heads, tq=128, tk=512):
    nq = t // tq
    gq = n_heads // KVH_D
    wq = gq * HEAD_DIM
    nb = t // CMP_BLOCK
    assert offs["gd"] % wq == 0 and offs["gl"] % LANES == 0 and t % tk == 0
    assert KVH_D == 2
    res = lambda seg: pl.BlockSpec((t, HEAD_DIM), lambda b, h, i: (b, seg * KVH_D + h))
    rows = gq * tq
    stat = pltpu.VMEM((rows, LANES), F32)
    return pl.pallas_call(
        functools.partial(_nsa_prompt_kernel, t_len=t, tq=tq, tk=tk, gq=gq, n_heads=n_heads),
        out_shape=jax.ShapeDtypeStruct((bsz * t, n_heads * HEAD_DIM), BF16),
        grid=(bsz, KVH_D, nq),
        in_specs=[pl.BlockSpec((tq, wq), lambda b, h, i: (b * nq + i, h)),
                  pl.BlockSpec((1, nb, HEAD_DIM), lambda b, h, i: (b, 0, h)),
                  pl.BlockSpec((1, nb, HEAD_DIM), lambda b, h, i: (b, 0, h)),
                  res(0), res(1), res(2), res(3),
                  pl.BlockSpec((tq, wq), lambda b, h, i: (b * nq + i, offs["gd"] // wq + h)),
                  pl.BlockSpec((tq, LANES), lambda b, h, i: (b * nq + i, offs["gl"] // LANES)),
                  pl.BlockSpec((1, LANES), lambda b, h, i: (0, 0))],
        out_specs=pl.BlockSpec((tq, wq), lambda b, h, i: (b * nq + i, h)),
        scratch_shapes=[pltpu.VMEM((t // tk, rows, tk), F32), stat, stat, stat, stat],
        compiler_params=_cparams(("arbitrary", "arbitrary", "arbitrary")),
        name="nsa_prompt",
    )(qr, kcmp, vcmp, kvb, kvb, kvb, kvb, z, z, gbias)


def _cmp_stream_kernel(pt_ref, kpool, vpool, cwk_ref, cwv_ref, kcmp_ref, vcmp_ref, kbuf, vbuf, sem, part_sc, *,
                       ppc, nch, nsteps, page_base, npages):
    b = pl.program_id(0)
    c = pl.program_id(1)
    step = b * nch + c
    slot = lax.rem(step, 2)

    def copies(bb, cc, sl):
        out = []
        for p in range(ppc):
            page = page_base + pt_ref[bb * npages + cc * ppc + p]
            out.append(pltpu.make_async_copy(kpool.at[page], kbuf.at[sl, p], sem.at[0, sl]))
            out.append(pltpu.make_async_copy(vpool.at[page], vbuf.at[sl, p], sem.at[1, sl]))
        return out

    @pl.when(step == 0)
    def _():
        for cp in copies(0, 0, 0):
            cp.start()

    @pl.when(step + 1 < nsteps)
    def _():
        nxt = step + 1
        for cp in copies(nxt // nch, lax.rem(nxt, nch), 1 - slot):
            cp.start()

    for cp in copies(b, c, slot):
        cp.wait()

    nblk = ppc * (PAGE_SIZE // CMP_BLOCK)
    vregs_per_blk = CMP_BLOCK * KVH_D // SUBLANES
    for buf, w_ref, out_ref in ((kbuf, cwk_ref, kcmp_ref), (vbuf, cwv_ref, vcmp_ref)):
        x = buf[slot] * w_ref[...][None]
        part_sc[...] = jnp.sum(x.reshape(nblk, vregs_per_blk, SUBLANES, LANES), axis=1).reshape(nblk * SUBLANES, LANES)
        for h in range(KVH_D):
            acc = part_sc[pl.ds(h, nblk, stride=SUBLANES), :]
            for k in range(1, SUBLANES // KVH_D):
                acc = acc + part_sc[pl.ds(k * KVH_D + h, nblk, stride=SUBLANES), :]
            out_ref[0, :, h * HEAD_DIM:(h + 1) * HEAD_DIM] = acc


def cmp_stream(pt_flat, kpool, vpool, cwk_page, cwv_page, bsz, npages, page_base, ppc=16):
    prow = kpool.shape[1]
    width = KVH_D * HEAD_DIM
    nch = npages // ppc
    bpp = PAGE_SIZE // CMP_BLOCK
    grid_spec = pltpu.PrefetchScalarGridSpec(
        num_scalar_prefetch=1,
        grid=(bsz, nch),
        in_specs=[pl.BlockSpec(memory_space=pl.ANY), pl.BlockSpec(memory_space=pl.ANY),
                  pl.BlockSpec((prow, LANES), lambda b, c, pt: (0, 0)),
                  pl.BlockSpec((prow, LANES), lambda b, c, pt: (0, 0))],
        out_specs=(pl.BlockSpec((1, ppc * bpp, width), lambda b, c, pt: (b, c, 0)),) * 2,
        scratch_shapes=[pltpu.VMEM((2, ppc, prow, LANES), F32), pltpu.VMEM((2, ppc, prow, LANES), F32),
                        pltpu.SemaphoreType.DMA((2, 2)), pltpu.VMEM((ppc * bpp * SUBLANES, LANES), F32)],
    )
    return pl.pallas_call(
        functools.partial(_cmp_stream_kernel, ppc=ppc, nch=nch, nsteps=bsz * nch, page_base=page_base,
                          npages=npages),
        out_shape=(jax.ShapeDtypeStruct((bsz, npages * bpp, width), F32),) * 2,
        grid_spec=grid_spec,
        compiler_params=_cparams(("arbitrary", "arbitrary")),
        name="cmp_stream",
    )(pt_flat, kpool, vpool, cwk_page, cwv_page)


def _nsa_sample_cmp_kernel(z_ref, kcp_ref, vcp_ref, c_ref, s_ref, cwk_ref, cwv_ref,
                           ocmp_ref, imp_ref, knew_ref, *, offs, gq, ts, n_past, past):
    c = c_ref[...]
    s = s_ref[...]
    nbp = n_past + SUBLANES
    blk_r = lax.broadcasted_iota(jnp.int32, (SROWS, nbp), 1)
    qp_r = past + lax.broadcasted_iota(jnp.int32, (SROWS, nbp), 0)
    ok_r = (blk_r + 1) * CMP_BLOCK - 1 <= qp_r
    blk_c = lax.broadcasted_iota(jnp.int32, (nbp, LANES), 0)
    qp_c = past + lax.broadcasted_iota(jnp.int32, (nbp, LANES), 1)
    ok_c = (blk_c + 1) * CMP_BLOCK - 1 <= qp_c
    row8 = lax.broadcasted_iota(jnp.int32, (SROWS, HEAD_DIM), 0)
    is_new = row8 < ts
    qpad = jnp.zeros((LANES - SROWS, HEAD_DIM), BF16)
    kvw = KVH_D * HEAD_DIM
    for h in range(KVH_D):
        hs = slice(h * HEAD_DIM, (h + 1) * HEAD_DIM)
        kcr = _rope(z_ref[0, :, offs["kc"] + h * HEAD_DIM:offs["kc"] + (h + 1) * HEAD_DIM], c, s)
        ksr = _rope(z_ref[0, :, offs["ks"] + h * HEAD_DIM:offs["ks"] + (h + 1) * HEAD_DIM], c, s)
        kwr = _rope(z_ref[0, :, offs["kw"] + h * HEAD_DIM:offs["kw"] + (h + 1) * HEAD_DIM], c, s)
        knew_ref[0, :, h * HEAD_DIM:(h + 1) * HEAD_DIM] = kcr
        knew_ref[0, :, kvw + h * HEAD_DIM:kvw + (h + 1) * HEAD_DIM] = ksr
        knew_ref[0, :, 2 * kvw + h * HEAD_DIM:2 * kvw + (h + 1) * HEAD_DIM] = kwr
        vcn = z_ref[0, :, offs["vc"] + h * HEAD_DIM:offs["vc"] + (h + 1) * HEAD_DIM]
        nk = jnp.sum(jnp.where(is_new, kcr * cwk_ref[0:SROWS, hs], 0.0), axis=0, keepdims=True)
        nv = jnp.sum(jnp.where(is_new, vcn * cwv_ref[0:SROWS, hs], 0.0), axis=0, keepdims=True)
        nk8 = jnp.where(row8 == 0, jnp.broadcast_to(nk, (SROWS, HEAD_DIM)), 0.0)
        nv8 = jnp.where(row8 == 0, jnp.broadcast_to(nv, (SROWS, HEAD_DIM)), 0.0)
        kall = jnp.concatenate([kcp_ref[0, :, hs], nk8], axis=0).astype(BF16)
        vall = jnp.concatenate([vcp_ref[0, :, hs], nv8], axis=0).astype(BF16)
        imp = jnp.zeros((nbp, LANES), F32)
        for g in range(gq):
            hd = h * gq + g
            q = _rope(z_ref[0, :, offs["q"] + hd * HEAD_DIM:offs["q"] + (hd + 1) * HEAD_DIM], c, s).astype(BF16)
            p = _masked_softmax_rows(_dot_nt(q, kall) * ATTN_SCALE, ok_r)
            ocmp_ref[0, :, hd * HEAD_DIM:(hd + 1) * HEAD_DIM] = _dot(p.astype(BF16), vall)
            st = _dot_nt(kall, jnp.concatenate([q, qpad], axis=0)) * ATTN_SCALE
            mt = jnp.max(jnp.where(ok_c, st, NEG), axis=0, keepdims=True)
            pt = jnp.where(ok_c, jnp.exp(st - mt), 0.0)
            dt = jnp.sum(pt, axis=0, keepdims=True)
            imp = imp + pt * (1.0 / jnp.maximum(dt, DENOM_FLOOR))
        imp_ref[0, h] = imp


def nsa_sample_cmp(z3, kcp, vcp, ctab, stab, cwk2, cwv2, offs, n_heads, ts, past):
    bsz, _, p = z3.shape
    n_past = kcp.shape[1]
    kvw = KVH_D * HEAD_DIM
    nbp = n_past + SUBLANES
    return pl.pallas_call(
        functools.partial(_nsa_sample_cmp_kernel, offs=offs, gq=n_heads // KVH_D, ts=ts, n_past=n_past, past=past),
        out_shape=(jax.ShapeDtypeStruct((bsz, SROWS, n_heads * HEAD_DIM), F32),
                   jax.ShapeDtypeStruct((bsz, KVH_D, nbp, LANES), F32),
                   jax.ShapeDtypeStruct((bsz, SROWS, 3 * kvw), F32)),
        grid=(bsz,),
        in_specs=[pl.BlockSpec((1, SROWS, p), lambda b: (b, 0, 0)),
                  pl.BlockSpec((1, n_past, kvw), lambda b: (b, 0, 0)),
                  pl.BlockSpec((1, n_past, kvw), lambda b: (b, 0, 0)),
                  pl.BlockSpec((SROWS, LANES), lambda b: (0, 0)),
                  pl.BlockSpec((SROWS, LANES), lambda b: (0, 0)),
                  pl.BlockSpec((CMP_BLOCK, kvw), lambda b: (0, 0)),
                  pl.BlockSpec((CMP_BLOCK, kvw), lambda b: (0, 0))],
        out_specs=(pl.BlockSpec((1, SROWS, n_heads * HEAD_DIM), lambda b: (b, 0, 0)),
                   pl.BlockSpec((1, KVH_D, nbp, LANES), lambda b: (b, 0, 0, 0)),
                   pl.BlockSpec((1, SROWS, 3 * kvw), lambda b: (b, 0, 0))),
        compiler_params=_cparams(("arbitrary",)),
        name="nsa_sample_cmp",
    )(z3, kcp, vcp, ctab, stab, cwk2, cwv2)


def _select_sample_kernel(imp_ref, qpos_ref, idx_ref, score_sc, cnt_sc, *, nbp, nsel):
    shape = score_sc.shape
    blk = lax.broadcasted_iota(jnp.int32, shape, 0)
    cur = jnp.right_shift(qpos_ref[...], CMP_SHIFT)
    forced = (blk == 0) | (blk == cur) | (blk == cur - 1)
    score_sc[...] = jnp.where(blk > cur, -1.0, jnp.where(forced, FORCE_SCORE, imp_ref[...]))
    cnt_sc[...] = jnp.zeros(shape, F32)

    def body(k, carry):
        rk = score_sc[pl.ds(k, 1), :]
        sc = score_sc[...]
        tie = jnp.where(blk > k, 1.0, 0.0)
        cnt_sc[...] = cnt_sc[...] + jnp.where(rk > sc, 1.0, jnp.where(rk == sc, tie, 0.0))
        return carry

    lax.fori_loop(0, nbp, body, 0)
    cnt = cnt_sc[...]
    ok = score_sc[...] >= 0.0
    blk_f = blk.astype(F32)
    for r in range(nsel):
        picked = jnp.sum(jnp.where((cnt == float(r)) & ok, blk_f, 0.0), axis=0, keepdims=True)
        idx_ref[r:r + 1, :] = picked.astype(jnp.int32)


def select_sample(imp_t, qpos):
    nbp, nc = imp_t.shape
    return pl.pallas_call(
        functools.partial(_select_sample_kernel, nbp=nbp, nsel=N_SEL),
        out_shape=jax.ShapeDtypeStruct((N_SEL, nc), jnp.int32),
        scratch_shapes=[pltpu.VMEM((nbp, nc), F32), pltpu.VMEM((nbp, nc), F32)],
        compiler_params=pltpu.CompilerParams(vmem_limit_bytes=VMEM_LIMIT),
        name="select_sample",
    )(imp_t, qpos)


def _nsa_sample_sel_kernel(idx_ref, pt_ref, z_ref, ocmp_ref, knew_ref, wk_ref, wv_ref, c_ref, s_ref, gbias_ref,
                           kpool, vpool, y_ref, kbuf, vbuf, sem, *, offs, gq, n_heads, ts, n_past, past,
                           page_base, npages, wlen):
    b = pl.program_id(0)
    c = c_ref[...]
    s = s_ref[...]
    bpp = PAGE_SIZE // CMP_BLOCK
    kvw = KVH_D * HEAD_DIM
    blk_rows = CMP_BLOCK * KVH_D

    def slot_copies(h, t, r):
        blk = idx_ref[((b * KVH_D + h) * ts + t) * N_SEL + r]
        past_blk = jnp.minimum(blk, n_past - 1)
        page = page_base + pt_ref[b * npages + past_blk // bpp]
        row0 = pl.multiple_of(lax.rem(past_blk, bpp) * blk_rows, blk_rows)
        src_k = kpool.at[page, pl.ds(row0, blk_rows), :]
        src_v = vpool.at[page, pl.ds(row0, blk_rows), :]
        return blk, (pltpu.make_async_copy(src_k, kbuf.at[h, t, r], sem.at[0]),
                     pltpu.make_async_copy(src_v, vbuf.at[h, t, r], sem.at[1]))

    slots = [(h, t, r) for h in range(KVH_D) for t in range(ts) for r in range(N_SEL)]
    for h, t, r in slots:
        blk, cps = slot_copies(h, t, r)

        @pl.when(blk < n_past)
        def _():
            for cp in cps:
                cp.start()

        @pl.when(blk >= n_past)
        def _():
            kbuf[h, t, r] = jnp.zeros((blk_rows, HEAD_DIM), F32)
            vbuf[h, t, r] = jnp.zeros((blk_rows, HEAD_DIM), F32)

    row8 = lax.broadcasted_iota(jnp.int32, (SROWS, HEAD_DIM), 0)
    kpad = jnp.zeros((LANES - SROWS, HEAD_DIM), F32)
    qi = lax.broadcasted_iota(jnp.int32, (SROWS, wlen + LANES), 0)
    ci = lax.broadcasted_iota(jnp.int32, (SROWS, wlen + LANES), 1)
    dw = jnp.where(ci < wlen, qi + wlen - ci, qi - (ci - wlen))
    ok_w = (dw >= 0) & (dw < WIN_D) & ((ci < wlen) | (ci - wlen < ts))
    qr = []
    o_win = []
    for h in range(KVH_D):
        hs = slice(h * HEAD_DIM, (h + 1) * HEAD_DIM)
        kwn = knew_ref[0, :, 2 * kvw + h * HEAD_DIM:2 * kvw + (h + 1) * HEAD_DIM]
        vwn = z_ref[0, :, offs["vw"] + h * HEAD_DIM:offs["vw"] + (h + 1) * HEAD_DIM]
        kw = jnp.concatenate([wk_ref[0, pl.ds(h, wlen, stride=KVH_D), :], kwn, kpad], axis=0).astype(BF16)
        vw = jnp.concatenate([wv_ref[0, pl.ds(h, wlen, stride=KVH_D), :], vwn, kpad], axis=0).astype(BF16)
        for g in range(gq):
            hd = h * gq + g
            q = _rope(z_ref[0, :, offs["q"] + hd * HEAD_DIM:offs["q"] + (hd + 1) * HEAD_DIM], c, s)
            qr.append(q)
            pw = _masked_softmax_rows(_dot_nt(q.astype(BF16), kw) * ATTN_SCALE, ok_w)
            o_win.append(_dot(pw.astype(BF16), vw))

    for h, t, r in slots:
        blk, cps = slot_copies(h, t, r)

        @pl.when(blk < n_past)
        def _():
            for cp in cps:
                cp.wait()

    nkeys = N_SEL * CMP_BLOCK
    colk = lax.broadcasted_iota(jnp.int32, (SROWS, nkeys + LANES), 1)
    slot_of_col = jnp.right_shift(colk, CMP_SHIFT)
    in_blk = colk & (CMP_BLOCK - 1)
    o_sel = [jnp.zeros((SROWS, HEAD_DIM), F32) for _ in range(n_heads)]
    for h in range(KVH_D):
        ksn = knew_ref[0, :, kvw + h * HEAD_DIM:kvw + (h + 1) * HEAD_DIM]
        vsn = z_ref[0, :, offs["vs"] + h * HEAD_DIM:offs["vs"] + (h + 1) * HEAD_DIM]
        for t in range(ts):
            qt = jnp.zeros((SROWS, HEAD_DIM), F32)
            for g in range(gq):
                qt = jnp.where(row8 == g, jnp.broadcast_to(qr[h * gq + g][t:t + 1, :], (SROWS, HEAD_DIM)), qt)
            qt = qt.astype(BF16)
            kg = kbuf[h, t, :, pl.ds(h, CMP_BLOCK, stride=KVH_D), :].reshape(nkeys, HEAD_DIM)
            vg = vbuf[h, t, :, pl.ds(h, CMP_BLOCK, stride=KVH_D), :].reshape(nkeys, HEAD_DIM)
            kall = jnp.concatenate([kg, ksn, kpad], axis=0).astype(BF16)
            vall = jnp.concatenate([vg, vsn, kpad], axis=0).astype(BF16)
            far = past + ts + SROWS
            kpos = jnp.where(colk >= nkeys, past + (colk - nkeys), 0)
            kpos = jnp.where((colk >= nkeys) & (colk - nkeys >= ts), far, kpos)
            n_new = jnp.int32(0)
            for r in range(N_SEL):
                blk = idx_ref[((b * KVH_D + h) * ts + t) * N_SEL + r]
                n_new = n_new + jnp.where(blk >= n_past, 1, 0)
                base = jnp.where(blk < n_past, blk * CMP_BLOCK, far)
                kpos = jnp.where(slot_of_col == r, base + in_blk, kpos)
            kpos = jnp.where((colk >= nkeys) & (n_new == 0), far, kpos)
            ok = kpos <= past + t
            p = _masked_softmax_rows(_dot_nt(qt, kall) * ATTN_SCALE, ok)
            res = _dot(p.astype(BF16), vall)
            for g in range(gq):
                hd = h * gq + g
                o_sel[hd] = jnp.where(row8 == t, jnp.broadcast_to(res[g:g + 1, :], (SROWS, HEAD_DIM)), o_sel[hd])

    gate = jax.nn.sigmoid(z_ref[0, :, offs["gl"]:offs["gl"] + LANES] + gbias_ref[...])
    for hd in range(n_heads):
        sl = slice(hd * HEAD_DIM, (hd + 1) * HEAD_DIM)
        yd = (gate[:, hd:hd + 1] * ocmp_ref[0, :, sl]
              + gate[:, n_heads + hd:n_heads + hd + 1] * o_sel[hd]
              + gate[:, 2 * n_heads + hd:2 * n_heads + hd + 1] * o_win[hd])
        y_ref[0, :, sl] = (yd * _silu(z_ref[0, :, offs["gd"] + hd * HEAD_DIM:offs["gd"] + (hd + 1) * HEAD_DIM])
                           ).astype(BF16)


def nsa_sample_sel(idx_flat, pt_flat, z3, ocmp, knew, wk, wv, ctab, stab, gbias, kpool, vpool, offs, n_heads,
                   ts, past, page_base, npages, layer, n_layers_b):
    bsz, _, p = z3.shape
    wlen = wk.shape[1] // KVH_D
    kvw = KVH_D * HEAD_DIM
    qw = n_heads * HEAD_DIM
    n_past = npages * (PAGE_SIZE // CMP_BLOCK)
    grid_spec = pltpu.PrefetchScalarGridSpec(
        num_scalar_prefetch=2,
        grid=(bsz,),
        in_specs=[pl.BlockSpec((1, SROWS, p), lambda b, i, t: (b, 0, 0)),
                  pl.BlockSpec((1, SROWS, qw), lambda b, i, t: (b, 0, 0)),
                  pl.BlockSpec((1, SROWS, 3 * kvw), lambda b, i, t: (b, 0, 0)),
                  pl.BlockSpec((1, wlen * KVH_D, HEAD_DIM), lambda b, i, t: (layer * n_layers_b + b, 0, 0)),
                  pl.BlockSpec((1, wlen * KVH_D, HEAD_DIM), lambda b, i, t: (layer * n_layers_b + b, 0, 0)),
                  pl.BlockSpec((SROWS, LANES), lambda b, i, t: (0, 0)),
                  pl.BlockSpec((SROWS, LANES), lambda b, i, t: (0, 0)),
                  pl.BlockSpec((1, LANES), lambda b, i, t: (0, 0)),
                  pl.BlockSpec(memory_space=pl.ANY), pl.BlockSpec(memory_space=pl.ANY)],
        out_specs=pl.BlockSpec((1, SROWS, qw), lambda b, i, t: (b, 0, 0)),
        scratch_shapes=[pltpu.VMEM((KVH_D, ts, N_SEL, CMP_BLOCK * KVH_D, HEAD_DIM), F32),
                        pltpu.VMEM((KVH_D, ts, N_SEL, CMP_BLOCK * KVH_D, HEAD_DIM), F32),
                        pltpu.SemaphoreType.DMA((2,))],
    )
    return pl.pallas_call(
        functools.partial(_nsa_sample_sel_kernel, offs=offs, gq=n_heads // KVH_D, n_heads=n_heads, ts=ts,
                          n_past=n_past, past=past, page_base=page_base, npages=npages, wlen=wlen),
        out_shape=jax.ShapeDtypeStruct((bsz, SROWS, qw), BF16),
        grid_spec=grid_spec,
        compiler_params=_cparams(("arbitrary",)),
        name="nsa_sample_sel",
    )(idx_flat, pt_flat, z3, ocmp, knew, wk, wv, ctab, stab, gbias, kpool, vpool)


EVEN_OFFS = dict(xa=0, ga=1024, q=2048, k=3072, v=3328, gb=3584)


def _odd_layout(d_c, d_d, kvw, n_gate):
    offs = {}
    pos = 0
    for name, width in (("q", d_d), ("gd", d_d), ("xc", d_c), ("gc", d_c), ("kc", kvw), ("vc", kvw), ("ks", kvw),
                        ("vs", kvw), ("kw", kvw), ("vw", kvw), ("gl", n_gate)):
        offs[name] = pos
        pos += width
    total = -(-pos // LANES) * LANES
    return offs, pos, total


def _permute_odd_weight(w, d_c, d_d, kvw, n_gate, total):
    o = 0
    segs = {}
    for name, width in (("xc", d_c), ("gc", d_c), ("q", d_d), ("kc", kvw), ("vc", kvw), ("ks", kvw), ("vs", kvw),
                        ("kw", kvw), ("vw", kvw), ("gd", d_d), ("gl", n_gate)):
        segs[name] = w[:, o:o + width]
        o += width
    cols = [segs[n] for n in ("q", "gd", "xc", "gc", "kc", "vc", "ks", "vs", "kw", "vw", "gl")]
    used = sum(c.shape[1] for c in cols)
    cols.append(jnp.zeros((w.shape[0], total - used), w.dtype))
    return jnp.concatenate(cols, axis=1).astype(BF16)


def _pick_tile(n, prefs):
    for t in prefs:
        if n % t == 0:
            return t
    raise ValueError(n)


def kernel(x_prompt, x_sample, state_lru_h, state_lru_conv, cache_swa_k, cache_swa_v, state_pool, cache_nsa_cmp_k, cache_nsa_cmp_v, cache_nsa_sel_k, cache_nsa_sel_v, cache_nsa_win_k, cache_nsa_win_v, page_table, norm_g, final_g, w_in_even, conv_w, conv_b, w_rgate, b_rgate, w_igate, b_igate, lru_lambda, swa_sinks, w_out_even, w_in_odd, w_pool, pool_scale, cmp_wk, cmp_wv, nsa_gate_b, w_out_odd):
    bp, tp, d = x_prompt.shape
    bs, ts, _ = x_sample.shape
    depth = norm_g.shape[0]
    npages = page_table.shape[1]
    past = npages * PAGE_SIZE
    n_phys = cache_nsa_cmp_k.shape[1]
    d_a = state_lru_h.shape[-1]
    n_heads_b = swa_sinks.shape[1]
    d_b = n_heads_b * HEAD_DIM
    d_c = state_pool.shape[-1]
    n_heads_d = nsa_gate_b.shape[1] // 3
    d_d = n_heads_d * HEAD_DIM
    kvw_b = KVH_B * HEAD_DIM
    kvw_d = KVH_D * HEAD_DIM
    assert ts <= SROWS and past % CMP_BLOCK == 0 and (past + ts - 1) // CMP_BLOCK == past // CMP_BLOCK

    hp = x_prompt.reshape(bp * tp, d)
    hs = jnp.pad(x_sample, ((0, 0), (0, SROWS - ts), (0, 0))).reshape(bs * SROWS, d)
    ms = bs * SROWS
    ctab_p, stab_p = rope_tables(tp, 0)
    ctab_s, stab_s = rope_tables(SROWS, past)
    tm_p = _pick_tile(bp * tp, (1024, 512, 256, 128))
    pt_flat = page_table.reshape(-1)

    odd_offs, odd_used, odd_total = _odd_layout(d_c, d_d, kvw_d, 3 * n_heads_d)
    ev_p, ev_s, od_p, od_s = [], [], [], []
    for i in range(depth):
        j = i // 2
        if i % 2 == 0:
            p_even = w_in_even.shape[2]
            w_in = w_in_even[j].astype(BF16)
            tn = _pick_tile(p_even, (512, 256, 128))
            zp = norm_proj(hp, norm_g[i], w_in, tm_p, tn)
            zs = norm_proj(hs, norm_g[i], w_in, ms, tn)
            wr = w_rgate[j].astype(BF16)
            wi = w_igate[j].astype(BF16)
            lru_args = (conv_w[j], conv_b[j], wr, b_rgate[j], wi, b_igate[j], lru_lambda[j])
            eo = EVEN_OFFS
            ya_p, hl_p = lru_prompt(zp, bp, tp, d_a, *lru_args)
            yb_p, kout_p = swa_prompt(zp, swa_sinks[j], ctab_p, stab_p, bp, tp, eo["q"], eo["k"], eo["v"], eo["gb"],
                                      n_heads_b)
            wo = w_out_even[j].astype(BF16)
            tn_o = _pick_tile(d, (512, 256, 128))
            hp = out_proj(ya_p, yb_p, wo[:d_a], wo[d_a:], hp, tm_p, tn_o)
            zp3 = zp.reshape(bp, tp, p_even)
            keep = min(WIN_B, tp)
            ev_p.append((hl_p[:, 0], zp3[:, tp - (CONV_W - 1):, eo["xa"]:eo["xa"] + d_a],
                         kout_p.transpose(0, 2, 1, 3)[:, WIN_B - keep:],
                         zp3[:, tp - keep:, eo["v"]:eo["v"] + kvw_b].reshape(bp, keep, KVH_B, HEAD_DIM)))
            zs3 = zs.reshape(bs, SROWS, p_even)
            zst = zs3.transpose(1, 0, 2)
            ya_t, hl_s = lru_sample(zst[:, :, eo["xa"]:eo["xa"] + d_a], zst[:, :, eo["ga"]:eo["ga"] + d_a],
                                    state_lru_conv[j].transpose(1, 0, 2), state_lru_h[j], *lru_args, ts)
            ya_s = ya_t.transpose(1, 0, 2).reshape(ms, d_a)
            rows3 = lambda a: a.reshape(a.shape[0] * a.shape[1], a.shape[2] * a.shape[3], a.shape[4])
            yb_s, knew = swa_sample(zs3, swa_sinks[j], rows3(cache_swa_k), rows3(cache_swa_v), j, ctab_s, stab_s,
                                    eo["q"], eo["k"], eo["v"], eo["gb"], n_heads_b, ts)
            hs = out_proj(ya_s, yb_s.reshape(ms, d_b), wo[:d_a], wo[d_a:], hs, ms, tn_o)
            wlen = cache_swa_k.shape[2]
            new_conv = jnp.concatenate([state_lru_conv[j], zs3[:, :ts, eo["xa"]:eo["xa"] + d_a]], axis=1)[:, -(CONV_W - 1):]
            new_k = jnp.concatenate([cache_swa_k[j], knew[:, :ts].reshape(bs, ts, KVH_B, HEAD_DIM)], axis=1)[:, -wlen:]
            new_v = jnp.concatenate([cache_swa_v[j], zs3[:, :ts, eo["v"]:eo["v"] + kvw_b].reshape(bs, ts, KVH_B, HEAD_DIM)],
                                    axis=1)[:, -wlen:]
            ev_s.append((hl_s, new_conv, new_k, new_v))
        else:
            oo = odd_offs
            w_in = _permute_odd_weight(w_in_odd[j], d_c, d_d, kvw_d, 3 * n_heads_d, odd_total)
            tn = _pick_tile(odd_total, (640, 512, 384, 256, 128))
            zp = norm_proj(hp, norm_g[i], w_in, tm_p, tn)
            zs = norm_proj(hs, norm_g[i], w_in, ms, tn)
            wp = w_pool[j].astype(BF16)
            cwk2 = jnp.repeat(cmp_wk[j], HEAD_DIM, axis=1)
            cwv2 = jnp.repeat(cmp_wv[j], HEAD_DIM, axis=1)
            gbias = jnp.pad(nsa_gate_b[j], (0, LANES - 3 * n_heads_d)).reshape(1, LANES)
            wo = w_out_odd[j].astype(BF16)
            tn_o = _pick_tile(d, (512, 256, 128))
            yc_p = pool_prompt(zp, bp, tp, d_c, oo["xc"], oo["gc"], wp, pool_scale[j])
            qr, kcr, ksr, kwr, kvb, kcmp, vcmp = nsa_prep(zp, ctab_p, stab_p, cwk2, cwv2, bp, tp, oo, n_heads_d)
            yd_p = nsa_prompt(zp, qr, kcmp, vcmp, kvb, gbias, bp, tp, oo, n_heads_d)
            hp = out_proj(yc_p, yd_p, wo[:d_c], wo[d_c:], hp, tm_p, tn_o)
            zp3 = zp.reshape(bp, tp, odd_total)
            kv4 = lambda a: a.reshape(bp, tp, KVH_D, HEAD_DIM)
            keep = min(WIN_D, tp)
            od_p.append((zp3[:, tp - (POOL_MAX - 1):, oo["xc"]:oo["xc"] + d_c],
                         kv4(kcr), kv4(zp3[:, :, oo["vc"]:oo["vc"] + kvw_d]),
                         kv4(ksr), kv4(zp3[:, :, oo["vs"]:oo["vs"] + kvw_d]),
                         kv4(kwr)[:, tp - keep:], kv4(zp3[:, :, oo["vw"]:oo["vw"] + kvw_d])[:, tp - keep:]))
            zs3 = zs.reshape(bs, SROWS, odd_total)
            zst = zs3.transpose(1, 0, 2)
            yc_t = pool_sample(zst[:, :, oo["xc"]:oo["xc"] + d_c], zst[:, :, oo["gc"]:oo["gc"] + d_c],
                               state_pool[j].transpose(1, 0, 2), wp, pool_scale[j], ts, past)
            yc_s = yc_t.transpose(1, 0, 2).reshape(ms, d_c)
            n_layers_odd = cache_nsa_cmp_k.shape[0]
            pool3 = lambda a: a.reshape(n_layers_odd * n_phys, PAGE_SIZE * KVH_D, HEAD_DIM)
            page_w = lambda w: jnp.broadcast_to(jnp.tile(w, (PAGE_SIZE // CMP_BLOCK, 1)).reshape(-1, 1),
                                                (PAGE_SIZE * KVH_D, LANES))
            cwk_page = page_w(cmp_wk[j])
            cwv_page = page_w(cmp_wv[j])
            kcp, vcp = cmp_stream(pt_flat, pool3(cache_nsa_cmp_k), pool3(cache_nsa_cmp_v), cwk_page, cwv_page,
                                  bs, npages, j * n_phys)
            ocmp, imp, knew = nsa_sample_cmp(zs3, kcp, vcp, ctab_s, stab_s, cwk2, cwv2, oo, n_heads_d, ts, past)
            nbp = imp.shape[2]
            assert nbp >= N_SEL
            imp_t = imp[:, :, :, :ts].transpose(2, 0, 1, 3).reshape(nbp, bs * KVH_D * ts)
            qpos = jnp.tile(past + jnp.arange(ts, dtype=jnp.int32), bs * KVH_D).reshape(1, -1)
            idx = select_sample(imp_t, qpos)
            idx_flat = idx.T.reshape(-1)
            wlen = cache_nsa_win_k.shape[2]
            win3 = lambda a: a.reshape(n_layers_odd * bs, wlen * KVH_D, HEAD_DIM)
            yd_s = nsa_sample_sel(idx_flat, pt_flat, zs3, ocmp, knew, win3(cache_nsa_win_k), win3(cache_nsa_win_v),
                                  ctab_s, stab_s, gbias, pool3(cache_nsa_sel_k), pool3(cache_nsa_sel_v), oo,
                                  n_heads_d, ts, past, j * n_phys, npages, j, bs)
            hs = out_proj(yc_s, yd_s.reshape(ms, d_d), wo[:d_c], wo[d_c:], hs, ms, tn_o)
            kv4s = lambda a: a[:, :ts].reshape(bs, ts, KVH_D, HEAD_DIM)
            new_pool = jnp.concatenate([state_pool[j], zs3[:, :ts, oo["xc"]:oo["xc"] + d_c]], axis=1)[:, -(POOL_MAX - 1):]
            kwn = kv4s(knew[:, :, 2 * kvw_d:3 * kvw_d])
            vwn = kv4s(zs3[:, :, oo["vw"]:oo["vw"] + kvw_d])
            od_s.append((new_pool,
                         kv4s(knew[:, :, 0:kvw_d]), kv4s(zs3[:, :, oo["vc"]:oo["vc"] + kvw_d]),
                         kv4s(knew[:, :, kvw_d:2 * kvw_d]), kv4s(zs3[:, :, oo["vs"]:oo["vs"] + kvw_d]),
                         jnp.concatenate([cache_nsa_win_k[j], kwn], axis=1)[:, -wlen:],
                         jnp.concatenate([cache_nsa_win_v[j], vwn], axis=1)[:, -wlen:]))

    y_prompt = final_norm(hp, final_g, tm_p).reshape(bp, tp, d)
    y_sample = final_norm(hs, final_g, ms).reshape(bs, SROWS, d)[:, :ts]

    def field(lst, k):
        return jnp.stack([st[k] for st in lst], axis=0)

    return (y_prompt, y_sample,
            field(ev_p, 0), field(ev_p, 1), field(ev_p, 2), field(ev_p, 3),
            field(od_p, 0), field(od_p, 1), field(od_p, 2), field(od_p, 3), field(od_p, 4), field(od_p, 5), field(od_p, 6),
            field(ev_s, 0), field(ev_s, 1), field(ev_s, 2), field(ev_s, 3),
            field(od_s, 0), field(od_s, 1), field(od_s, 2), field(od_s, 3), field(od_s, 4), field(od_s, 5), field(od_s, 6))
```

```python
import functools

import jax
import jax.numpy as jnp
import numpy as np
from jax import lax
from jax.experimental import pallas as pl
from jax.experimental.pallas import tpu as pltpu

F32 = jnp.float32
BF16 = jnp.bfloat16

LANES = 128
SUBLANES = 8
VMEM_LIMIT = 48 * 1024 * 1024

HEAD_DIM = 128
ATTN_SCALE = HEAD_DIM ** -0.5
ROPE_THETA = 10000.0
NORM_EPS = 1e-6
PAGE_SIZE = 128
CONV_W = 4
LRU_C = 8.0
KVH_B = 2
WIN_B = 128
POOL_WINDOWS = (2, 4, 8, 16)
POOL_MAX = 16
KVH_D = 2
CMP_BLOCK = 64
CMP_SHIFT = 6
N_SEL = 16
WIN_D = 512
FORCE_SCORE = 1e6
DENOM_FLOOR = 1e-30
NEG = -1e30
SROWS = 8


def _cparams(sem):
    return pltpu.CompilerParams(dimension_semantics=sem, vmem_limit_bytes=VMEM_LIMIT)


def _dot(a, b):
    return jnp.dot(a, b, preferred_element_type=F32)


def _dot_nt(a, b):
    return lax.dot_general(a, b, (((1,), (1,)), ((), ())), preferred_element_type=F32)


def _silu(x):
    return x * jax.nn.sigmoid(x)


def _rope(x, c, s):
    return x * c + pltpu.roll(x, HEAD_DIM // 2, 1) * s


def _rope_tab_kernel(inv_ref, sgn_ref, c_ref, s_ref, *, start, rows):
    i = pl.program_id(0)
    pos = start + i * rows + lax.broadcasted_iota(jnp.int32, (rows, LANES), 0)
    ang = pos.astype(F32) * inv_ref[...]
    c_ref[...] = jnp.cos(ang)
    s_ref[...] = jnp.sin(ang) * sgn_ref[...]


def rope_tables(n_rows, start):
    half = HEAD_DIM // 2
    inv = ROPE_THETA ** (-jnp.arange(half, dtype=F32) / half)
    inv2 = jnp.concatenate([inv, inv]).reshape(1, LANES)
    sgn = jnp.concatenate([-jnp.ones((half,), F32), jnp.ones((half,), F32)]).reshape(1, LANES)
    rows = min(n_rows, 512)
    assert n_rows % rows == 0
    return pl.pallas_call(
        functools.partial(_rope_tab_kernel, start=start, rows=rows),
        out_shape=(jax.ShapeDtypeStruct((n_rows, LANES), F32),) * 2,
        grid=(n_rows // rows,),
        in_specs=[pl.BlockSpec((1, LANES), lambda i: (0, 0))] * 2,
        out_specs=(pl.BlockSpec((rows, LANES), lambda i: (i, 0)),) * 2,
        compiler_params=_cparams(("arbitrary",)),
        name="rope_tables",
    )(inv2, sgn)


def _norm_proj_kernel(x_ref, g_ref, w_ref, o_ref, xn_ref):
    @pl.when(pl.program_id(1) == 0)
    def _():
        x = x_ref[...]
        ms = jnp.mean(x * x, axis=-1, keepdims=True)
        xn_ref[...] = (x * lax.rsqrt(ms + NORM_EPS) * g_ref[...]).astype(BF16)

    o_ref[...] = _dot(xn_ref[...], w_ref[...])


def norm_proj(x, g, w, tm, tn):
    m, d = x.shape
    n = w.shape[1]
    assert m % tm == 0
    return pl.pallas_call(
        _norm_proj_kernel,
        out_shape=jax.ShapeDtypeStruct((m, n), F32),
        grid=(m // tm, pl.cdiv(n, tn)),
        in_specs=[pl.BlockSpec((tm, d), lambda i, j: (i, 0)),
                  pl.BlockSpec((1, d), lambda i, j: (0, 0)),
                  pl.BlockSpec((d, tn), lambda i, j: (0, j))],
        out_specs=pl.BlockSpec((tm, tn), lambda i, j: (i, j)),
        scratch_shapes=[pltpu.VMEM((tm, d), BF16)],
        compiler_params=_cparams(("arbitrary", "arbitrary")),
        name="norm_proj",
    )(x, g.reshape(1, d), w)


def _out_proj_kernel(ya_ref, yb_ref, wa_ref, wb_ref, r_ref, o_ref):
    o_ref[...] = r_ref[...] + (_dot(ya_ref[...], wa_ref[...]) + _dot(yb_ref[...], wb_ref[...]))


def out_proj(ya, yb, wa, wb, resid, tm, tn):
    m, ka = ya.shape
    kb = yb.shape[1]
    n = wa.shape[1]
    assert m % tm == 0 and n % tn == 0
    return pl.pallas_call(
        _out_proj_kernel,
        out_shape=jax.ShapeDtypeStruct((m, n), F32),
        grid=(m // tm, n // tn),
        in_specs=[pl.BlockSpec((tm, ka), lambda i, j: (i, 0)),
                  pl.BlockSpec((tm, kb), lambda i, j: (i, 0)),
                  pl.BlockSpec((ka, tn), lambda i, j: (0, j)),
                  pl.BlockSpec((kb, tn), lambda i, j: (0, j)),
                  pl.BlockSpec((tm, tn), lambda i, j: (i, j))],
        out_specs=pl.BlockSpec((tm, tn), lambda i, j: (i, j)),
        compiler_params=_cparams(("arbitrary", "arbitrary")),
        name="out_proj",
    )(ya, yb, wa, wb, resid)


def _final_norm_kernel(x_ref, g_ref, o_ref):
    x = x_ref[...]
    ms = jnp.mean(x * x, axis=-1, keepdims=True)
    o_ref[...] = x * lax.rsqrt(ms + NORM_EPS) * g_ref[...]


def final_norm(x, g, tm):
    m, d = x.shape
    return pl.pallas_call(
        _final_norm_kernel,
        out_shape=jax.ShapeDtypeStruct((m, d), F32),
        grid=(m // tm,),
        in_specs=[pl.BlockSpec((tm, d), lambda i: (i, 0)), pl.BlockSpec((1, d), lambda i: (0, 0))],
        out_specs=pl.BlockSpec((tm, d), lambda i: (i, 0)),
        compiler_params=_cparams(("arbitrary",)),
        name="final_norm",
    )(x, g.reshape(1, d))


def _lru_gates(xs, wr, br, wi, bi, sp):
    xb = xs.astype(BF16)
    r = jax.nn.sigmoid(_dot(xb, wr) + br)
    ig = jax.nn.sigmoid(_dot(xb, wi) + bi)
    log_a = -LRU_C * r * sp
    a = jnp.exp(log_a)
    u = jnp.sqrt(-jnp.tanh(log_a) * (a * a + 1.0)) * (ig * xs)
    return a, u


def _softplus(z):
    return jnp.maximum(z, 0.0) + jnp.log1p(jnp.exp(-jnp.abs(z)))


def _lru_prompt_kernel(xa_ref, ga_ref, cw_ref, cb_ref, wr_ref, br_ref, wi_ref, bi_ref, lam_ref,
                       ya_ref, hl_ref, h_sc, halo_sc, *, tt, nblk):
    i = pl.program_id(1)

    @pl.when(i == 0)
    def _():
        h_sc[...] = jnp.zeros_like(h_sc)
        halo_sc[...] = jnp.zeros_like(halo_sc)

    x = xa_ref[...]
    xe = jnp.concatenate([halo_sc[...], x], axis=0)
    halo_sc[...] = x[tt - SUBLANES:, :]
    cw = cw_ref[...]
    y = cb_ref[...] + xe[SUBLANES:, :] * cw[CONV_W - 1:CONV_W, :]
    for k in range(CONV_W - 1):
        y = y + pltpu.roll(xe, CONV_W - 1 - k, 0)[SUBLANES:, :] * cw[k:k + 1, :]
    sp = _softplus(-lam_ref[...])
    rid = lax.broadcasted_iota(jnp.int32, (tt, LANES), 0) & (SUBLANES - 1)
    for n in range(nblk):
        sl = slice(n * LANES, (n + 1) * LANES)
        a, u = _lru_gates(y[:, sl], wr_ref[n], br_ref[:, sl], wi_ref[n], bi_ref[:, sl], sp[:, sl])
        for s in (1, 2, 4):
            keep = rid >= s
            u = jnp.where(keep, a * pltpu.roll(u, s, 0) + u, u)
            a = jnp.where(keep, a * pltpu.roll(a, s, 0), a)
        h = h_sc[:, sl]
        outs = []
        for g in range(tt // SUBLANES):
            rows = slice(g * SUBLANES, (g + 1) * SUBLANES)
            hg = u[rows, :] + a[rows, :] * h
            outs.append(hg)
            h = hg[SUBLANES - 1:, :]
        h_sc[:, sl] = h
        hs = jnp.concatenate(outs, axis=0)
        ya_ref[:, sl] = (hs * _silu(ga_ref[:, sl])).astype(BF16)
    hl_ref[0] = h_sc[...]


def lru_prompt(z, bsz, t, d_a, cw, cb, wr, br, wi, bi, lam, tt=256):
    nt = t // tt
    nblk = d_a // LANES
    row = lambda b, i: (b * nt + i, 0)
    vec = pl.BlockSpec((1, d_a), lambda b, i: (0, 0))
    wspec = pl.BlockSpec((nblk, LANES, LANES), lambda b, i: (0, 0, 0))
    return pl.pallas_call(
        functools.partial(_lru_prompt_kernel, tt=tt, nblk=nblk),
        out_shape=(jax.ShapeDtypeStruct((bsz * t, d_a), BF16), jax.ShapeDtypeStruct((bsz, 1, d_a), F32)),
        grid=(bsz, nt),
        in_specs=[pl.BlockSpec((tt, d_a), row),
                  pl.BlockSpec((tt, d_a), lambda b, i: (b * nt + i, 1)),
                  pl.BlockSpec((CONV_W, d_a), lambda b, i: (0, 0)), vec, wspec, vec, wspec, vec, vec],
        out_specs=(pl.BlockSpec((tt, d_a), row), pl.BlockSpec((1, 1, d_a), lambda b, i: (b, 0, 0))),
        scratch_shapes=[pltpu.VMEM((1, d_a), F32), pltpu.VMEM((SUBLANES, d_a), F32)],
        compiler_params=_cparams(("arbitrary", "arbitrary")),
        name="lru_prompt",
    )(z, z, cw, cb.reshape(1, d_a), wr, br.reshape(1, d_a), wi, bi.reshape(1, d_a), lam.reshape(1, d_a))


def _lru_sample_kernel(x_ref, ga_ref, c0_ref, h0_ref, cw_ref, cb_ref, wr_ref, br_ref, wi_ref, bi_ref, lam_ref,
                       ya_ref, hl_ref, *, ts, nblk):
    cw = cw_ref[...]
    sp = _softplus(-lam_ref[...])
    xe = [c0_ref[k] for k in range(CONV_W - 1)] + [x_ref[t] for t in range(ts)]
    h = h0_ref[...]
    for t in range(ts):
        y = cb_ref[...]
        for k in range(CONV_W):
            y = y + xe[t + k] * cw[k:k + 1, :]
        a_l, u_l = [], []
        for n in range(nblk):
            sl = slice(n * LANES, (n + 1) * LANES)
            a, u = _lru_gates(y[:, sl], wr_ref[n], br_ref[:, sl], wi_ref[n], bi_ref[:, sl], sp[:, sl])
            a_l.append(a)
            u_l.append(u)
        h = jnp.concatenate(a_l, axis=1) * h + jnp.concatenate(u_l, axis=1)
        ya_ref[t] = (h * _silu(ga_ref[t])).astype(BF16)
    for t in range(ts, SROWS):
        ya_ref[t] = jnp.zeros(ya_ref.shape[1:], BF16)
    hl_ref[...] = h


def lru_sample(xa_t, ga_t, conv_t, h0, cw, cb, wr, br, wi, bi, lam, ts):
    _, bsz, d_a = xa_t.shape
    nblk = d_a // LANES
    return pl.pallas_call(
        functools.partial(_lru_sample_kernel, ts=ts, nblk=nblk),
        out_shape=(jax.ShapeDtypeStruct((SROWS, bsz, d_a), BF16), jax.ShapeDtypeStruct((bsz, d_a), F32)),
        compiler_params=pltpu.CompilerParams(vmem_limit_bytes=VMEM_LIMIT),
        name="lru_sample",
    )(xa_t, ga_t, conv_t, h0, cw, cb.reshape(1, d_a), wr, br.reshape(1, d_a), wi, bi.reshape(1, d_a),
      lam.reshape(1, d_a))


def _swa_prompt_kernel(sink_ref, q_ref, kp_ref, kc_ref, vp_ref, vc_ref, gb_ref, cq_ref, sq_ref, cp_ref, sp_ref,
                       yb_ref, kout_ref, *, nq, gq, tq):
    kv = pl.program_id(1)
    i = pl.program_id(2)
    cq = cq_ref[...]
    sq = sq_ref[...]
    k_cur = _rope(kc_ref[...], cq, sq)
    k_prev = _rope(kp_ref[...], cp_ref[...], sp_ref[...])

    @pl.when(i == nq - 1)
    def _():
        kout_ref[0, 0] = k_cur

    k = jnp.concatenate([k_prev, k_cur], axis=0).astype(BF16)
    v = jnp.concatenate([vp_ref[...], vc_ref[...]], axis=0).astype(BF16)
    r = lax.broadcasted_iota(jnp.int32, (tq, 2 * tq), 0)
    c = lax.broadcasted_iota(jnp.int32, (tq, 2 * tq), 1)
    valid = (c > r) & (c <= r + WIN_B) & ((c >= tq) | (i > 0))
    for g in range(gq):
        sl = slice(g * HEAD_DIM, (g + 1) * HEAD_DIM)
        qg = _rope(q_ref[:, sl], cq, sq).astype(BF16)
        s = _dot_nt(qg, k) * ATTN_SCALE
        sink = sink_ref[kv * gq + g]
        m = jnp.maximum(jnp.max(jnp.where(valid, s, NEG), axis=1, keepdims=True), sink)
        p = jnp.where(valid, jnp.exp(s - m), 0.0)
        den = jnp.sum(p, axis=1, keepdims=True) + jnp.exp(sink - m)
        p = p * (1.0 / jnp.maximum(den, DENOM_FLOOR))
        o = _dot(p.astype(BF16), v)
        yb_ref[:, sl] = (o * _silu(gb_ref[:, sl])).astype(BF16)


def swa_prompt(z, sinks, ctab, stab, bsz, t, q_off, k_off, v_off, g_off, n_heads):
    tq = WIN_B
    nq = t // tq
    gq = n_heads // KVH_B
    wq = gq * HEAD_DIM
    assert q_off % wq == 0 and g_off % wq == 0 and k_off % HEAD_DIM == 0 and v_off % HEAD_DIM == 0
    prev = lambda i: jnp.maximum(i - 1, 0)
    in_specs = [
        pl.BlockSpec(memory_space=pltpu.SMEM),
        pl.BlockSpec((tq, wq), lambda b, h, i: (b * nq + i, q_off // wq + h)),
        pl.BlockSpec((tq, HEAD_DIM), lambda b, h, i: (b * nq + prev(i), k_off // HEAD_DIM + h)),
        pl.BlockSpec((tq, HEAD_DIM), lambda b, h, i: (b * nq + i, k_off // HEAD_DIM + h)),
        pl.BlockSpec((tq, HEAD_DIM), lambda b, h, i: (b * nq + prev(i), v_off // HEAD_DIM + h)),
        pl.BlockSpec((tq, HEAD_DIM), lambda b, h, i: (b * nq + i, v_off // HEAD_DIM + h)),
        pl.BlockSpec((tq, wq), lambda b, h, i: (b * nq + i, g_off // wq + h)),
        pl.BlockSpec((tq, LANES), lambda b, h, i: (i, 0)),
        pl.BlockSpec((tq, LANES), lambda b, h, i: (i, 0)),
        pl.BlockSpec((tq, LANES), lambda b, h, i: (prev(i), 0)),
        pl.BlockSpec((tq, LANES), lambda b, h, i: (prev(i), 0)),
    ]
    return pl.pallas_call(
        functools.partial(_swa_prompt_kernel, nq=nq, gq=gq, tq=tq),
        out_shape=(jax.ShapeDtypeStruct((bsz * t, n_heads * HEAD_DIM), BF16),
                   jax.ShapeDtypeStruct((bsz, KVH_B, tq, HEAD_DIM), F32)),
        grid=(bsz, KVH_B, nq),
        in_specs=in_specs,
        out_specs=(pl.BlockSpec((tq, wq), lambda b, h, i: (b * nq + i, h)),
                   pl.BlockSpec((1, 1, tq, HEAD_DIM), lambda b, h, i: (b, h, 0, 0))),
        compiler_params=_cparams(("arbitrary", "arbitrary", "arbitrary")),
        name="swa_prompt",
    )(sinks, z, z, z, z, z, z, ctab, stab, ctab, stab)


def _swa_sample_kernel(sink_ref, z_ref, kc_ref, vc_ref, c_ref, s_ref, yb_ref, kn_ref, *,
                       gq, q_off, k_off, v_off, g_off, ts, wlen, bb):
    c = c_ref[...]
    s = s_ref[...]
    qi = lax.broadcasted_iota(jnp.int32, (SROWS, wlen + SROWS), 0)
    ci = lax.broadcasted_iota(jnp.int32, (SROWS, wlen + SROWS), 1)
    diff = jnp.where(ci < wlen, qi + wlen - ci, qi - (ci - wlen))
    valid = (diff >= 0) & (diff < WIN_B)
    valid_g = jnp.concatenate([valid] * gq, axis=0)
    for bi in range(bb):
        for h in range(KVH_B):
            ks = slice(k_off + h * HEAD_DIM, k_off + (h + 1) * HEAD_DIM)
            vs = slice(v_off + h * HEAD_DIM, v_off + (h + 1) * HEAD_DIM)
            hs = slice(h * HEAD_DIM, (h + 1) * HEAD_DIM)
            k_new = _rope(z_ref[bi, :, ks], c, s)
            kn_ref[bi, :, hs] = k_new
            k = jnp.concatenate([kc_ref[bi, pl.ds(h, wlen, stride=KVH_B), :], k_new], axis=0).astype(BF16)
            v = jnp.concatenate([vc_ref[bi, pl.ds(h, wlen, stride=KVH_B), :], z_ref[bi, :, vs]], axis=0).astype(BF16)
            qs = jnp.concatenate(
                [_rope(z_ref[bi, :, q_off + (h * gq + g) * HEAD_DIM:q_off + (h * gq + g + 1) * HEAD_DIM], c, s)
                 for g in range(gq)], axis=0).astype(BF16)
            sink = jnp.concatenate([jnp.full((SROWS, 1), sink_ref[h * gq + g], F32) for g in range(gq)], axis=0)
            sc = _dot_nt(qs, k) * ATTN_SCALE
            m = jnp.maximum(jnp.max(jnp.where(valid_g, sc, NEG), axis=1, keepdims=True), sink)
            p = jnp.where(valid_g, jnp.exp(sc - m), 0.0)
            den = jnp.sum(p, axis=1, keepdims=True) + jnp.exp(sink - m)
            p = p * (1.0 / jnp.maximum(den, DENOM_FLOOR))
            o = _dot(p.astype(BF16), v)
            for g in range(gq):
                hd = h * gq + g
                gate = _silu(z_ref[bi, :, g_off + hd * HEAD_DIM:g_off + (hd + 1) * HEAD_DIM])
                yb_ref[bi, :, hd * HEAD_DIM:(hd + 1) * HEAD_DIM] = (o[g * SROWS:(g + 1) * SROWS, :] * gate).astype(BF16)


def swa_sample(z3, sinks, cache_k, cache_v, layer, ctab, stab, q_off, k_off, v_off, g_off, n_heads, ts, bb=4):
    bsz, _, p = z3.shape
    wlen = cache_k.shape[1] // KVH_B
    kvw = KVH_B * HEAD_DIM
    assert bsz % bb == 0
    return pl.pallas_call(
        functools.partial(_swa_sample_kernel, gq=n_heads // KVH_B, q_off=q_off, k_off=k_off, v_off=v_off,
                          g_off=g_off, ts=ts, wlen=wlen, bb=bb),
        out_shape=(jax.ShapeDtypeStruct((bsz, SROWS, n_heads * HEAD_DIM), BF16),
                   jax.ShapeDtypeStruct((bsz, SROWS, kvw), F32)),
        grid=(bsz // bb,),
        in_specs=[pl.BlockSpec(memory_space=pltpu.SMEM),
                  pl.BlockSpec((bb, SROWS, p), lambda b: (b, 0, 0)),
                  pl.BlockSpec((bb, wlen * KVH_B, HEAD_DIM), lambda b: (layer * (bsz // bb) + b, 0, 0)),
                  pl.BlockSpec((bb, wlen * KVH_B, HEAD_DIM), lambda b: (layer * (bsz // bb) + b, 0, 0)),
                  pl.BlockSpec((SROWS, LANES), lambda b: (0, 0)),
                  pl.BlockSpec((SROWS, LANES), lambda b: (0, 0))],
        out_specs=(pl.BlockSpec((bb, SROWS, n_heads * HEAD_DIM), lambda b: (b, 0, 0)),
                   pl.BlockSpec((bb, SROWS, kvw), lambda b: (b, 0, 0))),
        compiler_params=_cparams(("arbitrary",)),
        name="swa_sample",
    )(sinks, z3, cache_k, cache_v, ctab, stab)


def _pool_prompt_kernel(x_ref, gc_ref, w_ref, sc_ref, y_ref, halo_sc, *, tt, ngrp):
    i = pl.program_id(1)

    @pl.when(i == 0)
    def _():
        halo_sc[...] = jnp.zeros_like(halo_sc)

    x = x_ref[...]
    xe = jnp.concatenate([halo_sc[...], x], axis=0)
    halo_sc[...] = x[tt - POOL_MAX:, :]
    pos1 = (i * tt + 1 + lax.broadcasted_iota(jnp.int32, (tt, LANES), 0)).astype(F32)
    for g in range(ngrp):
        sl = slice(g * LANES, (g + 1) * LANES)
        w = POOL_WINDOWS[g]
        s = xe[:, sl]
        step = 1
        while step < w:
            s = s + pltpu.roll(s, step, 0)
            step *= 2
        pooled = s[POOL_MAX:, :] / jnp.minimum(float(w), pos1) - x[:, sl]
        y = _dot(pooled.astype(BF16), w_ref[g]) * sc_ref[:, sl]
        y_ref[:, sl] = (y * _silu(gc_ref[:, sl])).astype(BF16)


def pool_prompt(z, bsz, t, d_c, x_off, g_off, w_pool, scale, tt=256):
    nt = t // tt
    ngrp = d_c // LANES
    assert x_off % d_c == 0 and g_off % d_c == 0
    return pl.pallas_call(
        functools.partial(_pool_prompt_kernel, tt=tt, ngrp=ngrp),
        out_shape=jax.ShapeDtypeStruct((bsz * t, d_c), BF16),
        grid=(bsz, nt),
        in_specs=[pl.BlockSpec((tt, d_c), lambda b, i: (b * nt + i, x_off // d_c)),
                  pl.BlockSpec((tt, d_c), lambda b, i: (b * nt + i, g_off // d_c)),
                  pl.BlockSpec((ngrp, LANES, LANES), lambda b, i: (0, 0, 0)),
                  pl.BlockSpec((1, d_c), lambda b, i: (0, 0))],
        out_specs=pl.BlockSpec((tt, d_c), lambda b, i: (b * nt + i, 0)),
        scratch_shapes=[pltpu.VMEM((POOL_MAX, d_c), F32)],
        compiler_params=_cparams(("arbitrary", "arbitrary")),
        name="pool_prompt",
    )(z, z, w_pool, scale.reshape(1, d_c))


def _pool_sample_kernel(x_ref, gc_ref, buf_ref, w_ref, sc_ref, y_ref, *, ts, ngrp, start_pos):
    nbuf = POOL_MAX - 1
    xe = [buf_ref[k] for k in range(nbuf)] + [x_ref[t] for t in range(ts)]
    for t in range(ts):
        cols = []
        for g in range(ngrp):
            sl = slice(g * LANES, (g + 1) * LANES)
            w = POOL_WINDOWS[g]
            s = xe[nbuf + t][:, sl]
            for k in range(1, w):
                s = s + xe[nbuf + t - k][:, sl]
            pooled = s / min(float(w), float(start_pos + t + 1)) - xe[nbuf + t][:, sl]
            cols.append(_dot(pooled.astype(BF16), w_ref[g]))
        y = jnp.concatenate(cols, axis=1) * sc_ref[...]
        y_ref[t] = (y * _silu(gc_ref[t])).astype(BF16)
    for t in range(ts, SROWS):
        y_ref[t] = jnp.zeros(y_ref.shape[1:], BF16)


def pool_sample(xc_t, gc_t, buf_t, w_pool, scale, ts, start_pos):
    _, bsz, d_c = xc_t.shape
    return pl.pallas_call(
        functools.partial(_pool_sample_kernel, ts=ts, ngrp=d_c // LANES, start_pos=start_pos),
        out_shape=jax.ShapeDtypeStruct((SROWS, bsz, d_c), BF16),
        compiler_params=pltpu.CompilerParams(vmem_limit_bytes=VMEM_LIMIT),
        name="pool_sample",
    )(xc_t, gc_t, buf_t, w_pool, scale.reshape(1, d_c))


def _nsa_prep_kernel(q0_ref, q1_ref, q2_ref, kc_ref, vc_ref, ks_ref, vs_ref, kw_ref, vw_ref, c_ref, s_ref,
                     cwk_ref, cwv_ref, qr_ref, kcr_ref, ksr_ref, kwr_ref, kvb_ref, kcmp_ref, vcmp_ref, *, tt, nh):
    c = c_ref[...]
    s = s_ref[...]
    hpb = nh // 3
    for h in range(nh):
        q_ref = (q0_ref, q1_ref, q2_ref)[h // hpb]
        src = slice((h % hpb) * HEAD_DIM, (h % hpb + 1) * HEAD_DIM)
        qr_ref[:, h * HEAD_DIM:(h + 1) * HEAD_DIM] = _rope(q_ref[:, src], c, s).astype(BF16)
    nblk = tt // CMP_BLOCK
    kvw = KVH_D * HEAD_DIM
    for h in range(KVH_D):
        sl = slice(h * HEAD_DIM, (h + 1) * HEAD_DIM)
        kcr = _rope(kc_ref[:, sl], c, s)
        kcr_ref[:, sl] = kcr
        ksr = _rope(ks_ref[:, sl], c, s)
        ksr_ref[:, sl] = ksr
        kwr = _rope(kw_ref[:, sl], c, s)
        kwr_ref[:, sl] = kwr
        for seg, val in enumerate((ksr, vs_ref[:, sl], kwr, vw_ref[:, sl])):
            kvb_ref[:, seg * kvw + h * HEAD_DIM:seg * kvw + (h + 1) * HEAD_DIM] = val.astype(BF16)
        kcmp_ref[0, :, sl] = jnp.sum(kcr.reshape(nblk, CMP_BLOCK, HEAD_DIM) * cwk_ref[:, sl][None], axis=1)
        vcmp_ref[0, :, sl] = jnp.sum(vc_ref[:, sl].reshape(nblk, CMP_BLOCK, HEAD_DIM) * cwv_ref[:, sl][None], axis=1)


def nsa_prep(z, ctab, stab, cwk2, cwv2, bsz, t, offs, n_heads, tt=512):
    nt = t // tt
    qw = n_heads * HEAD_DIM
    kvw = KVH_D * HEAD_DIM
    qb = qw // 3
    assert n_heads % 3 == 0 and offs["q"] % qb == 0
    assert all(offs[k] % kvw == 0 for k in ("kc", "vc", "ks", "vs", "kw", "vw"))
    qspec = lambda k: pl.BlockSpec((tt, qb), lambda b, i: (b * nt + i, offs["q"] // qb + k))
    row = lambda b, i: (b * nt + i, 0)
    kvspec = lambda name: pl.BlockSpec((tt, kvw), lambda b, i: (b * nt + i, offs[name] // kvw))
    return pl.pallas_call(
        functools.partial(_nsa_prep_kernel, tt=tt, nh=n_heads),
        out_shape=(jax.ShapeDtypeStruct((bsz * t, qw), BF16),
                   jax.ShapeDtypeStruct((bsz * t, kvw), F32),
                   jax.ShapeDtypeStruct((bsz * t, kvw), F32),
                   jax.ShapeDtypeStruct((bsz * t, kvw), F32),
                   jax.ShapeDtypeStruct((bsz * t, 4 * kvw), BF16),
                   jax.ShapeDtypeStruct((bsz, t // CMP_BLOCK, kvw), F32),
                   jax.ShapeDtypeStruct((bsz, t // CMP_BLOCK, kvw), F32)),
        grid=(bsz, nt),
        in_specs=[qspec(0), qspec(1), qspec(2),
                  kvspec("kc"), kvspec("vc"), kvspec("ks"), kvspec("vs"), kvspec("kw"), kvspec("vw"),
                  pl.BlockSpec((tt, LANES), lambda b, i: (i, 0)),
                  pl.BlockSpec((tt, LANES), lambda b, i: (i, 0)),
                  pl.BlockSpec((CMP_BLOCK, kvw), lambda b, i: (0, 0)),
                  pl.BlockSpec((CMP_BLOCK, kvw), lambda b, i: (0, 0))],
        out_specs=(pl.BlockSpec((tt, qw), row), pl.BlockSpec((tt, kvw), row), pl.BlockSpec((tt, kvw), row),
                   pl.BlockSpec((tt, kvw), row), pl.BlockSpec((tt, 4 * kvw), row),
                   pl.BlockSpec((1, tt // CMP_BLOCK, kvw), lambda b, i: (b, i, 0)),
                   pl.BlockSpec((1, tt // CMP_BLOCK, kvw), lambda b, i: (b, i, 0))),
        compiler_params=_cparams(("arbitrary", "arbitrary")),
        name="nsa_prep",
    )(z, z, z, z, z, z, z, z, z, ctab, stab, cwk2, cwv2)


def _rank_select(score, nsel):
    nb = score.shape[0]
    jidx = lax.broadcasted_iota(jnp.int32, score.shape, 0)
    cnt = jnp.zeros(score.shape, F32)
    for k in range(nb):
        rk = score[k:k + 1, :]
        tie = jnp.where(jidx > k, 1.0, 0.0)
        cnt = cnt + jnp.where(rk > score, 1.0, jnp.where(rk == score, tie, 0.0))
    return cnt


def _masked_softmax_rows(s, valid):
    m = jnp.max(jnp.where(valid, s, NEG), axis=1, keepdims=True)
    p = jnp.where(valid, jnp.exp(s - m), 0.0)
    den = jnp.sum(p, axis=1, keepdims=True)
    return p * (1.0 / jnp.maximum(den, DENOM_FLOOR))


def _nsa_prompt_kernel(q_ref, kcmp_ref, vcmp_ref, ks_ref, vs_ref, kw_ref, vw_ref, gd0_ref, gd1_ref, gd2_ref,
                       gl_ref, gbias_ref,
                       o_ref, s_sc, mrun_sc, m_sc, lrun_sc, acc_sc, *, t_len, tq, tk, gq, n_heads):
    kv = pl.program_id(1)
    i = pl.program_id(2)
    nb = t_len // CMP_BLOCK
    q0 = i * tq
    bpt = tk // CMP_BLOCK

    rows = gq * tq
    qs = jnp.concatenate([q_ref[:, g * HEAD_DIM:(g + 1) * HEAD_DIM] for g in range(gq)], axis=0)
    kc = kcmp_ref[0].astype(BF16)
    vc = vcmp_ref[0].astype(BF16)

    blk_r = lax.broadcasted_iota(jnp.int32, (rows, nb), 1)
    qp_r = q0 + (lax.broadcasted_iota(jnp.int32, (rows, nb), 0) & (tq - 1))
    ok_r = (blk_r + 1) * CMP_BLOCK - 1 <= qp_r
    p_cmp = _masked_softmax_rows(_dot_nt(qs, kc) * ATTN_SCALE, ok_r)
    o_cmp = _dot(p_cmp.astype(BF16), vc)
    blk_a = lax.broadcasted_iota(jnp.int32, (nb, rows), 0)
    qp_a = q0 + (lax.broadcasted_iota(jnp.int32, (nb, rows), 1) & (tq - 1))
    ok_a = (blk_a + 1) * CMP_BLOCK - 1 <= qp_a
    st = _dot_nt(kc, qs) * ATTN_SCALE
    mt = jnp.max(jnp.where(ok_a, st, NEG), axis=0, keepdims=True)
    pt = jnp.where(ok_a, jnp.exp(st - mt), 0.0)
    dt = jnp.sum(pt, axis=0, keepdims=True)
    pt = pt * (1.0 / jnp.maximum(dt, DENOM_FLOOR))
    imp = pt[:, 0:tq]
    for g in range(1, gq):
        imp = imp + pt[:, g * tq:(g + 1) * tq]
    blk_c = lax.broadcasted_iota(jnp.int32, (nb, tq), 0)
    qp_c = q0 + lax.broadcasted_iota(jnp.int32, (nb, tq), 1)

    cur = jnp.right_shift(qp_c, CMP_SHIFT)
    forced = (blk_c == 0) | (blk_c == cur) | (blk_c == cur - 1)
    score = jnp.where(blk_c > cur, -1.0, jnp.where(forced, FORCE_SCORE, imp))
    cnt = _rank_select(score, N_SEL)
    sel_t = jnp.where((cnt < N_SEL) & (score >= 0.0), 1.0, 0.0).astype(BF16)
    eye = jnp.where(lax.broadcasted_iota(jnp.int32, (tq, tq), 0) == lax.broadcasted_iota(jnp.int32, (tq, tq), 1),
                    1.0, 0.0).astype(BF16)
    sel = _dot_nt(eye, sel_t).astype(BF16)

    n_kt = (q0 + tq + tk - 1) // tk
    qp_k = q0 + lax.broadcasted_iota(jnp.int32, (tq, tk), 0)
    col_k = lax.broadcasted_iota(jnp.int32, (tq, tk), 1)
    e_row = lax.broadcasted_iota(jnp.int32, (nb, tk), 0)
    e_col = jnp.right_shift(lax.broadcasted_iota(jnp.int32, (nb, tk), 1), CMP_SHIFT)
    nlt = tk // LANES
    mrun_sc[...] = jnp.full(mrun_sc.shape, NEG, F32)

    def score_pass(kt, carry):
        k0 = pl.multiple_of(kt * tk, tk)
        s = _dot_nt(qs, ks_ref[pl.ds(k0, tk), :])
        expand = jnp.where(e_row == kt * bpt + e_col, 1.0, 0.0).astype(BF16)
        ok = (_dot(sel, expand) > 0.5) & (k0 + col_k <= qp_k)
        bias = jnp.where(ok, 0.0, NEG)
        sb = s * ATTN_SCALE + jnp.concatenate([bias] * gq, axis=0)
        s_sc[kt] = sb
        mx = sb[:, 0:LANES]
        for c in range(1, nlt):
            mx = jnp.maximum(mx, sb[:, c * LANES:(c + 1) * LANES])
        mrun_sc[...] = jnp.maximum(mrun_sc[...], mx)
        return carry

    lax.fori_loop(0, n_kt, score_pass, 0)
    m_sc[...] = jnp.broadcast_to(jnp.max(mrun_sc[...], axis=1, keepdims=True), m_sc.shape)
    lrun_sc[...] = jnp.zeros(lrun_sc.shape, F32)
    acc_sc[...] = jnp.zeros(acc_sc.shape, F32)

    def value_pass(kt, carry):
        k0 = pl.multiple_of(kt * tk, tk)
        sb = s_sc[kt]
        mrep = m_sc[...]
        ps = [jnp.exp(sb[:, c * LANES:(c + 1) * LANES] - mrep) for c in range(nlt)]
        lsum = ps[0]
        for c in range(1, nlt):
            lsum = lsum + ps[c]
        lrun_sc[...] = lrun_sc[...] + lsum
        p = jnp.concatenate(ps, axis=1).astype(BF16)
        acc_sc[...] = acc_sc[...] + _dot(p, vs_ref[pl.ds(k0, tk), :])
        return carry

    lax.fori_loop(0, n_kt, value_pass, 0)
    o_sel = acc_sc[...] * (1.0 / jnp.sum(lrun_sc[...], axis=1, keepdims=True))

    nwb = (WIN_D + tq - 2) // tq + 1
    wlen = nwb * tq
    w0 = pl.multiple_of(jnp.maximum(i - (nwb - 1), 0) * tq, tq)
    dw = (q0 + lax.broadcasted_iota(jnp.int32, (tq, wlen), 0)) - (w0 + lax.broadcasted_iota(jnp.int32, (tq, wlen), 1))
    bias_w = jnp.where((dw >= 0) & (dw < WIN_D), 0.0, NEG)
    sw = _dot_nt(qs, kw_ref[pl.ds(w0, wlen), :]) * ATTN_SCALE + jnp.concatenate([bias_w] * gq, axis=0)
    pw = jnp.exp(sw - jnp.max(sw, axis=1, keepdims=True))
    lw = jnp.sum(pw, axis=1, keepdims=True)
    o_win = _dot(pw.astype(BF16), vw_ref[pl.ds(w0, wlen), :]) * (1.0 / lw)

    gate = jax.nn.sigmoid(gl_ref[...] + gbias_ref[...])
    hpb = gq // 3
    for g in range(gq):
        rs = slice(g * tq, (g + 1) * tq)
        gs = []
        for br in range(3):
            c0 = br * n_heads + g
            c1 = c0 + gq
            gs.append(jnp.where(kv == 0, gate[:, c0:c0 + 1], gate[:, c1:c1 + 1]))
        yd = gs[0] * o_cmp[rs, :] + gs[1] * o_sel[rs, :] + gs[2] * o_win[rs, :]
        gd_ref = (gd0_ref, gd1_ref, gd2_ref)[g // hpb]
        gd = gd_ref[:, (g % hpb) * HEAD_DIM:(g % hpb + 1) * HEAD_DIM]
        o_ref[:, g * HEAD_DIM:(g + 1) * HEAD_DIM] = (yd * _silu(gd)).astype(BF16)


def nsa_prompt(z, qr, kcmp, vcmp, kvb, gbias, bsz, t, offs, n_heads, tq=128, tk=512):
    nq = t // tq
    gq = n_heads // KVH_D
    wq = gq * HEAD_DIM
    nb = t // CMP_BLOCK
    gw = wq // 3
    assert gq % 3 == 0 and offs["gd"] % gw == 0 and offs["gl"] % LANES == 0 and t % tk == 0
    assert KVH_D == 2
    gdspec = lambda k: pl.BlockSpec((tq, gw), lambda b, h, i: (b * nq + i, offs["gd"] // gw + 3 * h + k))
    res = lambda seg: pl.BlockSpec((t, HEAD_DIM), lambda b, h, i: (b, seg * KVH_D + h))
    rows = gq * tq
    stat = pltpu.VMEM((rows, LANES), F32)
    return pl.pallas_call(
        functools.partial(_nsa_prompt_kernel, t_len=t, tq=tq, tk=tk, gq=gq, n_heads=n_heads),
        out_shape=jax.ShapeDtypeStruct((bsz * t, n_heads * HEAD_DIM), BF16),
        grid=(bsz, KVH_D, nq),
        in_specs=[pl.BlockSpec((tq, wq), lambda b, h, i: (b * nq + i, h)),
                  pl.BlockSpec((1, nb, HEAD_DIM), lambda b, h, i: (b, 0, h)),
                  pl.BlockSpec((1, nb, HEAD_DIM), lambda b, h, i: (b, 0, h)),
                  res(0), res(1), res(2), res(3),
                  gdspec(0), gdspec(1), gdspec(2),
                  pl.BlockSpec((tq, LANES), lambda b, h, i: (b * nq + i, offs["gl"] // LANES)),
                  pl.BlockSpec((1, LANES), lambda b, h, i: (0, 0))],
        out_specs=pl.BlockSpec((tq, wq), lambda b, h, i: (b * nq + i, h)),
        scratch_shapes=[pltpu.VMEM((t // tk, rows, tk), F32), stat, stat, stat, stat],
        compiler_params=_cparams(("arbitrary", "arbitrary", "arbitrary")),
        name="nsa_prompt",
    )(qr, kcmp, vcmp, kvb, kvb, kvb, kvb, z, z, z, z, gbias)


def _cmp_stream_kernel(pt_ref, kpool, vpool, cwk_ref, cwv_ref, kcmp_ref, vcmp_ref, kbuf, vbuf, sem, part_sc, *,
                       ppc, nch, nsteps, page_base, npages):
    b = pl.program_id(0)
    c = pl.program_id(1)
    step = b * nch + c
    slot = lax.rem(step, 2)

    def copies(bb, cc, sl):
        out = []
        for p in range(ppc):
            page = page_base + pt_ref[bb * npages + cc * ppc + p]
            out.append(pltpu.make_async_copy(kpool.at[page], kbuf.at[sl, p], sem.at[0, sl]))
            out.append(pltpu.make_async_copy(vpool.at[page], vbuf.at[sl, p], sem.at[1, sl]))
        return out

    @pl.when(step == 0)
    def _():
        for cp in copies(0, 0, 0):
            cp.start()

    @pl.when(step + 1 < nsteps)
    def _():
        nxt = step + 1
        for cp in copies(nxt // nch, lax.rem(nxt, nch), 1 - slot):
            cp.start()

    for cp in copies(b, c, slot):
        cp.wait()

    nblk = ppc * (PAGE_SIZE // CMP_BLOCK)
    vregs_per_blk = CMP_BLOCK * KVH_D // SUBLANES
    for buf, w_ref, out_ref in ((kbuf, cwk_ref, kcmp_ref), (vbuf, cwv_ref, vcmp_ref)):
        x = buf[slot] * w_ref[...][None]
        part_sc[...] = jnp.sum(x.reshape(nblk, vregs_per_blk, SUBLANES, LANES), axis=1).reshape(nblk * SUBLANES, LANES)
        for h in range(KVH_D):
            acc = part_sc[pl.ds(h, nblk, stride=SUBLANES), :]
            for k in range(1, SUBLANES // KVH_D):
                acc = acc + part_sc[pl.ds(k * KVH_D + h, nblk, stride=SUBLANES), :]
            out_ref[0, :, h * HEAD_DIM:(h + 1) * HEAD_DIM] = acc


def cmp_stream(pt_flat, kpool, vpool, cwk_page, cwv_page, bsz, npages, page_base, ppc=16):
    prow = kpool.shape[1]
    width = KVH_D * HEAD_DIM
    nch = npages // ppc
    bpp = PAGE_SIZE // CMP_BLOCK
    grid_spec = pltpu.PrefetchScalarGridSpec(
        num_scalar_prefetch=1,
        grid=(bsz, nch),
        in_specs=[pl.BlockSpec(memory_space=pl.ANY), pl.BlockSpec(memory_space=pl.ANY),
                  pl.BlockSpec((prow, LANES), lambda b, c, pt: (0, 0)),
                  pl.BlockSpec((prow, LANES), lambda b, c, pt: (0, 0))],
        out_specs=(pl.BlockSpec((1, ppc * bpp, width), lambda b, c, pt: (b, c, 0)),) * 2,
        scratch_shapes=[pltpu.VMEM((2, ppc, prow, LANES), F32), pltpu.VMEM((2, ppc, prow, LANES), F32),
                        pltpu.SemaphoreType.DMA((2, 2)), pltpu.VMEM((ppc * bpp * SUBLANES, LANES), F32)],
    )
    return pl.pallas_call(
        functools.partial(_cmp_stream_kernel, ppc=ppc, nch=nch, nsteps=bsz * nch, page_base=page_base,
                          npages=npages),
        out_shape=(jax.ShapeDtypeStruct((bsz, npages * bpp, width), F32),) * 2,
        grid_spec=grid_spec,
        compiler_params=_cparams(("arbitrary", "arbitrary")),
        name="cmp_stream",
    )(pt_flat, kpool, vpool, cwk_page, cwv_page)


def _nsa_sample_cmp_kernel(z_ref, kcp_ref, vcp_ref, c_ref, s_ref, cwk_ref, cwv_ref,
                           ocmp_ref, imp_ref, knew_ref, *, offs, gq, ts, n_past, past, bb):
    c = c_ref[...]
    s = s_ref[...]
    nbp = n_past + SUBLANES
    blk_r = lax.broadcasted_iota(jnp.int32, (gq * SROWS, nbp), 1)
    qp_r = past + (lax.broadcasted_iota(jnp.int32, (gq * SROWS, nbp), 0) & (SROWS - 1))
    ok_r = (blk_r + 1) * CMP_BLOCK - 1 <= qp_r
    blk_c = lax.broadcasted_iota(jnp.int32, (nbp, LANES), 0)
    qp_c = past + (lax.broadcasted_iota(jnp.int32, (nbp, LANES), 1) & (SROWS - 1))
    ok_c = (blk_c + 1) * CMP_BLOCK - 1 <= qp_c
    row8 = lax.broadcasted_iota(jnp.int32, (SROWS, HEAD_DIM), 0)
    is_new = row8 < ts
    qpad = jnp.zeros((LANES - gq * SROWS, HEAD_DIM), BF16)
    kvw = KVH_D * HEAD_DIM
    for bi in range(bb):
        for h in range(KVH_D):
            hs = slice(h * HEAD_DIM, (h + 1) * HEAD_DIM)
            kcr = _rope(z_ref[bi, :, offs["kc"] + h * HEAD_DIM:offs["kc"] + (h + 1) * HEAD_DIM], c, s)
            ksr = _rope(z_ref[bi, :, offs["ks"] + h * HEAD_DIM:offs["ks"] + (h + 1) * HEAD_DIM], c, s)
            kwr = _rope(z_ref[bi, :, offs["kw"] + h * HEAD_DIM:offs["kw"] + (h + 1) * HEAD_DIM], c, s)
            knew_ref[bi, :, h * HEAD_DIM:(h + 1) * HEAD_DIM] = kcr
            knew_ref[bi, :, kvw + h * HEAD_DIM:kvw + (h + 1) * HEAD_DIM] = ksr
            knew_ref[bi, :, 2 * kvw + h * HEAD_DIM:2 * kvw + (h + 1) * HEAD_DIM] = kwr
            vcn = z_ref[bi, :, offs["vc"] + h * HEAD_DIM:offs["vc"] + (h + 1) * HEAD_DIM]
            nk = jnp.sum(jnp.where(is_new, kcr * cwk_ref[0:SROWS, hs], 0.0), axis=0, keepdims=True)
            nv = jnp.sum(jnp.where(is_new, vcn * cwv_ref[0:SROWS, hs], 0.0), axis=0, keepdims=True)
            nk8 = jnp.where(row8 == 0, jnp.broadcast_to(nk, (SROWS, HEAD_DIM)), 0.0)
            nv8 = jnp.where(row8 == 0, jnp.broadcast_to(nv, (SROWS, HEAD_DIM)), 0.0)
            kall = jnp.concatenate([kcp_ref[bi, :, hs], nk8], axis=0).astype(BF16)
            vall = jnp.concatenate([vcp_ref[bi, :, hs], nv8], axis=0).astype(BF16)
            qs = jnp.concatenate(
                [_rope(z_ref[bi, :, offs["q"] + (h * gq + g) * HEAD_DIM:offs["q"] + (h * gq + g + 1) * HEAD_DIM], c, s)
                 for g in range(gq)], axis=0).astype(BF16)
            p = _masked_softmax_rows(_dot_nt(qs, kall) * ATTN_SCALE, ok_r)
            o = _dot(p.astype(BF16), vall)
            for g in range(gq):
                hd = h * gq + g
                ocmp_ref[bi, :, hd * HEAD_DIM:(hd + 1) * HEAD_DIM] = o[g * SROWS:(g + 1) * SROWS, :]
            st = _dot_nt(kall, jnp.concatenate([qs, qpad], axis=0)) * ATTN_SCALE
            mt = jnp.max(jnp.where(ok_c, st, NEG), axis=0, keepdims=True)
            pt = jnp.where(ok_c, jnp.exp(st - mt), 0.0)
            dt = jnp.sum(pt, axis=0, keepdims=True)
            pt = pt * (1.0 / jnp.maximum(dt, DENOM_FLOOR))
            imp = pt
            for g in range(1, gq):
                imp = imp + pltpu.roll(pt, LANES - g * SROWS, 1)
            imp_ref[bi, h] = imp


def nsa_sample_cmp(z3, kcp, vcp, ctab, stab, cwk2, cwv2, offs, n_heads, ts, past, bb=4):
    bsz, _, p = z3.shape
    n_past = kcp.shape[1]
    kvw = KVH_D * HEAD_DIM
    nbp = n_past + SUBLANES
    assert bsz % bb == 0
    return pl.pallas_call(
        functools.partial(_nsa_sample_cmp_kernel, offs=offs, gq=n_heads // KVH_D, ts=ts, n_past=n_past, past=past,
                          bb=bb),
        out_shape=(jax.ShapeDtypeStruct((bsz, SROWS, n_heads * HEAD_DIM), F32),
                   jax.ShapeDtypeStruct((bsz, KVH_D, nbp, LANES), F32),
                   jax.ShapeDtypeStruct((bsz, SROWS, 3 * kvw), F32)),
        grid=(bsz // bb,),
        in_specs=[pl.BlockSpec((bb, SROWS, p), lambda b: (b, 0, 0)),
                  pl.BlockSpec((bb, n_past, kvw), lambda b: (b, 0, 0)),
                  pl.BlockSpec((bb, n_past, kvw), lambda b: (b, 0, 0)),
                  pl.BlockSpec((SROWS, LANES), lambda b: (0, 0)),
                  pl.BlockSpec((SROWS, LANES), lambda b: (0, 0)),
                  pl.BlockSpec((CMP_BLOCK, kvw), lambda b: (0, 0)),
                  pl.BlockSpec((CMP_BLOCK, kvw), lambda b: (0, 0))],
        out_specs=(pl.BlockSpec((bb, SROWS, n_heads * HEAD_DIM), lambda b: (b, 0, 0)),
                   pl.BlockSpec((bb, KVH_D, nbp, LANES), lambda b: (b, 0, 0, 0)),
                   pl.BlockSpec((bb, SROWS, 3 * kvw), lambda b: (b, 0, 0))),
        compiler_params=_cparams(("arbitrary",)),
        name="nsa_sample_cmp",
    )(z3, kcp, vcp, ctab, stab, cwk2, cwv2)


def _select_sample_kernel(imp_ref, qpos_ref, idx_ref, score_sc, cnt_sc, *, nbp, nsel):
    shape = score_sc.shape
    blk = lax.broadcasted_iota(jnp.int32, shape, 0)
    cur = jnp.right_shift(qpos_ref[...], CMP_SHIFT)
    forced = (blk == 0) | (blk == cur) | (blk == cur - 1)
    score_sc[...] = jnp.where(blk > cur, -1.0, jnp.where(forced, FORCE_SCORE, imp_ref[...]))
    cnt_sc[...] = jnp.zeros(shape, F32)

    def body(k, carry):
        rk = score_sc[pl.ds(k, 1), :]
        sc = score_sc[...]
        tie = jnp.where(blk > k, 1.0, 0.0)
        cnt_sc[...] = cnt_sc[...] + jnp.where(rk > sc, 1.0, jnp.where(rk == sc, tie, 0.0))
        return carry

    lax.fori_loop(0, nbp, body, 0)
    cnt = cnt_sc[...]
    ok = score_sc[...] >= 0.0
    blk_f = blk.astype(F32)
    for r in range(nsel):
        picked = jnp.sum(jnp.where((cnt == float(r)) & ok, blk_f, 0.0), axis=0, keepdims=True)
        idx_ref[r:r + 1, :] = picked.astype(jnp.int32)


def select_sample(imp_t, qpos):
    nbp, nc = imp_t.shape
    return pl.pallas_call(
        functools.partial(_select_sample_kernel, nbp=nbp, nsel=N_SEL),
        out_shape=jax.ShapeDtypeStruct((N_SEL, nc), jnp.int32),
        scratch_shapes=[pltpu.VMEM((nbp, nc), F32), pltpu.VMEM((nbp, nc), F32)],
        compiler_params=pltpu.CompilerParams(vmem_limit_bytes=VMEM_LIMIT),
        name="select_sample",
    )(imp_t, qpos)


def _nsa_sample_sel_kernel(idx_ref, pt_ref, z_ref, ocmp_ref, knew_ref, wk_ref, wv_ref, c_ref, s_ref, gbias_ref,
                           kpool, vpool, y_ref, kbuf, vbuf, sem, *, offs, gq, n_heads, ts, n_past, past,
                           page_base, npages, wlen):
    b = pl.program_id(0)
    c = c_ref[...]
    s = s_ref[...]
    bpp = PAGE_SIZE // CMP_BLOCK
    kvw = KVH_D * HEAD_DIM
    blk_rows = CMP_BLOCK * KVH_D

    nslots = KVH_D * ts * N_SEL

    def slot_copies(sidx):
        blk = idx_ref[b * nslots + sidx]
        past_blk = jnp.minimum(blk, n_past - 1)
        page = page_base + pt_ref[b * npages + lax.div(past_blk, bpp)]
        row0 = pl.multiple_of(lax.rem(past_blk, bpp) * blk_rows, blk_rows)
        return (pltpu.make_async_copy(kpool.at[page, pl.ds(row0, blk_rows), :], kbuf.at[sidx], sem.at[0]),
                pltpu.make_async_copy(vpool.at[page, pl.ds(row0, blk_rows), :], vbuf.at[sidx], sem.at[1]))

    def issue(sidx, carry):
        for cp in slot_copies(sidx):
            cp.start()
        return carry

    lax.fori_loop(0, nslots, issue, 0)

    row8 = lax.broadcasted_iota(jnp.int32, (SROWS, HEAD_DIM), 0)
    kpad = jnp.zeros((LANES - SROWS, HEAD_DIM), F32)
    qi = lax.broadcasted_iota(jnp.int32, (SROWS, wlen + LANES), 0)
    ci = lax.broadcasted_iota(jnp.int32, (SROWS, wlen + LANES), 1)
    dw = jnp.where(ci < wlen, qi + wlen - ci, qi - (ci - wlen))
    ok_w = (dw >= 0) & (dw < WIN_D) & ((ci < wlen) | (ci - wlen < ts))
    ok_wg = jnp.concatenate([ok_w] * gq, axis=0)
    qr = []
    o_win = []
    for h in range(KVH_D):
        hs = slice(h * HEAD_DIM, (h + 1) * HEAD_DIM)
        kwn = knew_ref[0, :, 2 * kvw + h * HEAD_DIM:2 * kvw + (h + 1) * HEAD_DIM]
        vwn = z_ref[0, :, offs["vw"] + h * HEAD_DIM:offs["vw"] + (h + 1) * HEAD_DIM]
        kw = jnp.concatenate([wk_ref[0, pl.ds(h, wlen, stride=KVH_D), :], kwn, kpad], axis=0).astype(BF16)
        vw = jnp.concatenate([wv_ref[0, pl.ds(h, wlen, stride=KVH_D), :], vwn, kpad], axis=0).astype(BF16)
        qh = [_rope(z_ref[0, :, offs["q"] + (h * gq + g) * HEAD_DIM:offs["q"] + (h * gq + g + 1) * HEAD_DIM], c, s)
              for g in range(gq)]
        qr.extend(qh)
        qs = jnp.concatenate(qh, axis=0).astype(BF16)
        pw = _masked_softmax_rows(_dot_nt(qs, kw) * ATTN_SCALE, ok_wg)
        ow = _dot(pw.astype(BF16), vw)
        o_win.extend([ow[g * SROWS:(g + 1) * SROWS, :] for g in range(gq)])

    def drain(sidx, carry):
        for cp in slot_copies(sidx):
            cp.wait()
        return carry

    lax.fori_loop(0, nslots, drain, 0)

    nkeys = N_SEL * CMP_BLOCK
    colk = lax.broadcasted_iota(jnp.int32, (SROWS, nkeys + LANES), 1)
    slot_of_col = jnp.right_shift(colk, CMP_SHIFT)
    in_blk = colk & (CMP_BLOCK - 1)
    o_sel = [jnp.zeros((SROWS, HEAD_DIM), F32) for _ in range(n_heads)]
    for h in range(KVH_D):
        ksn = knew_ref[0, :, kvw + h * HEAD_DIM:kvw + (h + 1) * HEAD_DIM]
        vsn = z_ref[0, :, offs["vs"] + h * HEAD_DIM:offs["vs"] + (h + 1) * HEAD_DIM]
        for t in range(ts):
            qt = jnp.zeros((SROWS, HEAD_DIM), F32)
            for g in range(gq):
                qt = jnp.where(row8 == g, jnp.broadcast_to(qr[h * gq + g][t:t + 1, :], (SROWS, HEAD_DIM)), qt)
            qt = qt.astype(BF16)
            s0 = (h * ts + t) * N_SEL
            kg = kbuf[pl.ds(s0, N_SEL), pl.ds(h, CMP_BLOCK, stride=KVH_D), :].reshape(nkeys, HEAD_DIM)
            vg = vbuf[pl.ds(s0, N_SEL), pl.ds(h, CMP_BLOCK, stride=KVH_D), :].reshape(nkeys, HEAD_DIM)
            kall = jnp.concatenate([kg, ksn, kpad], axis=0).astype(BF16)
            vall = jnp.concatenate([vg, vsn, kpad], axis=0).astype(BF16)
            far = past + ts + SROWS
            kpos = jnp.where(colk >= nkeys, past + (colk - nkeys), 0)
            kpos = jnp.where((colk >= nkeys) & (colk - nkeys >= ts), far, kpos)
            n_new = jnp.int32(0)
            for r in range(N_SEL):
                blk = idx_ref[((b * KVH_D + h) * ts + t) * N_SEL + r]
                n_new = n_new + jnp.where(blk >= n_past, 1, 0)
                base = jnp.where(blk < n_past, blk * CMP_BLOCK, far)
                kpos = jnp.where(slot_of_col == r, base + in_blk, kpos)
            kpos = jnp.where((colk >= nkeys) & (n_new == 0), far, kpos)
            ok = kpos <= past + t
            p = _masked_softmax_rows(_dot_nt(qt, kall) * ATTN_SCALE, ok)
            res = _dot(p.astype(BF16), vall)
            for g in range(gq):
                hd = h * gq + g
                o_sel[hd] = jnp.where(row8 == t, jnp.broadcast_to(res[g:g + 1, :], (SROWS, HEAD_DIM)), o_sel[hd])

    ngate = 3 * n_heads
    gate = jax.nn.sigmoid(z_ref[0, :, offs["gl"]:offs["gl"] + ngate] + gbias_ref[:, 0:ngate])
    for hd in range(n_heads):
        sl = slice(hd * HEAD_DIM, (hd + 1) * HEAD_DIM)
        yd = (gate[:, hd:hd + 1] * ocmp_ref[0, :, sl]
              + gate[:, n_heads + hd:n_heads + hd + 1] * o_sel[hd]
              + gate[:, 2 * n_heads + hd:2 * n_heads + hd + 1] * o_win[hd])
        y_ref[0, :, sl] = (yd * _silu(z_ref[0, :, offs["gd"] + hd * HEAD_DIM:offs["gd"] + (hd + 1) * HEAD_DIM])
                           ).astype(BF16)


def nsa_sample_sel(idx_flat, pt_flat, z3, ocmp, knew, wk, wv, ctab, stab, gbias, kpool, vpool, offs, n_heads,
                   ts, past, page_base, npages, layer, n_layers_b):
    bsz, _, p = z3.shape
    wlen = wk.shape[1] // KVH_D
    kvw = KVH_D * HEAD_DIM
    qw = n_heads * HEAD_DIM
    n_past = npages * (PAGE_SIZE // CMP_BLOCK)
    grid_spec = pltpu.PrefetchScalarGridSpec(
        num_scalar_prefetch=2,
        grid=(bsz,),
        in_specs=[pl.BlockSpec((1, SROWS, p), lambda b, i, t: (b, 0, 0)),
                  pl.BlockSpec((1, SROWS, qw), lambda b, i, t: (b, 0, 0)),
                  pl.BlockSpec((1, SROWS, 3 * kvw), lambda b, i, t: (b, 0, 0)),
                  pl.BlockSpec((1, wlen * KVH_D, HEAD_DIM), lambda b, i, t: (layer * n_layers_b + b, 0, 0)),
                  pl.BlockSpec((1, wlen * KVH_D, HEAD_DIM), lambda b, i, t: (layer * n_layers_b + b, 0, 0)),
                  pl.BlockSpec((SROWS, LANES), lambda b, i, t: (0, 0)),
                  pl.BlockSpec((SROWS, LANES), lambda b, i, t: (0, 0)),
                  pl.BlockSpec((1, LANES), lambda b, i, t: (0, 0)),
                  pl.BlockSpec(memory_space=pl.ANY), pl.BlockSpec(memory_space=pl.ANY)],
        out_specs=pl.BlockSpec((1, SROWS, qw), lambda b, i, t: (b, 0, 0)),
        scratch_shapes=[pltpu.VMEM((KVH_D * ts * N_SEL, CMP_BLOCK * KVH_D, HEAD_DIM), F32),
                        pltpu.VMEM((KVH_D * ts * N_SEL, CMP_BLOCK * KVH_D, HEAD_DIM), F32),
                        pltpu.SemaphoreType.DMA((2,))],
    )
    return pl.pallas_call(
        functools.partial(_nsa_sample_sel_kernel, offs=offs, gq=n_heads // KVH_D, n_heads=n_heads, ts=ts,
                          n_past=n_past, past=past, page_base=page_base, npages=npages, wlen=wlen),
        out_shape=jax.ShapeDtypeStruct((bsz, SROWS, qw), BF16),
        grid_spec=grid_spec,
        compiler_params=_cparams(("arbitrary",)),
        name="nsa_sample_sel",
    )(idx_flat, pt_flat, z3, ocmp, knew, wk, wv, ctab, stab, gbias, kpool, vpool)


EVEN_OFFS = dict(xa=0, ga=1024, q=2048, k=3072, v=3328, gb=3584)


def _odd_layout(d_c, d_d, kvw, n_gate):
    offs = {}
    pos = 0
    for name, width in (("xc", d_c), ("gc", d_c), ("q", d_d), ("kc", kvw), ("vc", kvw), ("ks", kvw), ("vs", kvw),
                        ("kw", kvw), ("vw", kvw), ("gd", d_d), ("gl", n_gate)):
        offs[name] = pos
        pos += width
    return offs, pos


def _pick_tile(n, prefs):
    for t in prefs:
        if n % t == 0:
            return t
    raise ValueError(n)


def kernel(x_prompt, x_sample, state_lru_h, state_lru_conv, cache_swa_k, cache_swa_v, state_pool, cache_nsa_cmp_k, cache_nsa_cmp_v, cache_nsa_sel_k, cache_nsa_sel_v, cache_nsa_win_k, cache_nsa_win_v, page_table, norm_g, final_g, w_in_even, conv_w, conv_b, w_rgate, b_rgate, w_igate, b_igate, lru_lambda, swa_sinks, w_out_even, w_in_odd, w_pool, pool_scale, cmp_wk, cmp_wv, nsa_gate_b, w_out_odd):
    bp, tp, d = x_prompt.shape
    bs, ts, _ = x_sample.shape
    depth = norm_g.shape[0]
    npages = page_table.shape[1]
    past = npages * PAGE_SIZE
    n_phys = cache_nsa_cmp_k.shape[1]
    d_a = state_lru_h.shape[-1]
    n_heads_b = swa_sinks.shape[1]
    d_b = n_heads_b * HEAD_DIM
    d_c = state_pool.shape[-1]
    n_heads_d = nsa_gate_b.shape[1] // 3
    d_d = n_heads_d * HEAD_DIM
    kvw_b = KVH_B * HEAD_DIM
    kvw_d = KVH_D * HEAD_DIM
    assert ts <= SROWS and past % CMP_BLOCK == 0 and (past + ts - 1) // CMP_BLOCK == past // CMP_BLOCK

    hp = x_prompt.reshape(bp * tp, d)
    hs = jnp.pad(x_sample, ((0, 0), (0, SROWS - ts), (0, 0))).reshape(bs * SROWS, d)
    ms = bs * SROWS
    ctab_p, stab_p = rope_tables(tp, 0)
    ctab_s, stab_s = rope_tables(SROWS, past)
    tm_p = _pick_tile(bp * tp, (1024, 512, 256, 128))
    pt_flat = page_table.reshape(-1)

    odd_offs, odd_total = _odd_layout(d_c, d_d, kvw_d, 3 * n_heads_d)
    ev_p, ev_s, od_p, od_s = [], [], [], []
    for i in range(depth):
        j = i // 2
        if i % 2 == 0:
            p_even = w_in_even.shape[2]
            w_in = w_in_even[j].astype(BF16)
            tn = _pick_tile(p_even, (512, 256, 128))
            zp = norm_proj(hp, norm_g[i], w_in, tm_p, tn)
            zs = norm_proj(hs, norm_g[i], w_in, ms, tn)
            wr = w_rgate[j].astype(BF16)
            wi = w_igate[j].astype(BF16)
            lru_args = (conv_w[j], conv_b[j], wr, b_rgate[j], wi, b_igate[j], lru_lambda[j])
            eo = EVEN_OFFS
            ya_p, hl_p = lru_prompt(zp, bp, tp, d_a, *lru_args)
            yb_p, kout_p = swa_prompt(zp, swa_sinks[j], ctab_p, stab_p, bp, tp, eo["q"], eo["k"], eo["v"], eo["gb"],
                                      n_heads_b)
            wo = w_out_even[j].astype(BF16)
            tn_o = _pick_tile(d, (512, 256, 128))
            hp = out_proj(ya_p, yb_p, wo[:d_a], wo[d_a:], hp, tm_p, tn_o)
            zp3 = zp.reshape(bp, tp, p_even)
            keep = min(WIN_B, tp)
            ev_p.append((hl_p[:, 0], zp3[:, tp - (CONV_W - 1):, eo["xa"]:eo["xa"] + d_a],
                         kout_p.transpose(0, 2, 1, 3)[:, WIN_B - keep:],
                         zp3[:, tp - keep:, eo["v"]:eo["v"] + kvw_b].reshape(bp, keep, KVH_B, HEAD_DIM)))
            zs3 = zs.reshape(bs, SROWS, p_even)
            zst = zs3.transpose(1, 0, 2)
            ya_t, hl_s = lru_sample(zst[:, :, eo["xa"]:eo["xa"] + d_a], zst[:, :, eo["ga"]:eo["ga"] + d_a],
                                    state_lru_conv[j].transpose(1, 0, 2), state_lru_h[j], *lru_args, ts)
            ya_s = ya_t.transpose(1, 0, 2).reshape(ms, d_a)
            rows3 = lambda a: a.reshape(a.shape[0] * a.shape[1], a.shape[2] * a.shape[3], a.shape[4])
            yb_s, knew = swa_sample(zs3, swa_sinks[j], rows3(cache_swa_k), rows3(cache_swa_v), j, ctab_s, stab_s,
                                    eo["q"], eo["k"], eo["v"], eo["gb"], n_heads_b, ts)
            hs = out_proj(ya_s, yb_s.reshape(ms, d_b), wo[:d_a], wo[d_a:], hs, ms, tn_o)
            wlen = cache_swa_k.shape[2]
            new_conv = jnp.concatenate([state_lru_conv[j], zs3[:, :ts, eo["xa"]:eo["xa"] + d_a]], axis=1)[:, -(CONV_W - 1):]
            new_k = jnp.concatenate([cache_swa_k[j], knew[:, :ts].reshape(bs, ts, KVH_B, HEAD_DIM)], axis=1)[:, -wlen:]
            new_v = jnp.concatenate([cache_swa_v[j], zs3[:, :ts, eo["v"]:eo["v"] + kvw_b].reshape(bs, ts, KVH_B, HEAD_DIM)],
                                    axis=1)[:, -wlen:]
            ev_s.append((hl_s, new_conv, new_k, new_v))
        else:
            oo = odd_offs
            w_in = w_in_odd[j].astype(BF16)
            tn = 512
            zp = norm_proj(hp, norm_g[i], w_in, tm_p, tn)
            zs = norm_proj(hs, norm_g[i], w_in, ms, tn)
            wp = w_pool[j].astype(BF16)
            cwk2 = jnp.repeat(cmp_wk[j], HEAD_DIM, axis=1)
            cwv2 = jnp.repeat(cmp_wv[j], HEAD_DIM, axis=1)
            gbias = jnp.pad(nsa_gate_b[j], (0, LANES - 3 * n_heads_d)).reshape(1, LANES)
            wo = w_out_odd[j].astype(BF16)
            tn_o = _pick_tile(d, (512, 256, 128))
            yc_p = pool_prompt(zp, bp, tp, d_c, oo["xc"], oo["gc"], wp, pool_scale[j])
            qr, kcr, ksr, kwr, kvb, kcmp, vcmp = nsa_prep(zp, ctab_p, stab_p, cwk2, cwv2, bp, tp, oo, n_heads_d)
            yd_p = nsa_prompt(zp, qr, kcmp, vcmp, kvb, gbias, bp, tp, oo, n_heads_d)
            hp = out_proj(yc_p, yd_p, wo[:d_c], wo[d_c:], hp, tm_p, tn_o)
            zp3 = zp.reshape(bp, tp, odd_total)
            kv4 = lambda a: a.reshape(bp, tp, KVH_D, HEAD_DIM)
            keep = min(WIN_D, tp)
            od_p.append((zp3[:, tp - (POOL_MAX - 1):, oo["xc"]:oo["xc"] + d_c],
                         kv4(kcr), kv4(zp3[:, :, oo["vc"]:oo["vc"] + kvw_d]),
                         kv4(ksr), kv4(zp3[:, :, oo["vs"]:oo["vs"] + kvw_d]),
                         kv4(kwr)[:, tp - keep:], kv4(zp3[:, :, oo["vw"]:oo["vw"] + kvw_d])[:, tp - keep:]))
            zs3 = zs.reshape(bs, SROWS, odd_total)
            zst = zs3.transpose(1, 0, 2)
            yc_t = pool_sample(zst[:, :, oo["xc"]:oo["xc"] + d_c], zst[:, :, oo["gc"]:oo["gc"] + d_c],
                               state_pool[j].transpose(1, 0, 2), wp, pool_scale[j], ts, past)
            yc_s = yc_t.transpose(1, 0, 2).reshape(ms, d_c)
            n_layers_odd = cache_nsa_cmp_k.shape[0]
            pool3 = lambda a: a.reshape(n_layers_odd * n_phys, PAGE_SIZE * KVH_D, HEAD_DIM)
            page_w = lambda w: jnp.broadcast_to(jnp.tile(w, (PAGE_SIZE // CMP_BLOCK, 1)).reshape(-1, 1),
                                                (PAGE_SIZE * KVH_D, LANES))
            cwk_page = page_w(cmp_wk[j])
            cwv_page = page_w(cmp_wv[j])
            kcp, vcp = cmp_stream(pt_flat, pool3(cache_nsa_cmp_k), pool3(cache_nsa_cmp_v), cwk_page, cwv_page,
                                  bs, npages, j * n_phys)
            ocmp, imp, knew = nsa_sample_cmp(zs3, kcp, vcp, ctab_s, stab_s, cwk2, cwv2, oo, n_heads_d, ts, past)
            nbp = imp.shape[2]
            assert nbp >= N_SEL
            imp_t = imp[:, :, :, :ts].transpose(2, 0, 1, 3).reshape(nbp, bs * KVH_D * ts)
            qpos = jnp.tile(past + jnp.arange(ts, dtype=jnp.int32), bs * KVH_D).reshape(1, -1)
            idx = select_sample(imp_t, qpos)
            idx_flat = idx.T.reshape(-1)
            wlen = cache_nsa_win_k.shape[2]
            win3 = lambda a: a.reshape(n_layers_odd * bs, wlen * KVH_D, HEAD_DIM)
            yd_s = nsa_sample_sel(idx_flat, pt_flat, zs3, ocmp, knew, win3(cache_nsa_win_k), win3(cache_nsa_win_v),
                                  ctab_s, stab_s, gbias, pool3(cache_nsa_sel_k), pool3(cache_nsa_sel_v), oo,
                                  n_heads_d, ts, past, j * n_phys, npages, j, bs)
            hs = out_proj(yc_s, yd_s.reshape(ms, d_d), wo[:d_c], wo[d_c:], hs, ms, tn_o)
            kv4s = lambda a: a[:, :ts].reshape(bs, ts, KVH_D, HEAD_DIM)
            new_pool = jnp.concatenate([state_pool[j], zs3[:, :ts, oo["xc"]:oo["xc"] + d_c]], axis=1)[:, -(POOL_MAX - 1):]
            kwn = kv4s(knew[:, :, 2 * kvw_d:3 * kvw_d])
            vwn = kv4s(zs3[:, :, oo["vw"]:oo["vw"] + kvw_d])
            od_s.append((new_pool,
                         kv4s(knew[:, :, 0:kvw_d]), kv4s(zs3[:, :, oo["vc"]:oo["vc"] + kvw_d]),
                         kv4s(knew[:, :, kvw_d:2 * kvw_d]), kv4s(zs3[:, :, oo["vs"]:oo["vs"] + kvw_d]),
                         jnp.concatenate([cache_nsa_win_k[j], kwn], axis=1)[:, -wlen:],
                         jnp.concatenate([cache_nsa_win_v[j], vwn], axis=1)[:, -wlen:]))

    y_prompt = final_norm(hp, final_g, tm_p).reshape(bp, tp, d)
    y_sample = final_norm(hs, final_g, ms).reshape(bs, SROWS, d)[:, :ts]

    def field(lst, k):
        return jnp.stack([st[k] for st in lst], axis=0)

    return (y_prompt, y_sample,
            field(ev_p, 0), field(ev_p, 1), field(ev_p, 2), field(ev_p, 3),
            field(od_p, 0), field(od_p, 1), field(od_p, 2), field(od_p, 3), field(od_p, 4), field(od_p, 5), field(od_p, 6),
            field(ev_s, 0), field(ev_s, 1), field(ev_s, 2), field(ev_s, 3),
            field(od_s, 0), field(od_s, 1), field(od_s, 2), field(od_s, 3), field(od_s, 4), field(od_s, 5), field(od_s, 6))
```

```python
import functools

import jax
import jax.numpy as jnp
import numpy as np
from jax import lax
from jax.experimental import pallas as pl
from jax.experimental.pallas import tpu as pltpu

F32 = jnp.float32
BF16 = jnp.bfloat16

LANES = 128
SUBLANES = 8
VMEM_LIMIT = 48 * 1024 * 1024

HEAD_DIM = 128
ATTN_SCALE = HEAD_DIM ** -0.5
ROPE_THETA = 10000.0
NORM_EPS = 1e-6
PAGE_SIZE = 128
CONV_W = 4
LRU_C = 8.0
KVH_B = 2
WIN_B = 128
POOL_WINDOWS = (2, 4, 8, 16)
POOL_MAX = 16
KVH_D = 2
CMP_BLOCK = 64
CMP_SHIFT = 6
N_SEL = 16
WIN_D = 512
FORCE_SCORE = 1e6
DENOM_FLOOR = 1e-30
NEG = -1e30
SROWS = 8


def _cparams(sem):
    return pltpu.CompilerParams(dimension_semantics=sem, vmem_limit_bytes=VMEM_LIMIT)


def _dot(a, b):
    return jnp.dot(a, b, preferred_element_type=F32)


def _dot_nt(a, b):
    return lax.dot_general(a, b, (((1,), (1,)), ((), ())), preferred_element_type=F32)


def _silu(x):
    return x * jax.nn.sigmoid(x)


def _rope(x, c, s):
    return x * c + pltpu.roll(x, HEAD_DIM // 2, 1) * s


def _rope_tab_kernel(inv_ref, sgn_ref, c_ref, s_ref, *, start, rows):
    i = pl.program_id(0)
    pos = start + i * rows + lax.broadcasted_iota(jnp.int32, (rows, LANES), 0)
    ang = pos.astype(F32) * inv_ref[...]
    c_ref[...] = jnp.cos(ang)
    s_ref[...] = jnp.sin(ang) * sgn_ref[...]


def rope_tables(n_rows, start):
    half = HEAD_DIM // 2
    inv = ROPE_THETA ** (-jnp.arange(half, dtype=F32) / half)
    inv2 = jnp.concatenate([inv, inv]).reshape(1, LANES)
    sgn = jnp.concatenate([-jnp.ones((half,), F32), jnp.ones((half,), F32)]).reshape(1, LANES)
    rows = min(n_rows, 512)
    assert n_rows % rows == 0
    return pl.pallas_call(
        functools.partial(_rope_tab_kernel, start=start, rows=rows),
        out_shape=(jax.ShapeDtypeStruct((n_rows, LANES), F32),) * 2,
        grid=(n_rows // rows,),
        in_specs=[pl.BlockSpec((1, LANES), lambda i: (0, 0))] * 2,
        out_specs=(pl.BlockSpec((rows, LANES), lambda i: (i, 0)),) * 2,
        compiler_params=_cparams(("arbitrary",)),
        name="rope_tables",
    )(inv2, sgn)


def _norm_proj_kernel(x_ref, g_ref, w_ref, o_ref, xn_ref):
    @pl.when(pl.program_id(1) == 0)
    def _():
        x = x_ref[...]
        ms = jnp.mean(x * x, axis=-1, keepdims=True)
        xn_ref[...] = (x * lax.rsqrt(ms + NORM_EPS) * g_ref[...]).astype(BF16)

    o_ref[...] = _dot(xn_ref[...], w_ref[...])


def norm_proj(x, g, w, tm, tn):
    m, d = x.shape
    n = w.shape[1]
    assert m % tm == 0
    return pl.pallas_call(
        _norm_proj_kernel,
        out_shape=jax.ShapeDtypeStruct((m, n), F32),
        grid=(m // tm, pl.cdiv(n, tn)),
        in_specs=[pl.BlockSpec((tm, d), lambda i, j: (i, 0)),
                  pl.BlockSpec((1, d), lambda i, j: (0, 0)),
                  pl.BlockSpec((d, tn), lambda i, j: (0, j))],
        out_specs=pl.BlockSpec((tm, tn), lambda i, j: (i, j)),
        scratch_shapes=[pltpu.VMEM((tm, d), BF16)],
        compiler_params=_cparams(("arbitrary", "arbitrary")),
        name="norm_proj",
    )(x, g.reshape(1, d), w)


def _out_proj_kernel(ya_ref, yb_ref, wa_ref, wb_ref, r_ref, o_ref):
    o_ref[...] = r_ref[...] + (_dot(ya_ref[...], wa_ref[...]) + _dot(yb_ref[...], wb_ref[...]))


def out_proj(ya, yb, wa, wb, resid, tm, tn):
    m, ka = ya.shape
    kb = yb.shape[1]
    n = wa.shape[1]
    assert m % tm == 0 and n % tn == 0
    return pl.pallas_call(
        _out_proj_kernel,
        out_shape=jax.ShapeDtypeStruct((m, n), F32),
        grid=(m // tm, n // tn),
        in_specs=[pl.BlockSpec((tm, ka), lambda i, j: (i, 0)),
                  pl.BlockSpec((tm, kb), lambda i, j: (i, 0)),
                  pl.BlockSpec((ka, tn), lambda i, j: (0, j)),
                  pl.BlockSpec((kb, tn), lambda i, j: (0, j)),
                  pl.BlockSpec((tm, tn), lambda i, j: (i, j))],
        out_specs=pl.BlockSpec((tm, tn), lambda i, j: (i, j)),
        compiler_params=_cparams(("arbitrary", "arbitrary")),
        name="out_proj",
    )(ya, yb, wa, wb, resid)


def _final_norm_kernel(x_ref, g_ref, o_ref):
    x = x_ref[...]
    ms = jnp.mean(x * x, axis=-1, keepdims=True)
    o_ref[...] = x * lax.rsqrt(ms + NORM_EPS) * g_ref[...]


def final_norm(x, g, tm):
    m, d = x.shape
    return pl.pallas_call(
        _final_norm_kernel,
        out_shape=jax.ShapeDtypeStruct((m, d), F32),
        grid=(m // tm,),
        in_specs=[pl.BlockSpec((tm, d), lambda i: (i, 0)), pl.BlockSpec((1, d), lambda i: (0, 0))],
        out_specs=pl.BlockSpec((tm, d), lambda i: (i, 0)),
        compiler_params=_cparams(("arbitrary",)),
        name="final_norm",
    )(x, g.reshape(1, d))


def _lru_gates(xs, wr, br, wi, bi, sp):
    xb = xs.astype(BF16)
    r = jax.nn.sigmoid(_dot(xb, wr) + br)
    ig = jax.nn.sigmoid(_dot(xb, wi) + bi)
    log_a = -LRU_C * r * sp
    a = jnp.exp(log_a)
    u = jnp.sqrt(-jnp.tanh(log_a) * (a * a + 1.0)) * (ig * xs)
    return a, u


def _softplus(z):
    return jnp.maximum(z, 0.0) + jnp.log1p(jnp.exp(-jnp.abs(z)))


def _lru_prompt_kernel(xa_ref, ga_ref, cw_ref, cb_ref, wr_ref, br_ref, wi_ref, bi_ref, lam_ref,
                       ya_ref, hl_ref, h_sc, halo_sc, *, tt, nblk):
    i = pl.program_id(1)

    @pl.when(i == 0)
    def _():
        h_sc[...] = jnp.zeros_like(h_sc)
        halo_sc[...] = jnp.zeros_like(halo_sc)

    x = xa_ref[...]
    xe = jnp.concatenate([halo_sc[...], x], axis=0)
    halo_sc[...] = x[tt - SUBLANES:, :]
    cw = cw_ref[...]
    y = cb_ref[...] + xe[SUBLANES:, :] * cw[CONV_W - 1:CONV_W, :]
    for k in range(CONV_W - 1):
        y = y + pltpu.roll(xe, CONV_W - 1 - k, 0)[SUBLANES:, :] * cw[k:k + 1, :]
    sp = _softplus(-lam_ref[...])
    rid = lax.broadcasted_iota(jnp.int32, (tt, LANES), 0) & (SUBLANES - 1)
    for n in range(nblk):
        sl = slice(n * LANES, (n + 1) * LANES)
        a, u = _lru_gates(y[:, sl], wr_ref[n], br_ref[:, sl], wi_ref[n], bi_ref[:, sl], sp[:, sl])
        for s in (1, 2, 4):
            keep = rid >= s
            u = jnp.where(keep, a * pltpu.roll(u, s, 0) + u, u)
            a = jnp.where(keep, a * pltpu.roll(a, s, 0), a)
        h = h_sc[:, sl]
        outs = []
        for g in range(tt // SUBLANES):
            rows = slice(g * SUBLANES, (g + 1) * SUBLANES)
            hg = u[rows, :] + a[rows, :] * h
            outs.append(hg)
            h = hg[SUBLANES - 1:, :]
        h_sc[:, sl] = h
        hs = jnp.concatenate(outs, axis=0)
        ya_ref[:, sl] = (hs * _silu(ga_ref[:, sl])).astype(BF16)
    hl_ref[0] = h_sc[...]


def lru_prompt(z, bsz, t, d_a, cw, cb, wr, br, wi, bi, lam, tt=256):
    nt = t // tt
    nblk = d_a // LANES
    row = lambda b, i: (b * nt + i, 0)
    vec = pl.BlockSpec((1, d_a), lambda b, i: (0, 0))
    wspec = pl.BlockSpec((nblk, LANES, LANES), lambda b, i: (0, 0, 0))
    return pl.pallas_call(
        functools.partial(_lru_prompt_kernel, tt=tt, nblk=nblk),
        out_shape=(jax.ShapeDtypeStruct((bsz * t, d_a), BF16), jax.ShapeDtypeStruct((bsz, 1, d_a), F32)),
        grid=(bsz, nt),
        in_specs=[pl.BlockSpec((tt, d_a), row),
                  pl.BlockSpec((tt, d_a), lambda b, i: (b * nt + i, 1)),
                  pl.BlockSpec((CONV_W, d_a), lambda b, i: (0, 0)), vec, wspec, vec, wspec, vec, vec],
        out_specs=(pl.BlockSpec((tt, d_a), row), pl.BlockSpec((1, 1, d_a), lambda b, i: (b, 0, 0))),
        scratch_shapes=[pltpu.VMEM((1, d_a), F32), pltpu.VMEM((SUBLANES, d_a), F32)],
        compiler_params=_cparams(("arbitrary", "arbitrary")),
        name="lru_prompt",
    )(z, z, cw, cb.reshape(1, d_a), wr, br.reshape(1, d_a), wi, bi.reshape(1, d_a), lam.reshape(1, d_a))


def _lru_sample_kernel(x_ref, ga_ref, c0_ref, h0_ref, cw_ref, cb_ref, wr_ref, br_ref, wi_ref, bi_ref, lam_ref,
                       ya_ref, hl_ref, *, ts, nblk):
    cw = cw_ref[...]
    sp = _softplus(-lam_ref[...])
    xe = [c0_ref[k] for k in range(CONV_W - 1)] + [x_ref[t] for t in range(ts)]
    h = h0_ref[...]
    for t in range(ts):
        y = cb_ref[...]
        for k in range(CONV_W):
            y = y + xe[t + k] * cw[k:k + 1, :]
        a_l, u_l = [], []
        for n in range(nblk):
            sl = slice(n * LANES, (n + 1) * LANES)
            a, u = _lru_gates(y[:, sl], wr_ref[n], br_ref[:, sl], wi_ref[n], bi_ref[:, sl], sp[:, sl])
            a_l.append(a)
            u_l.append(u)
        h = jnp.concatenate(a_l, axis=1) * h + jnp.concatenate(u_l, axis=1)
        ya_ref[t] = (h * _silu(ga_ref[t])).astype(BF16)
    for t in range(ts, SROWS):
        ya_ref[t] = jnp.zeros(ya_ref.shape[1:], BF16)
    hl_ref[...] = h


def lru_sample(xa_t, ga_t, conv_t, h0, cw, cb, wr, br, wi, bi, lam, ts):
    _, bsz, d_a = xa_t.shape
    nblk = d_a // LANES
    return pl.pallas_call(
        functools.partial(_lru_sample_kernel, ts=ts, nblk=nblk),
        out_shape=(jax.ShapeDtypeStruct((SROWS, bsz, d_a), BF16), jax.ShapeDtypeStruct((bsz, d_a), F32)),
        compiler_params=pltpu.CompilerParams(vmem_limit_bytes=VMEM_LIMIT),
        name="lru_sample",
    )(xa_t, ga_t, conv_t, h0, cw, cb.reshape(1, d_a), wr, br.reshape(1, d_a), wi, bi.reshape(1, d_a),
      lam.reshape(1, d_a))


def _swa_prompt_kernel(sink_ref, q_ref, kp_ref, kc_ref, vp_ref, vc_ref, gb_ref, cq_ref, sq_ref, cp_ref, sp_ref,
                       yb_ref, kout_ref, *, nq, gq, tq):
    kv = pl.program_id(1)
    i = pl.program_id(2)
    cq = cq_ref[...]
    sq = sq_ref[...]
    k_cur = _rope(kc_ref[...], cq, sq)
    k_prev = _rope(kp_ref[...], cp_ref[...], sp_ref[...])

    @pl.when(i == nq - 1)
    def _():
        kout_ref[0, 0] = k_cur

    k = jnp.concatenate([k_prev, k_cur], axis=0).astype(BF16)
    v = jnp.concatenate([vp_ref[...], vc_ref[...]], axis=0).astype(BF16)
    r = lax.broadcasted_iota(jnp.int32, (tq, 2 * tq), 0)
    c = lax.broadcasted_iota(jnp.int32, (tq, 2 * tq), 1)
    valid = (c > r) & (c <= r + WIN_B) & ((c >= tq) | (i > 0))
    for g in range(gq):
        sl = slice(g * HEAD_DIM, (g + 1) * HEAD_DIM)
        qg = _rope(q_ref[:, sl], cq, sq).astype(BF16)
        s = _dot_nt(qg, k) * ATTN_SCALE
        sink = sink_ref[kv * gq + g]
        m = jnp.maximum(jnp.max(jnp.where(valid, s, NEG), axis=1, keepdims=True), sink)
        p = jnp.where(valid, jnp.exp(s - m), 0.0)
        den = jnp.sum(p, axis=1, keepdims=True) + jnp.exp(sink - m)
        p = p * (1.0 / jnp.maximum(den, DENOM_FLOOR))
        o = _dot(p.astype(BF16), v)
        yb_ref[:, sl] = (o * _silu(gb_ref[:, sl])).astype(BF16)


def swa_prompt(z, sinks, ctab, stab, bsz, t, q_off, k_off, v_off, g_off, n_heads):
    tq = WIN_B
    nq = t // tq
    gq = n_heads // KVH_B
    wq = gq * HEAD_DIM
    assert q_off % wq == 0 and g_off % wq == 0 and k_off % HEAD_DIM == 0 and v_off % HEAD_DIM == 0
    prev = lambda i: jnp.maximum(i - 1, 0)
    in_specs = [
        pl.BlockSpec(memory_space=pltpu.SMEM),
        pl.BlockSpec((tq, wq), lambda b, h, i: (b * nq + i, q_off // wq + h)),
        pl.BlockSpec((tq, HEAD_DIM), lambda b, h, i: (b * nq + prev(i), k_off // HEAD_DIM + h)),
        pl.BlockSpec((tq, HEAD_DIM), lambda b, h, i: (b * nq + i, k_off // HEAD_DIM + h)),
        pl.BlockSpec((tq, HEAD_DIM), lambda b, h, i: (b * nq + prev(i), v_off // HEAD_DIM + h)),
        pl.BlockSpec((tq, HEAD_DIM), lambda b, h, i: (b * nq + i, v_off // HEAD_DIM + h)),
        pl.BlockSpec((tq, wq), lambda b, h, i: (b * nq + i, g_off // wq + h)),
        pl.BlockSpec((tq, LANES), lambda b, h, i: (i, 0)),
        pl.BlockSpec((tq, LANES), lambda b, h, i: (i, 0)),
        pl.BlockSpec((tq, LANES), lambda b, h, i: (prev(i), 0)),
        pl.BlockSpec((tq, LANES), lambda b, h, i: (prev(i), 0)),
    ]
    return pl.pallas_call(
        functools.partial(_swa_prompt_kernel, nq=nq, gq=gq, tq=tq),
        out_shape=(jax.ShapeDtypeStruct((bsz * t, n_heads * HEAD_DIM), BF16),
                   jax.ShapeDtypeStruct((bsz, KVH_B, tq, HEAD_DIM), F32)),
        grid=(bsz, KVH_B, nq),
        in_specs=in_specs,
        out_specs=(pl.BlockSpec((tq, wq), lambda b, h, i: (b * nq + i, h)),
                   pl.BlockSpec((1, 1, tq, HEAD_DIM), lambda b, h, i: (b, h, 0, 0))),
        compiler_params=_cparams(("arbitrary", "arbitrary", "arbitrary")),
        name="swa_prompt",
    )(sinks, z, z, z, z, z, z, ctab, stab, ctab, stab)


def _swa_sample_kernel(sink_ref, z_ref, kc_ref, vc_ref, c_ref, s_ref, yb_ref, kn_ref, *,
                       gq, q_off, k_off, v_off, g_off, ts, wlen, bb):
    c = c_ref[...]
    s = s_ref[...]
    qi = lax.broadcasted_iota(jnp.int32, (SROWS, wlen + SROWS), 0)
    ci = lax.broadcasted_iota(jnp.int32, (SROWS, wlen + SROWS), 1)
    diff = jnp.where(ci < wlen, qi + wlen - ci, qi - (ci - wlen))
    valid = (diff >= 0) & (diff < WIN_B)
    valid_g = jnp.concatenate([valid] * gq, axis=0)
    for bi in range(bb):
        for h in range(KVH_B):
            ks = slice(k_off + h * HEAD_DIM, k_off + (h + 1) * HEAD_DIM)
            vs = slice(v_off + h * HEAD_DIM, v_off + (h + 1) * HEAD_DIM)
            hs = slice(h * HEAD_DIM, (h + 1) * HEAD_DIM)
            k_new = _rope(z_ref[bi, :, ks], c, s)
            kn_ref[bi, :, hs] = k_new
            k = jnp.concatenate([kc_ref[bi, pl.ds(h, wlen, stride=KVH_B), :], k_new], axis=0).astype(BF16)
            v = jnp.concatenate([vc_ref[bi, pl.ds(h, wlen, stride=KVH_B), :], z_ref[bi, :, vs]], axis=0).astype(BF16)
            qs = jnp.concatenate(
                [_rope(z_ref[bi, :, q_off + (h * gq + g) * HEAD_DIM:q_off + (h * gq + g + 1) * HEAD_DIM], c, s)
                 for g in range(gq)], axis=0).astype(BF16)
            sink = jnp.concatenate([jnp.full((SROWS, 1), sink_ref[h * gq + g], F32) for g in range(gq)], axis=0)
            sc = _dot_nt(qs, k) * ATTN_SCALE
            m = jnp.maximum(jnp.max(jnp.where(valid_g, sc, NEG), axis=1, keepdims=True), sink)
            p = jnp.where(valid_g, jnp.exp(sc - m), 0.0)
            den = jnp.sum(p, axis=1, keepdims=True) + jnp.exp(sink - m)
            p = p * (1.0 / jnp.maximum(den, DENOM_FLOOR))
            o = _dot(p.astype(BF16), v)
            for g in range(gq):
                hd = h * gq + g
                gate = _silu(z_ref[bi, :, g_off + hd * HEAD_DIM:g_off + (hd + 1) * HEAD_DIM])
                yb_ref[bi, :, hd * HEAD_DIM:(hd + 1) * HEAD_DIM] = (o[g * SROWS:(g + 1) * SROWS, :] * gate).astype(BF16)


def swa_sample(z3, sinks, cache_k, cache_v, layer, ctab, stab, q_off, k_off, v_off, g_off, n_heads, ts, bb=4):
    bsz, _, p = z3.shape
    wlen = cache_k.shape[1] // KVH_B
    kvw = KVH_B * HEAD_DIM
    assert bsz % bb == 0
    return pl.pallas_call(
        functools.partial(_swa_sample_kernel, gq=n_heads // KVH_B, q_off=q_off, k_off=k_off, v_off=v_off,
                          g_off=g_off, ts=ts, wlen=wlen, bb=bb),
        out_shape=(jax.ShapeDtypeStruct((bsz, SROWS, n_heads * HEAD_DIM), BF16),
                   jax.ShapeDtypeStruct((bsz, SROWS, kvw), F32)),
        grid=(bsz // bb,),
        in_specs=[pl.BlockSpec(memory_space=pltpu.SMEM),
                  pl.BlockSpec((bb, SROWS, p), lambda b: (b, 0, 0)),
                  pl.BlockSpec((bb, wlen * KVH_B, HEAD_DIM), lambda b: (layer * (bsz // bb) + b, 0, 0)),
                  pl.BlockSpec((bb, wlen * KVH_B, HEAD_DIM), lambda b: (layer * (bsz // bb) + b, 0, 0)),
                  pl.BlockSpec((SROWS, LANES), lambda b: (0, 0)),
                  pl.BlockSpec((SROWS, LANES), lambda b: (0, 0))],
        out_specs=(pl.BlockSpec((bb, SROWS, n_heads * HEAD_DIM), lambda b: (b, 0, 0)),
                   pl.BlockSpec((bb, SROWS, kvw), lambda b: (b, 0, 0))),
        compiler_params=_cparams(("arbitrary",)),
        name="swa_sample",
    )(sinks, z3, cache_k, cache_v, ctab, stab)


def _pool_prompt_kernel(x_ref, gc_ref, w_ref, sc_ref, y_ref, halo_sc, *, tt, ngrp):
    i = pl.program_id(1)

    @pl.when(i == 0)
    def _():
        halo_sc[...] = jnp.zeros_like(halo_sc)

    x = x_ref[...]
    xe = jnp.concatenate([halo_sc[...], x], axis=0)
    halo_sc[...] = x[tt - POOL_MAX:, :]
    pos1 = (i * tt + 1 + lax.broadcasted_iota(jnp.int32, (tt, LANES), 0)).astype(F32)
    for g in range(ngrp):
        sl = slice(g * LANES, (g + 1) * LANES)
        w = POOL_WINDOWS[g]
        s = xe[:, sl]
        step = 1
        while step < w:
            s = s + pltpu.roll(s, step, 0)
            step *= 2
        pooled = s[POOL_MAX:, :] / jnp.minimum(float(w), pos1) - x[:, sl]
        y = _dot(pooled.astype(BF16), w_ref[g]) * sc_ref[:, sl]
        y_ref[:, sl] = (y * _silu(gc_ref[:, sl])).astype(BF16)


def pool_prompt(z, bsz, t, d_c, x_off, g_off, w_pool, scale, tt=256):
    nt = t // tt
    ngrp = d_c // LANES
    assert x_off % d_c == 0 and g_off % d_c == 0
    return pl.pallas_call(
        functools.partial(_pool_prompt_kernel, tt=tt, ngrp=ngrp),
        out_shape=jax.ShapeDtypeStruct((bsz * t, d_c), BF16),
        grid=(bsz, nt),
        in_specs=[pl.BlockSpec((tt, d_c), lambda b, i: (b * nt + i, x_off // d_c)),
                  pl.BlockSpec((tt, d_c), lambda b, i: (b * nt + i, g_off // d_c)),
                  pl.BlockSpec((ngrp, LANES, LANES), lambda b, i: (0, 0, 0)),
                  pl.BlockSpec((1, d_c), lambda b, i: (0, 0))],
        out_specs=pl.BlockSpec((tt, d_c), lambda b, i: (b * nt + i, 0)),
        scratch_shapes=[pltpu.VMEM((POOL_MAX, d_c), F32)],
        compiler_params=_cparams(("arbitrary", "arbitrary")),
        name="pool_prompt",
    )(z, z, w_pool, scale.reshape(1, d_c))


def _pool_sample_kernel(x_ref, gc_ref, buf_ref, w_ref, sc_ref, y_ref, *, ts, ngrp, start_pos):
    nbuf = POOL_MAX - 1
    xe = [buf_ref[k] for k in range(nbuf)] + [x_ref[t] for t in range(ts)]
    for t in range(ts):
        cols = []
        for g in range(ngrp):
            sl = slice(g * LANES, (g + 1) * LANES)
            w = POOL_WINDOWS[g]
            s = xe[nbuf + t][:, sl]
            for k in range(1, w):
                s = s + xe[nbuf + t - k][:, sl]
            pooled = s / min(float(w), float(start_pos + t + 1)) - xe[nbuf + t][:, sl]
            cols.append(_dot(pooled.astype(BF16), w_ref[g]))
        y = jnp.concatenate(cols, axis=1) * sc_ref[...]
        y_ref[t] = (y * _silu(gc_ref[t])).astype(BF16)
    for t in range(ts, SROWS):
        y_ref[t] = jnp.zeros(y_ref.shape[1:], BF16)


def pool_sample(xc_t, gc_t, buf_t, w_pool, scale, ts, start_pos):
    _, bsz, d_c = xc_t.shape
    return pl.pallas_call(
        functools.partial(_pool_sample_kernel, ts=ts, ngrp=d_c // LANES, start_pos=start_pos),
        out_shape=jax.ShapeDtypeStruct((SROWS, bsz, d_c), BF16),
        compiler_params=pltpu.CompilerParams(vmem_limit_bytes=VMEM_LIMIT),
        name="pool_sample",
    )(xc_t, gc_t, buf_t, w_pool, scale.reshape(1, d_c))


def _nsa_prep_kernel(q0_ref, q1_ref, q2_ref, kc_ref, vc_ref, ks_ref, vs_ref, kw_ref, vw_ref, c_ref, s_ref,
                     cwk_ref, cwv_ref, qr_ref, kcr_ref, vcf_ref, ksr_ref, vsf_ref, kwr_ref, vwf_ref, kvb_ref,
                     kcmp_ref, vcmp_ref, *, tt, nh):
    c = c_ref[...]
    s = s_ref[...]
    hpb = nh // 3
    for h in range(nh):
        q_ref = (q0_ref, q1_ref, q2_ref)[h // hpb]
        src = slice((h % hpb) * HEAD_DIM, (h % hpb + 1) * HEAD_DIM)
        qr_ref[:, h * HEAD_DIM:(h + 1) * HEAD_DIM] = _rope(q_ref[:, src], c, s).astype(BF16)
    nblk = tt // CMP_BLOCK
    kvw = KVH_D * HEAD_DIM
    for h in range(KVH_D):
        sl = slice(h * HEAD_DIM, (h + 1) * HEAD_DIM)
        kcr = _rope(kc_ref[:, sl], c, s)
        ksr = _rope(ks_ref[:, sl], c, s)
        kwr = _rope(kw_ref[:, sl], c, s)
        for out_ref, val in ((kcr_ref, kcr), (vcf_ref, vc_ref[:, sl]), (ksr_ref, ksr), (vsf_ref, vs_ref[:, sl]),
                             (kwr_ref, kwr), (vwf_ref, vw_ref[:, sl])):
            out_ref[pl.ds(h, tt, stride=KVH_D), :] = val
        for seg, val in enumerate((ksr, vs_ref[:, sl], kwr, vw_ref[:, sl])):
            kvb_ref[:, seg * kvw + h * HEAD_DIM:seg * kvw + (h + 1) * HEAD_DIM] = val.astype(BF16)
        kcmp_ref[0, :, sl] = jnp.sum(kcr.reshape(nblk, CMP_BLOCK, HEAD_DIM) * cwk_ref[:, sl][None], axis=1)
        vcmp_ref[0, :, sl] = jnp.sum(vc_ref[:, sl].reshape(nblk, CMP_BLOCK, HEAD_DIM) * cwv_ref[:, sl][None], axis=1)


def nsa_prep(z, ctab, stab, cwk2, cwv2, bsz, t, offs, n_heads, tt=512):
    nt = t // tt
    qw = n_heads * HEAD_DIM
    kvw = KVH_D * HEAD_DIM
    qb = qw // 3
    assert n_heads % 3 == 0 and offs["q"] % qb == 0
    assert all(offs[k] % kvw == 0 for k in ("kc", "vc", "ks", "vs", "kw", "vw"))
    qspec = lambda k: pl.BlockSpec((tt, qb), lambda b, i: (b * nt + i, offs["q"] // qb + k))
    row = lambda b, i: (b * nt + i, 0)
    kvspec = lambda name: pl.BlockSpec((tt, kvw), lambda b, i: (b * nt + i, offs[name] // kvw))
    return pl.pallas_call(
        functools.partial(_nsa_prep_kernel, tt=tt, nh=n_heads),
        out_shape=(jax.ShapeDtypeStruct((bsz * t, qw), BF16),)
                  + (jax.ShapeDtypeStruct((bsz * t * KVH_D, HEAD_DIM), F32),) * 6
                  + (jax.ShapeDtypeStruct((bsz * t, 4 * kvw), BF16),
                   jax.ShapeDtypeStruct((bsz, t // CMP_BLOCK, kvw), F32),
                   jax.ShapeDtypeStruct((bsz, t // CMP_BLOCK, kvw), F32)),
        grid=(bsz, nt),
        in_specs=[qspec(0), qspec(1), qspec(2),
                  kvspec("kc"), kvspec("vc"), kvspec("ks"), kvspec("vs"), kvspec("kw"), kvspec("vw"),
                  pl.BlockSpec((tt, LANES), lambda b, i: (i, 0)),
                  pl.BlockSpec((tt, LANES), lambda b, i: (i, 0)),
                  pl.BlockSpec((CMP_BLOCK, kvw), lambda b, i: (0, 0)),
                  pl.BlockSpec((CMP_BLOCK, kvw), lambda b, i: (0, 0))],
        out_specs=(pl.BlockSpec((tt, qw), row),) + (pl.BlockSpec((tt * KVH_D, HEAD_DIM), row),) * 6
                  + (pl.BlockSpec((tt, 4 * kvw), row),
                   pl.BlockSpec((1, tt // CMP_BLOCK, kvw), lambda b, i: (b, i, 0)),
                   pl.BlockSpec((1, tt // CMP_BLOCK, kvw), lambda b, i: (b, i, 0))),
        compiler_params=_cparams(("arbitrary", "arbitrary")),
        name="nsa_prep",
    )(z, z, z, z, z, z, z, z, z, ctab, stab, cwk2, cwv2)


def _masked_softmax_rows(s, valid):
    m = jnp.max(jnp.where(valid, s, NEG), axis=1, keepdims=True)
    p = jnp.where(valid, jnp.exp(s - m), 0.0)
    den = jnp.sum(p, axis=1, keepdims=True)
    return p * (1.0 / jnp.maximum(den, DENOM_FLOOR))


def _nsa_prompt_kernel(q_ref, kcmp_ref, vcmp_ref, ks_ref, vs_ref, kw_ref, vw_ref, gd0_ref, gd1_ref, gd2_ref,
                       gl_ref, gbias_ref,
                       o_ref, s_sc, mrun_sc, m_sc, lrun_sc, acc_sc, score_sc, *, t_len, tq, tk, gq, n_heads):
    kv = pl.program_id(1)
    i = pl.program_id(2)
    nb = t_len // CMP_BLOCK
    q0 = i * tq
    bpt = tk // CMP_BLOCK

    rows = gq * tq
    qs = jnp.concatenate([q_ref[:, g * HEAD_DIM:(g + 1) * HEAD_DIM] for g in range(gq)], axis=0)
    kc = kcmp_ref[0].astype(BF16)
    vc = vcmp_ref[0].astype(BF16)

    blk_r = lax.broadcasted_iota(jnp.int32, (rows, nb), 1)
    qp_r = q0 + (lax.broadcasted_iota(jnp.int32, (rows, nb), 0) & (tq - 1))
    ok_r = (blk_r + 1) * CMP_BLOCK - 1 <= qp_r
    p_cmp = _masked_softmax_rows(_dot_nt(qs, kc) * ATTN_SCALE, ok_r)
    o_cmp = _dot(p_cmp.astype(BF16), vc)
    blk_a = lax.broadcasted_iota(jnp.int32, (nb, rows), 0)
    qp_a = q0 + (lax.broadcasted_iota(jnp.int32, (nb, rows), 1) & (tq - 1))
    ok_a = (blk_a + 1) * CMP_BLOCK - 1 <= qp_a
    st = _dot_nt(kc, qs) * ATTN_SCALE
    mt = jnp.max(jnp.where(ok_a, st, NEG), axis=0, keepdims=True)
    pt = jnp.where(ok_a, jnp.exp(st - mt), 0.0)
    dt = jnp.sum(pt, axis=0, keepdims=True)
    pt = pt * (1.0 / jnp.maximum(dt, DENOM_FLOOR))
    imp = pt[:, 0:tq]
    for g in range(1, gq):
        imp = imp + pt[:, g * tq:(g + 1) * tq]
    blk_c = lax.broadcasted_iota(jnp.int32, (nb, tq), 0)
    qp_c = q0 + lax.broadcasted_iota(jnp.int32, (nb, tq), 1)

    cur = jnp.right_shift(qp_c, CMP_SHIFT)
    forced = (blk_c == 0) | (blk_c == cur) | (blk_c == cur - 1)
    score = jnp.where(blk_c > cur, -1.0, jnp.where(forced, FORCE_SCORE, imp))
    score_sc[...] = score
    n_vis = jnp.minimum((q0 + tq - 1) // CMP_BLOCK + 1, nb)

    def rank_body(k, cnt):
        rk = score_sc[pl.ds(k, 1), :]
        tie = jnp.where(blk_c > k, 1.0, 0.0)
        return cnt + jnp.where(rk > score, 1.0, jnp.where(rk == score, tie, 0.0))

    cnt = lax.fori_loop(0, n_vis, rank_body, jnp.zeros((nb, tq), F32))
    sel_t = jnp.where((cnt < N_SEL) & (score >= 0.0), 1.0, 0.0).astype(BF16)
    eye = jnp.where(lax.broadcasted_iota(jnp.int32, (tq, tq), 0) == lax.broadcasted_iota(jnp.int32, (tq, tq), 1),
                    1.0, 0.0).astype(BF16)
    sel = _dot_nt(eye, sel_t).astype(BF16)

    n_kt = (q0 + tq + tk - 1) // tk
    qp_k = q0 + lax.broadcasted_iota(jnp.int32, (tq, tk), 0)
    col_k = lax.broadcasted_iota(jnp.int32, (tq, tk), 1)
    e_row = lax.broadcasted_iota(jnp.int32, (nb, tk), 0)
    e_col = jnp.right_shift(lax.broadcasted_iota(jnp.int32, (nb, tk), 1), CMP_SHIFT)
    nlt = tk // LANES
    mrun_sc[...] = jnp.full(mrun_sc.shape, NEG, F32)

    def score_pass(kt, carry):
        k0 = pl.multiple_of(kt * tk, tk)
        s = _dot_nt(qs, ks_ref[pl.ds(k0, tk), :])
        expand = jnp.where(e_row == kt * bpt + e_col, 1.0, 0.0).astype(BF16)
        ok = (_dot(sel, expand) > 0.5) & (k0 + col_k <= qp_k)
        bias = jnp.where(ok, 0.0, NEG)
        sb = s * ATTN_SCALE + jnp.concatenate([bias] * gq, axis=0)
        s_sc[kt] = sb
        mx = sb[:, 0:LANES]
        for c in range(1, nlt):
            mx = jnp.maximum(mx, sb[:, c * LANES:(c + 1) * LANES])
        mrun_sc[...] = jnp.maximum(mrun_sc[...], mx)
        return carry

    lax.fori_loop(0, n_kt, score_pass, 0)
    m_sc[...] = jnp.broadcast_to(jnp.max(mrun_sc[...], axis=1, keepdims=True), m_sc.shape)
    lrun_sc[...] = jnp.zeros(lrun_sc.shape, F32)
    acc_sc[...] = jnp.zeros(acc_sc.shape, F32)

    def value_pass(kt, carry):
        k0 = pl.multiple_of(kt * tk, tk)
        sb = s_sc[kt]
        mrep = m_sc[...]
        ps = [jnp.exp(sb[:, c * LANES:(c + 1) * LANES] - mrep) for c in range(nlt)]
        lsum = ps[0]
        for c in range(1, nlt):
            lsum = lsum + ps[c]
        lrun_sc[...] = lrun_sc[...] + lsum
        p = jnp.concatenate(ps, axis=1).astype(BF16)
        acc_sc[...] = acc_sc[...] + _dot(p, vs_ref[pl.ds(k0, tk), :])
        return carry

    lax.fori_loop(0, n_kt, value_pass, 0)
    o_sel = acc_sc[...] * (1.0 / jnp.sum(lrun_sc[...], axis=1, keepdims=True))

    nwb = (WIN_D + tq - 2) // tq + 1
    wlen = nwb * tq
    w0 = pl.multiple_of(jnp.maximum(i - (nwb - 1), 0) * tq, tq)
    dw = (q0 + lax.broadcasted_iota(jnp.int32, (tq, wlen), 0)) - (w0 + lax.broadcasted_iota(jnp.int32, (tq, wlen), 1))
    bias_w = jnp.where((dw >= 0) & (dw < WIN_D), 0.0, NEG)
    sw = _dot_nt(qs, kw_ref[pl.ds(w0, wlen), :]) * ATTN_SCALE + jnp.concatenate([bias_w] * gq, axis=0)
    pw = jnp.exp(sw - jnp.max(sw, axis=1, keepdims=True))
    lw = jnp.sum(pw, axis=1, keepdims=True)
    o_win = _dot(pw.astype(BF16), vw_ref[pl.ds(w0, wlen), :]) * (1.0 / lw)

    gate = jax.nn.sigmoid(gl_ref[...] + gbias_ref[...])
    hpb = gq // 3
    for g in range(gq):
        rs = slice(g * tq, (g + 1) * tq)
        gs = []
        for br in range(3):
            c0 = br * n_heads + g
            c1 = c0 + gq
            gs.append(jnp.where(kv == 0, gate[:, c0:c0 + 1], gate[:, c1:c1 + 1]))
        yd = gs[0] * o_cmp[rs, :] + gs[1] * o_sel[rs, :] + gs[2] * o_win[rs, :]
        gd_ref = (gd0_ref, gd1_ref, gd2_ref)[g // hpb]
        gd = gd_ref[:, (g % hpb) * HEAD_DIM:(g % hpb + 1) * HEAD_DIM]
        o_ref[:, g * HEAD_DIM:(g + 1) * HEAD_DIM] = (yd * _silu(gd)).astype(BF16)


def nsa_prompt(z, qr, kcmp, vcmp, kvb, gbias, bsz, t, offs, n_heads, tq=128, tk=512):
    nq = t // tq
    gq = n_heads // KVH_D
    wq = gq * HEAD_DIM
    nb = t // CMP_BLOCK
    gw = wq // 3
    assert gq % 3 == 0 and offs["gd"] % gw == 0 and offs["gl"] % LANES == 0 and t % tk == 0
    assert KVH_D == 2
    gdspec = lambda k: pl.BlockSpec((tq, gw), lambda b, h, i: (b * nq + i, offs["gd"] // gw + 3 * h + k))
    res = lambda seg: pl.BlockSpec((t, HEAD_DIM), lambda b, h, i: (b, seg * KVH_D + h))
    rows = gq * tq
    stat = pltpu.VMEM((rows, LANES), F32)
    return pl.pallas_call(
        functools.partial(_nsa_prompt_kernel, t_len=t, tq=tq, tk=tk, gq=gq, n_heads=n_heads),
        out_shape=jax.ShapeDtypeStruct((bsz * t, n_heads * HEAD_DIM), BF16),
        grid=(bsz, KVH_D, nq),
        in_specs=[pl.BlockSpec((tq, wq), lambda b, h, i: (b * nq + i, h)),
                  pl.BlockSpec((1, nb, HEAD_DIM), lambda b, h, i: (b, 0, h)),
                  pl.BlockSpec((1, nb, HEAD_DIM), lambda b, h, i: (b, 0, h)),
                  res(0), res(1), res(2), res(3),
                  gdspec(0), gdspec(1), gdspec(2),
                  pl.BlockSpec((tq, LANES), lambda b, h, i: (b * nq + i, offs["gl"] // LANES)),
                  pl.BlockSpec((1, LANES), lambda b, h, i: (0, 0))],
        out_specs=pl.BlockSpec((tq, wq), lambda b, h, i: (b * nq + i, h)),
        scratch_shapes=[pltpu.VMEM((t // tk, rows, tk), F32), stat, stat, stat, stat, pltpu.VMEM((nb, tq), F32)],
        compiler_params=_cparams(("arbitrary", "arbitrary", "arbitrary")),
        name="nsa_prompt",
    )(qr, kcmp, vcmp, kvb, kvb, kvb, kvb, z, z, z, z, gbias)


def _cmp_stream_kernel(pt_ref, kpool, vpool, cwk_ref, cwv_ref, kcmp_ref, vcmp_ref, kbuf, vbuf, sem, part_sc, *,
                       ppc, nch, nsteps, page_base, npages):
    b = pl.program_id(0)
    c = pl.program_id(1)
    step = b * nch + c
    slot = lax.rem(step, 2)

    def copies(bb, cc, sl):
        out = []
        for p in range(ppc):
            page = page_base + pt_ref[bb * npages + cc * ppc + p]
            out.append(pltpu.make_async_copy(kpool.at[page], kbuf.at[sl, p], sem.at[0, sl]))
            out.append(pltpu.make_async_copy(vpool.at[page], vbuf.at[sl, p], sem.at[1, sl]))
        return out

    @pl.when(step == 0)
    def _():
        for cp in copies(0, 0, 0):
            cp.start()

    @pl.when(step + 1 < nsteps)
    def _():
        nxt = step + 1
        for cp in copies(nxt // nch, lax.rem(nxt, nch), 1 - slot):
            cp.start()

    for cp in copies(b, c, slot):
        cp.wait()

    nblk = ppc * (PAGE_SIZE // CMP_BLOCK)
    vregs_per_blk = CMP_BLOCK * KVH_D // SUBLANES
    for buf, w_ref, out_ref in ((kbuf, cwk_ref, kcmp_ref), (vbuf, cwv_ref, vcmp_ref)):
        x = buf[slot] * w_ref[...][None]
        part_sc[...] = jnp.sum(x.reshape(nblk, vregs_per_blk, SUBLANES, LANES), axis=1).reshape(nblk * SUBLANES, LANES)
        for h in range(KVH_D):
            acc = part_sc[pl.ds(h, nblk, stride=SUBLANES), :]
            for k in range(1, SUBLANES // KVH_D):
                acc = acc + part_sc[pl.ds(k * KVH_D + h, nblk, stride=SUBLANES), :]
            out_ref[0, :, h * HEAD_DIM:(h + 1) * HEAD_DIM] = acc


def cmp_stream(pt_flat, kpool, vpool, cwk_page, cwv_page, bsz, npages, page_base, ppc=16):
    prow = kpool.shape[1]
    width = KVH_D * HEAD_DIM
    nch = npages // ppc
    bpp = PAGE_SIZE // CMP_BLOCK
    grid_spec = pltpu.PrefetchScalarGridSpec(
        num_scalar_prefetch=1,
        grid=(bsz, nch),
        in_specs=[pl.BlockSpec(memory_space=pl.ANY), pl.BlockSpec(memory_space=pl.ANY),
                  pl.BlockSpec((prow, LANES), lambda b, c, pt: (0, 0)),
                  pl.BlockSpec((prow, LANES), lambda b, c, pt: (0, 0))],
        out_specs=(pl.BlockSpec((1, ppc * bpp, width), lambda b, c, pt: (b, c, 0)),) * 2,
        scratch_shapes=[pltpu.VMEM((2, ppc, prow, LANES), F32), pltpu.VMEM((2, ppc, prow, LANES), F32),
                        pltpu.SemaphoreType.DMA((2, 2)), pltpu.VMEM((ppc * bpp * SUBLANES, LANES), F32)],
    )
    return pl.pallas_call(
        functools.partial(_cmp_stream_kernel, ppc=ppc, nch=nch, nsteps=bsz * nch, page_base=page_base,
                          npages=npages),
        out_shape=(jax.ShapeDtypeStruct((bsz, npages * bpp, width), F32),) * 2,
        grid_spec=grid_spec,
        compiler_params=_cparams(("arbitrary", "arbitrary")),
        name="cmp_stream",
    )(pt_flat, kpool, vpool, cwk_page, cwv_page)


def _nsa_sample_cmp_kernel(z_ref, kcp_ref, vcp_ref, c_ref, s_ref, cwk_ref, cwv_ref,
                           ocmp_ref, imp_ref, knew_ref, *, offs, gq, ts, n_past, past, bb):
    c = c_ref[...]
    s = s_ref[...]
    nbp = n_past + SUBLANES
    blk_r = lax.broadcasted_iota(jnp.int32, (gq * SROWS, nbp), 1)
    qp_r = past + (lax.broadcasted_iota(jnp.int32, (gq * SROWS, nbp), 0) & (SROWS - 1))
    ok_r = (blk_r + 1) * CMP_BLOCK - 1 <= qp_r
    blk_c = lax.broadcasted_iota(jnp.int32, (nbp, LANES), 0)
    qp_c = past + (lax.broadcasted_iota(jnp.int32, (nbp, LANES), 1) & (SROWS - 1))
    ok_c = (blk_c + 1) * CMP_BLOCK - 1 <= qp_c
    row8 = lax.broadcasted_iota(jnp.int32, (SROWS, HEAD_DIM), 0)
    is_new = row8 < ts
    qpad = jnp.zeros((LANES - gq * SROWS, HEAD_DIM), BF16)
    kvw = KVH_D * HEAD_DIM
    for bi in range(bb):
        for h in range(KVH_D):
            hs = slice(h * HEAD_DIM, (h + 1) * HEAD_DIM)
            kcr = _rope(z_ref[bi, :, offs["kc"] + h * HEAD_DIM:offs["kc"] + (h + 1) * HEAD_DIM], c, s)
            ksr = _rope(z_ref[bi, :, offs["ks"] + h * HEAD_DIM:offs["ks"] + (h + 1) * HEAD_DIM], c, s)
            kwr = _rope(z_ref[bi, :, offs["kw"] + h * HEAD_DIM:offs["kw"] + (h + 1) * HEAD_DIM], c, s)
            knew_ref[bi, :, h * HEAD_DIM:(h + 1) * HEAD_DIM] = kcr
            knew_ref[bi, :, kvw + h * HEAD_DIM:kvw + (h + 1) * HEAD_DIM] = ksr
            knew_ref[bi, :, 2 * kvw + h * HEAD_DIM:2 * kvw + (h + 1) * HEAD_DIM] = kwr
            vcn = z_ref[bi, :, offs["vc"] + h * HEAD_DIM:offs["vc"] + (h + 1) * HEAD_DIM]
            nk = jnp.sum(jnp.where(is_new, kcr * cwk_ref[0:SROWS, hs], 0.0), axis=0, keepdims=True)
            nv = jnp.sum(jnp.where(is_new, vcn * cwv_ref[0:SROWS, hs], 0.0), axis=0, keepdims=True)
            nk8 = jnp.where(row8 == 0, jnp.broadcast_to(nk, (SROWS, HEAD_DIM)), 0.0)
            nv8 = jnp.where(row8 == 0, jnp.broadcast_to(nv, (SROWS, HEAD_DIM)), 0.0)
            kall = jnp.concatenate([kcp_ref[bi, :, hs], nk8], axis=0).astype(BF16)
            vall = jnp.concatenate([vcp_ref[bi, :, hs], nv8], axis=0).astype(BF16)
            qs = jnp.concatenate(
                [_rope(z_ref[bi, :, offs["q"] + (h * gq + g) * HEAD_DIM:offs["q"] + (h * gq + g + 1) * HEAD_DIM], c, s)
                 for g in range(gq)], axis=0).astype(BF16)
            p = _masked_softmax_rows(_dot_nt(qs, kall) * ATTN_SCALE, ok_r)
            o = _dot(p.astype(BF16), vall)
            for g in range(gq):
                hd = h * gq + g
                ocmp_ref[bi, :, hd * HEAD_DIM:(hd + 1) * HEAD_DIM] = o[g * SROWS:(g + 1) * SROWS, :]
            st = _dot_nt(kall, jnp.concatenate([qs, qpad], axis=0)) * ATTN_SCALE
            mt = jnp.max(jnp.where(ok_c, st, NEG), axis=0, keepdims=True)
            pt = jnp.where(ok_c, jnp.exp(st - mt), 0.0)
            dt = jnp.sum(pt, axis=0, keepdims=True)
            pt = pt * (1.0 / jnp.maximum(dt, DENOM_FLOOR))
            imp = pt
            for g in range(1, gq):
                imp = imp + pltpu.roll(pt, LANES - g * SROWS, 1)
            imp_ref[bi, h] = imp


def nsa_sample_cmp(z3, kcp, vcp, ctab, stab, cwk2, cwv2, offs, n_heads, ts, past, bb=4):
    bsz, _, p = z3.shape
    n_past = kcp.shape[1]
    kvw = KVH_D * HEAD_DIM
    nbp = n_past + SUBLANES
    assert bsz % bb == 0
    return pl.pallas_call(
        functools.partial(_nsa_sample_cmp_kernel, offs=offs, gq=n_heads // KVH_D, ts=ts, n_past=n_past, past=past,
                          bb=bb),
        out_shape=(jax.ShapeDtypeStruct((bsz, SROWS, n_heads * HEAD_DIM), F32),
                   jax.ShapeDtypeStruct((bsz, KVH_D, nbp, LANES), F32),
                   jax.ShapeDtypeStruct((bsz, SROWS, 3 * kvw), F32)),
        grid=(bsz // bb,),
        in_specs=[pl.BlockSpec((bb, SROWS, p), lambda b: (b, 0, 0)),
                  pl.BlockSpec((bb, n_past, kvw), lambda b: (b, 0, 0)),
                  pl.BlockSpec((bb, n_past, kvw), lambda b: (b, 0, 0)),
                  pl.BlockSpec((SROWS, LANES), lambda b: (0, 0)),
                  pl.BlockSpec((SROWS, LANES), lambda b: (0, 0)),
                  pl.BlockSpec((CMP_BLOCK, kvw), lambda b: (0, 0)),
                  pl.BlockSpec((CMP_BLOCK, kvw), lambda b: (0, 0))],
        out_specs=(pl.BlockSpec((bb, SROWS, n_heads * HEAD_DIM), lambda b: (b, 0, 0)),
                   pl.BlockSpec((bb, KVH_D, nbp, LANES), lambda b: (b, 0, 0, 0)),
                   pl.BlockSpec((bb, SROWS, 3 * kvw), lambda b: (b, 0, 0))),
        compiler_params=_cparams(("arbitrary",)),
        name="nsa_sample_cmp",
    )(z3, kcp, vcp, ctab, stab, cwk2, cwv2)


def _select_sample_kernel(imp_ref, qpos_ref, idx_ref, score_sc, cnt_sc, *, nbp, nsel):
    shape = score_sc.shape
    blk = lax.broadcasted_iota(jnp.int32, shape, 0)
    cur = jnp.right_shift(qpos_ref[...], CMP_SHIFT)
    forced = (blk == 0) | (blk == cur) | (blk == cur - 1)
    score_sc[...] = jnp.where(blk > cur, -1.0, jnp.where(forced, FORCE_SCORE, imp_ref[...]))
    cnt_sc[...] = jnp.zeros(shape, F32)

    def body(k, carry):
        rk = score_sc[pl.ds(k, 1), :]
        sc = score_sc[...]
        tie = jnp.where(blk > k, 1.0, 0.0)
        cnt_sc[...] = cnt_sc[...] + jnp.where(rk > sc, 1.0, jnp.where(rk == sc, tie, 0.0))
        return carry

    lax.fori_loop(0, nbp, body, 0)
    cnt = cnt_sc[...]
    ok = score_sc[...] >= 0.0
    blk_f = blk.astype(F32)
    for r in range(nsel):
        picked = jnp.sum(jnp.where((cnt == float(r)) & ok, blk_f, 0.0), axis=0, keepdims=True)
        idx_ref[r:r + 1, :] = picked.astype(jnp.int32)


def select_sample(imp_t, qpos):
    nbp, nc = imp_t.shape
    return pl.pallas_call(
        functools.partial(_select_sample_kernel, nbp=nbp, nsel=N_SEL),
        out_shape=jax.ShapeDtypeStruct((N_SEL, nc), jnp.int32),
        scratch_shapes=[pltpu.VMEM((nbp, nc), F32), pltpu.VMEM((nbp, nc), F32)],
        compiler_params=pltpu.CompilerParams(vmem_limit_bytes=VMEM_LIMIT),
        name="select_sample",
    )(imp_t, qpos)


def _nsa_sample_sel_kernel(idx_ref, pt_ref, z_ref, ocmp_ref, knew_ref, wk_ref, wv_ref, c_ref, s_ref, gbias_ref,
                           kpool, vpool, y_ref, kbuf, vbuf, sem, *, offs, gq, n_heads, ts, n_past, past,
                           page_base, npages, wlen):
    b = pl.program_id(0)
    c = c_ref[...]
    s = s_ref[...]
    bpp = PAGE_SIZE // CMP_BLOCK
    kvw = KVH_D * HEAD_DIM
    blk_rows = CMP_BLOCK * KVH_D

    nslots = KVH_D * ts * N_SEL
    nseq = pl.num_programs(0)
    half = lax.rem(b, 2)

    def slot_copies(seq, hf, sidx):
        blk = idx_ref[seq * nslots + sidx]
        past_blk = jnp.minimum(blk, n_past - 1)
        page = page_base + pt_ref[seq * npages + lax.div(past_blk, bpp)]
        row0 = pl.multiple_of(lax.rem(past_blk, bpp) * blk_rows, blk_rows)
        return (pltpu.make_async_copy(kpool.at[page, pl.ds(row0, blk_rows), :], kbuf.at[hf, sidx], sem.at[0, hf]),
                pltpu.make_async_copy(vpool.at[page, pl.ds(row0, blk_rows), :], vbuf.at[hf, sidx], sem.at[1, hf]))

    def issue_all(seq, hf):
        def issue(sidx, carry):
            for cp in slot_copies(seq, hf, sidx):
                cp.start()
            return carry

        lax.fori_loop(0, nslots, issue, 0)

    @pl.when(b == 0)
    def _():
        issue_all(b, half)

    @pl.when(b + 1 < nseq)
    def _():
        issue_all(b + 1, 1 - half)

    row8 = lax.broadcasted_iota(jnp.int32, (SROWS, HEAD_DIM), 0)
    kpad = jnp.zeros((LANES - SROWS, HEAD_DIM), F32)
    qi = lax.broadcasted_iota(jnp.int32, (SROWS, wlen + LANES), 0)
    ci = lax.broadcasted_iota(jnp.int32, (SROWS, wlen + LANES), 1)
    dw = jnp.where(ci < wlen, qi + wlen - ci, qi - (ci - wlen))
    ok_w = (dw >= 0) & (dw < WIN_D) & ((ci < wlen) | (ci - wlen < ts))
    ok_wg = jnp.concatenate([ok_w] * gq, axis=0)
    qr = []
    o_win = []
    for h in range(KVH_D):
        hs = slice(h * HEAD_DIM, (h + 1) * HEAD_DIM)
        kwn = knew_ref[0, :, 2 * kvw + h * HEAD_DIM:2 * kvw + (h + 1) * HEAD_DIM]
        vwn = z_ref[0, :, offs["vw"] + h * HEAD_DIM:offs["vw"] + (h + 1) * HEAD_DIM]
        kw = jnp.concatenate([wk_ref[0, pl.ds(h, wlen, stride=KVH_D), :], kwn, kpad], axis=0).astype(BF16)
        vw = jnp.concatenate([wv_ref[0, pl.ds(h, wlen, stride=KVH_D), :], vwn, kpad], axis=0).astype(BF16)
        qh = [_rope(z_ref[0, :, offs["q"] + (h * gq + g) * HEAD_DIM:offs["q"] + (h * gq + g + 1) * HEAD_DIM], c, s)
              for g in range(gq)]
        qr.extend(qh)
        qs = jnp.concatenate(qh, axis=0).astype(BF16)
        pw = _masked_softmax_rows(_dot_nt(qs, kw) * ATTN_SCALE, ok_wg)
        ow = _dot(pw.astype(BF16), vw)
        o_win.extend([ow[g * SROWS:(g + 1) * SROWS, :] for g in range(gq)])

    def drain(sidx, carry):
        for cp in slot_copies(b, half, sidx):
            cp.wait()
        return carry

    lax.fori_loop(0, nslots, drain, 0)

    nkeys = N_SEL * CMP_BLOCK
    colk = lax.broadcasted_iota(jnp.int32, (SROWS, nkeys + LANES), 1)
    slot_of_col = jnp.right_shift(colk, CMP_SHIFT)
    in_blk = colk & (CMP_BLOCK - 1)
    o_sel = [jnp.zeros((SROWS, HEAD_DIM), F32) for _ in range(n_heads)]
    for h in range(KVH_D):
        ksn = knew_ref[0, :, kvw + h * HEAD_DIM:kvw + (h + 1) * HEAD_DIM]
        vsn = z_ref[0, :, offs["vs"] + h * HEAD_DIM:offs["vs"] + (h + 1) * HEAD_DIM]
        for t in range(ts):
            qt = jnp.zeros((SROWS, HEAD_DIM), F32)
            for g in range(gq):
                qt = jnp.where(row8 == g, jnp.broadcast_to(qr[h * gq + g][t:t + 1, :], (SROWS, HEAD_DIM)), qt)
            qt = qt.astype(BF16)
            s0 = (h * ts + t) * N_SEL
            kg = kbuf[half, pl.ds(s0, N_SEL), pl.ds(h, CMP_BLOCK, stride=KVH_D), :].reshape(nkeys, HEAD_DIM)
            vg = vbuf[half, pl.ds(s0, N_SEL), pl.ds(h, CMP_BLOCK, stride=KVH_D), :].reshape(nkeys, HEAD_DIM)
            kall = jnp.concatenate([kg, ksn, kpad], axis=0).astype(BF16)
            vall = jnp.concatenate([vg, vsn, kpad], axis=0).astype(BF16)
            far = past + ts + SROWS
            kpos = jnp.where(colk >= nkeys, past + (colk - nkeys), 0)
            kpos = jnp.where((colk >= nkeys) & (colk - nkeys >= ts), far, kpos)
            n_new = jnp.int32(0)
            for r in range(N_SEL):
                blk = idx_ref[((b * KVH_D + h) * ts + t) * N_SEL + r]
                n_new = n_new + jnp.where(blk >= n_past, 1, 0)
                base = jnp.where(blk < n_past, blk * CMP_BLOCK, far)
                kpos = jnp.where(slot_of_col == r, base + in_blk, kpos)
            kpos = jnp.where((colk >= nkeys) & (n_new == 0), far, kpos)
            ok = kpos <= past + t
            p = _masked_softmax_rows(_dot_nt(qt, kall) * ATTN_SCALE, ok)
            res = _dot(p.astype(BF16), vall)
            for g in range(gq):
                hd = h * gq + g
                o_sel[hd] = jnp.where(row8 == t, jnp.broadcast_to(res[g:g + 1, :], (SROWS, HEAD_DIM)), o_sel[hd])

    ngate = 3 * n_heads
    gate = jax.nn.sigmoid(z_ref[0, :, offs["gl"]:offs["gl"] + ngate] + gbias_ref[:, 0:ngate])
    for hd in range(n_heads):
        sl = slice(hd * HEAD_DIM, (hd + 1) * HEAD_DIM)
        yd = (gate[:, hd:hd + 1] * ocmp_ref[0, :, sl]
              + gate[:, n_heads + hd:n_heads + hd + 1] * o_sel[hd]
              + gate[:, 2 * n_heads + hd:2 * n_heads + hd + 1] * o_win[hd])
        y_ref[0, :, sl] = (yd * _silu(z_ref[0, :, offs["gd"] + hd * HEAD_DIM:offs["gd"] + (hd + 1) * HEAD_DIM])
                           ).astype(BF16)


def nsa_sample_sel(idx_flat, pt_flat, z3, ocmp, knew, wk, wv, ctab, stab, gbias, kpool, vpool, offs, n_heads,
                   ts, past, page_base, npages, layer, n_layers_b):
    bsz, _, p = z3.shape
    wlen = wk.shape[1] // KVH_D
    kvw = KVH_D * HEAD_DIM
    qw = n_heads * HEAD_DIM
    n_past = npages * (PAGE_SIZE // CMP_BLOCK)
    grid_spec = pltpu.PrefetchScalarGridSpec(
        num_scalar_prefetch=2,
        grid=(bsz,),
        in_specs=[pl.BlockSpec((1, SROWS, p), lambda b, i, t: (b, 0, 0)),
                  pl.BlockSpec((1, SROWS, qw), lambda b, i, t: (b, 0, 0)),
                  pl.BlockSpec((1, SROWS, 3 * kvw), lambda b, i, t: (b, 0, 0)),
                  pl.BlockSpec((1, wlen * KVH_D, HEAD_DIM), lambda b, i, t: (layer * n_layers_b + b, 0, 0)),
                  pl.BlockSpec((1, wlen * KVH_D, HEAD_DIM), lambda b, i, t: (layer * n_layers_b + b, 0, 0)),
                  pl.BlockSpec((SROWS, LANES), lambda b, i, t: (0, 0)),
                  pl.BlockSpec((SROWS, LANES), lambda b, i, t: (0, 0)),
                  pl.BlockSpec((1, LANES), lambda b, i, t: (0, 0)),
                  pl.BlockSpec(memory_space=pl.ANY), pl.BlockSpec(memory_space=pl.ANY)],
        out_specs=pl.BlockSpec((1, SROWS, qw), lambda b, i, t: (b, 0, 0)),
        scratch_shapes=[pltpu.VMEM((2, KVH_D * ts * N_SEL, CMP_BLOCK * KVH_D, HEAD_DIM), F32),
                        pltpu.VMEM((2, KVH_D * ts * N_SEL, CMP_BLOCK * KVH_D, HEAD_DIM), F32),
                        pltpu.SemaphoreType.DMA((2, 2))],
    )
    return pl.pallas_call(
        functools.partial(_nsa_sample_sel_kernel, offs=offs, gq=n_heads // KVH_D, n_heads=n_heads, ts=ts,
                          n_past=n_past, past=past, page_base=page_base, npages=npages, wlen=wlen),
        out_shape=jax.ShapeDtypeStruct((bsz, SROWS, qw), BF16),
        grid_spec=grid_spec,
        compiler_params=_cparams(("arbitrary",)),
        name="nsa_sample_sel",
    )(idx_flat, pt_flat, z3, ocmp, knew, wk, wv, ctab, stab, gbias, kpool, vpool)


EVEN_OFFS = dict(xa=0, ga=1024, q=2048, k=3072, v=3328, gb=3584)


def _odd_layout(d_c, d_d, kvw, n_gate):
    offs = {}
    pos = 0
    for name, width in (("xc", d_c), ("gc", d_c), ("q", d_d), ("kc", kvw), ("vc", kvw), ("ks", kvw), ("vs", kvw),
                        ("kw", kvw), ("vw", kvw), ("gd", d_d), ("gl", n_gate)):
        offs[name] = pos
        pos += width
    return offs, pos


def _pick_tile(n, prefs):
    for t in prefs:
        if n % t == 0:
            return t
    raise ValueError(n)


def kernel(x_prompt, x_sample, state_lru_h, state_lru_conv, cache_swa_k, cache_swa_v, state_pool, cache_nsa_cmp_k, cache_nsa_cmp_v, cache_nsa_sel_k, cache_nsa_sel_v, cache_nsa_win_k, cache_nsa_win_v, page_table, norm_g, final_g, w_in_even, conv_w, conv_b, w_rgate, b_rgate, w_igate, b_igate, lru_lambda, swa_sinks, w_out_even, w_in_odd, w_pool, pool_scale, cmp_wk, cmp_wv, nsa_gate_b, w_out_odd):
    bp, tp, d = x_prompt.shape
    bs, ts, _ = x_sample.shape
    depth = norm_g.shape[0]
    npages = page_table.shape[1]
    past = npages * PAGE_SIZE
    n_phys = cache_nsa_cmp_k.shape[1]
    d_a = state_lru_h.shape[-1]
    n_heads_b = swa_sinks.shape[1]
    d_b = n_heads_b * HEAD_DIM
    d_c = state_pool.shape[-1]
    n_heads_d = nsa_gate_b.shape[1] // 3
    d_d = n_heads_d * HEAD_DIM
    kvw_b = KVH_B * HEAD_DIM
    kvw_d = KVH_D * HEAD_DIM
    assert ts <= SROWS and past % CMP_BLOCK == 0 and (past + ts - 1) // CMP_BLOCK == past // CMP_BLOCK

    hp = x_prompt.reshape(bp * tp, d)
    hs = jnp.pad(x_sample, ((0, 0), (0, SROWS - ts), (0, 0))).reshape(bs * SROWS, d)
    ms = bs * SROWS
    ctab_p, stab_p = rope_tables(tp, 0)
    ctab_s, stab_s = rope_tables(SROWS, past)
    tm_p = _pick_tile(bp * tp, (1024, 512, 256, 128))
    pt_flat = page_table.reshape(-1)

    odd_offs, odd_total = _odd_layout(d_c, d_d, kvw_d, 3 * n_heads_d)
    ev_p, ev_s, od_p, od_s = [], [], [], []
    for i in range(depth):
        j = i // 2
        if i % 2 == 0:
            p_even = w_in_even.shape[2]
            w_in = w_in_even[j].astype(BF16)
            tn = _pick_tile(p_even, (512, 256, 128))
            zp = norm_proj(hp, norm_g[i], w_in, tm_p, tn)
            zs = norm_proj(hs, norm_g[i], w_in, ms, tn)
            wr = w_rgate[j].astype(BF16)
            wi = w_igate[j].astype(BF16)
            lru_args = (conv_w[j], conv_b[j], wr, b_rgate[j], wi, b_igate[j], lru_lambda[j])
            eo = EVEN_OFFS
            ya_p, hl_p = lru_prompt(zp, bp, tp, d_a, *lru_args)
            yb_p, kout_p = swa_prompt(zp, swa_sinks[j], ctab_p, stab_p, bp, tp, eo["q"], eo["k"], eo["v"], eo["gb"],
                                      n_heads_b)
            wo = w_out_even[j].astype(BF16)
            tn_o = _pick_tile(d, (512, 256, 128))
            hp = out_proj(ya_p, yb_p, wo[:d_a], wo[d_a:], hp, tm_p, tn_o)
            zp3 = zp.reshape(bp, tp, p_even)
            keep = min(WIN_B, tp)
            ev_p.append((hl_p[:, 0], zp3[:, tp - (CONV_W - 1):, eo["xa"]:eo["xa"] + d_a],
                         kout_p.transpose(0, 2, 1, 3)[:, WIN_B - keep:],
                         zp3[:, tp - keep:, eo["v"]:eo["v"] + kvw_b].reshape(bp, keep, KVH_B, HEAD_DIM)))
            zs3 = zs.reshape(bs, SROWS, p_even)
            zst = zs3.transpose(1, 0, 2)
            ya_t, hl_s = lru_sample(zst[:, :, eo["xa"]:eo["xa"] + d_a], zst[:, :, eo["ga"]:eo["ga"] + d_a],
                                    state_lru_conv[j].transpose(1, 0, 2), state_lru_h[j], *lru_args, ts)
            ya_s = ya_t.transpose(1, 0, 2).reshape(ms, d_a)
            rows3 = lambda a: a.reshape(a.shape[0] * a.shape[1], a.shape[2] * a.shape[3], a.shape[4])
            yb_s, knew = swa_sample(zs3, swa_sinks[j], rows3(cache_swa_k), rows3(cache_swa_v), j, ctab_s, stab_s,
                                    eo["q"], eo["k"], eo["v"], eo["gb"], n_heads_b, ts)
            hs = out_proj(ya_s, yb_s.reshape(ms, d_b), wo[:d_a], wo[d_a:], hs, ms, tn_o)
            wlen = cache_swa_k.shape[2]
            new_conv = jnp.concatenate([state_lru_conv[j], zs3[:, :ts, eo["xa"]:eo["xa"] + d_a]], axis=1)[:, -(CONV_W - 1):]
            new_k = jnp.concatenate([cache_swa_k[j], knew[:, :ts].reshape(bs, ts, KVH_B, HEAD_DIM)], axis=1)[:, -wlen:]
            new_v = jnp.concatenate([cache_swa_v[j], zs3[:, :ts, eo["v"]:eo["v"] + kvw_b].reshape(bs, ts, KVH_B, HEAD_DIM)],
                                    axis=1)[:, -wlen:]
            ev_s.append((hl_s, new_conv, new_k, new_v))
        else:
            oo = odd_offs
            w_in = w_in_odd[j].astype(BF16)
            tn = 512
            zp = norm_proj(hp, norm_g[i], w_in, tm_p, tn)
            zs = norm_proj(hs, norm_g[i], w_in, ms, tn)
            wp = w_pool[j].astype(BF16)
            cwk2 = jnp.repeat(cmp_wk[j], HEAD_DIM, axis=1)
            cwv2 = jnp.repeat(cmp_wv[j], HEAD_DIM, axis=1)
            gbias = jnp.pad(nsa_gate_b[j], (0, LANES - 3 * n_heads_d)).reshape(1, LANES)
            wo = w_out_odd[j].astype(BF16)
            tn_o = _pick_tile(d, (512, 256, 128))
            yc_p = pool_prompt(zp, bp, tp, d_c, oo["xc"], oo["gc"], wp, pool_scale[j])
            qr, kcr, vcf, ksr, vsf, kwr, vwf, kvb, kcmp, vcmp = nsa_prep(zp, ctab_p, stab_p, cwk2, cwv2, bp, tp, oo,
                                                                         n_heads_d)
            yd_p = nsa_prompt(zp, qr, kcmp, vcmp, kvb, gbias, bp, tp, oo, n_heads_d)
            hp = out_proj(yc_p, yd_p, wo[:d_c], wo[d_c:], hp, tm_p, tn_o)
            zp3 = zp.reshape(bp, tp, odd_total)
            kv4 = lambda a: a.reshape(bp, tp, KVH_D, HEAD_DIM)
            keep = min(WIN_D, tp)
            od_p.append((zp3[:, tp - (POOL_MAX - 1):, oo["xc"]:oo["xc"] + d_c],
                         kv4(kcr), kv4(vcf), kv4(ksr), kv4(vsf), kv4(kwr)[:, tp - keep:], kv4(vwf)[:, tp - keep:]))
            zs3 = zs.reshape(bs, SROWS, odd_total)
            zst = zs3.transpose(1, 0, 2)
            yc_t = pool_sample(zst[:, :, oo["xc"]:oo["xc"] + d_c], zst[:, :, oo["gc"]:oo["gc"] + d_c],
                               state_pool[j].transpose(1, 0, 2), wp, pool_scale[j], ts, past)
            yc_s = yc_t.transpose(1, 0, 2).reshape(ms, d_c)
            n_layers_odd = cache_nsa_cmp_k.shape[0]
            pool3 = lambda a: a.reshape(n_layers_odd * n_phys, PAGE_SIZE * KVH_D, HEAD_DIM)
            page_w = lambda w: jnp.broadcast_to(jnp.tile(w, (PAGE_SIZE // CMP_BLOCK, 1)).reshape(-1, 1),
                                                (PAGE_SIZE * KVH_D, LANES))
            cwk_page = page_w(cmp_wk[j])
            cwv_page = page_w(cmp_wv[j])
            kcp, vcp = cmp_stream(pt_flat, pool3(cache_nsa_cmp_k), pool3(cache_nsa_cmp_v), cwk_page, cwv_page,
                                  bs, npages, j * n_phys)
            ocmp, imp, knew = nsa_sample_cmp(zs3, kcp, vcp, ctab_s, stab_s, cwk2, cwv2, oo, n_heads_d, ts, past)
            nbp = imp.shape[2]
            assert nbp >= N_SEL
            imp_t = imp[:, :, :, :ts].transpose(2, 0, 1, 3).reshape(nbp, bs * KVH_D * ts)
            qpos = jnp.tile(past + jnp.arange(ts, dtype=jnp.int32), bs * KVH_D).reshape(1, -1)
            idx = select_sample(imp_t, qpos)
            idx_flat = idx.T.reshape(-1)
            wlen = cache_nsa_win_k.shape[2]
            win3 = lambda a: a.reshape(n_layers_odd * bs, wlen * KVH_D, HEAD_DIM)
            yd_s = nsa_sample_sel(idx_flat, pt_flat, zs3, ocmp, knew, win3(cache_nsa_win_k), win3(cache_nsa_win_v),
                                  ctab_s, stab_s, gbias, pool3(cache_nsa_sel_k), pool3(cache_nsa_sel_v), oo,
                                  n_heads_d, ts, past, j * n_phys, npages, j, bs)
            hs = out_proj(yc_s, yd_s.reshape(ms, d_d), wo[:d_c], wo[d_c:], hs, ms, tn_o)
            kv4s = lambda a: a[:, :ts].reshape(bs, ts, KVH_D, HEAD_DIM)
            new_pool = jnp.concatenate([state_pool[j], zs3[:, :ts, oo["xc"]:oo["xc"] + d_c]], axis=1)[:, -(POOL_MAX - 1):]
            kwn = kv4s(knew[:, :, 2 * kvw_d:3 * kvw_d])
            vwn = kv4s(zs3[:, :, oo["vw"]:oo["vw"] + kvw_d])
            od_s.append((new_pool,
                         kv4s(knew[:, :, 0:kvw_d]), kv4s(zs3[:, :, oo["vc"]:oo["vc"] + kvw_d]),
                         kv4s(knew[:, :, kvw_d:2 * kvw_d]), kv4s(zs3[:, :, oo["vs"]:oo["vs"] + kvw_d]),
                         jnp.concatenate([cache_nsa_win_k[j], kwn], axis=1)[:, -wlen:],
                         jnp.concatenate([cache_nsa_win_v[j], vwn], axis=1)[:, -wlen:]))

    y_prompt = final_norm(hp, final_g, tm_p).reshape(bp, tp, d)
    y_sample = final_norm(hs, final_g, ms).reshape(bs, SROWS, d)[:, :ts]

    def field(lst, k):
        return jnp.stack([st[k] for st in lst], axis=0)

    return (y_prompt, y_sample,
            field(ev_p, 0), field(ev_p, 1), field(ev_p, 2), field(ev_p, 3),
            field(od_p, 0), field(od_p, 1), field(od_p, 2), field(od_p, 3), field(od_p, 4), field(od_p, 5), field(od_p, 6),
            field(ev_s, 0), field(ev_s, 1), field(ev_s, 2), field(ev_s, 3),
            field(od_s, 0), field(od_s, 1), field(od_s, 2), field(od_s, 3), field(od_s, 4), field(od_s, 5), field(od_s, 6))
```

```python
import functools

import jax
import jax.numpy as jnp
import numpy as np
from jax import lax
from jax.experimental import pallas as pl
from jax.experimental.pallas import tpu as pltpu

F32 = jnp.float32
BF16 = jnp.bfloat16

LANES = 128
SUBLANES = 8
VMEM_LIMIT = 48 * 1024 * 1024
NSA_VMEM_LIMIT = 56 * 1024 * 1024

HEAD_DIM = 128
ATTN_SCALE = HEAD_DIM ** -0.5
ROPE_THETA = 10000.0
NORM_EPS = 1e-6
PAGE_SIZE = 128
CONV_W = 4
LRU_C = 8.0
KVH_B = 2
WIN_B = 128
POOL_WINDOWS = (2, 4, 8, 16)
POOL_MAX = 16
KVH_D = 2
CMP_BLOCK = 64
CMP_SHIFT = 6
N_SEL = 16
WIN_D = 512
FORCE_SCORE = 1e6
DENOM_FLOOR = 1e-30
NEG = -1e30
SROWS = 8


def _cparams(sem):
    return pltpu.CompilerParams(dimension_semantics=sem, vmem_limit_bytes=VMEM_LIMIT)


def _dot(a, b):
    return jnp.dot(a, b, preferred_element_type=F32)


def _dot_nt(a, b):
    return lax.dot_general(a, b, (((1,), (1,)), ((), ())), preferred_element_type=F32)


def _silu(x):
    return x * jax.nn.sigmoid(x)


def _rope(x, c, s):
    return x * c + pltpu.roll(x, HEAD_DIM // 2, 1) * s


def _rope_tab_kernel(inv_ref, sgn_ref, c_ref, s_ref, *, start, rows):
    i = pl.program_id(0)
    pos = start + i * rows + lax.broadcasted_iota(jnp.int32, (rows, LANES), 0)
    ang = pos.astype(F32) * inv_ref[...]
    c_ref[...] = jnp.cos(ang)
    s_ref[...] = jnp.sin(ang) * sgn_ref[...]


def rope_tables(n_rows, start):
    half = HEAD_DIM // 2
    inv = ROPE_THETA ** (-jnp.arange(half, dtype=F32) / half)
    inv2 = jnp.concatenate([inv, inv]).reshape(1, LANES)
    sgn = jnp.concatenate([-jnp.ones((half,), F32), jnp.ones((half,), F32)]).reshape(1, LANES)
    rows = min(n_rows, 512)
    assert n_rows % rows == 0
    return pl.pallas_call(
        functools.partial(_rope_tab_kernel, start=start, rows=rows),
        out_shape=(jax.ShapeDtypeStruct((n_rows, LANES), F32),) * 2,
        grid=(n_rows // rows,),
        in_specs=[pl.BlockSpec((1, LANES), lambda i: (0, 0))] * 2,
        out_specs=(pl.BlockSpec((rows, LANES), lambda i: (i, 0)),) * 2,
        compiler_params=_cparams(("arbitrary",)),
        name="rope_tables",
    )(inv2, sgn)


def _norm_proj_kernel(x_ref, g_ref, w_ref, o_ref, xn_ref):
    @pl.when(pl.program_id(1) == 0)
    def _():
        x = x_ref[...]
        ms = jnp.mean(x * x, axis=-1, keepdims=True)
        xn_ref[...] = (x * lax.rsqrt(ms + NORM_EPS) * g_ref[...]).astype(BF16)

    o_ref[...] = _dot(xn_ref[...], w_ref[...])


def norm_proj(x, g, w_all, layer, tm, tn):
    m, d = x.shape
    n = w_all.shape[2]
    assert m % tm == 0
    return pl.pallas_call(
        _norm_proj_kernel,
        out_shape=jax.ShapeDtypeStruct((m, n), F32),
        grid=(m // tm, pl.cdiv(n, tn)),
        in_specs=[pl.BlockSpec((tm, d), lambda i, j: (i, 0)),
                  pl.BlockSpec((1, d), lambda i, j: (0, 0)),
                  pl.BlockSpec((None, d, tn), lambda i, j: (layer, 0, j))],
        out_specs=pl.BlockSpec((tm, tn), lambda i, j: (i, j)),
        scratch_shapes=[pltpu.VMEM((tm, d), BF16)],
        compiler_params=_cparams(("arbitrary", "arbitrary")),
        name="norm_proj",
    )(x, g.reshape(1, d), w_all)


OUT_KC = 512


def _out_proj_kernel(ya_ref, yb_ref, *rest, ka, nchunk):
    w_refs, r_ref, o_ref = rest[:nchunk], rest[nchunk], rest[nchunk + 1]
    acc = r_ref[...]
    for k in range(nchunk):
        lo = k * OUT_KC
        y = ya_ref[:, lo:lo + OUT_KC] if lo < ka else yb_ref[:, lo - ka:lo - ka + OUT_KC]
        acc = acc + _dot(y, w_refs[k][...])
    o_ref[...] = acc


def out_proj(ya, yb, w_all, layer, resid, tm, tn):
    m, ka = ya.shape
    kb = yb.shape[1]
    n = w_all.shape[2]
    assert m % tm == 0 and n % tn == 0 and ka % OUT_KC == 0 and kb % OUT_KC == 0
    nchunk = (ka + kb) // OUT_KC
    wspec = lambda k: pl.BlockSpec((None, OUT_KC, tn), lambda i, j: (layer, k, j))
    return pl.pallas_call(
        functools.partial(_out_proj_kernel, ka=ka, nchunk=nchunk),
        out_shape=jax.ShapeDtypeStruct((m, n), F32),
        grid=(m // tm, n // tn),
        in_specs=[pl.BlockSpec((tm, ka), lambda i, j: (i, 0)),
                  pl.BlockSpec((tm, kb), lambda i, j: (i, 0))]
                 + [wspec(k) for k in range(nchunk)]
                 + [pl.BlockSpec((tm, tn), lambda i, j: (i, j))],
        out_specs=pl.BlockSpec((tm, tn), lambda i, j: (i, j)),
        compiler_params=_cparams(("arbitrary", "arbitrary")),
        name="out_proj",
    )(ya, yb, *([w_all] * nchunk), resid)


def _final_norm_kernel(x_ref, g_ref, o_ref):
    x = x_ref[...]
    ms = jnp.mean(x * x, axis=-1, keepdims=True)
    o_ref[...] = x * lax.rsqrt(ms + NORM_EPS) * g_ref[...]


def final_norm(x, g, tm):
    m, d = x.shape
    return pl.pallas_call(
        _final_norm_kernel,
        out_shape=jax.ShapeDtypeStruct((m, d), F32),
        grid=(m // tm,),
        in_specs=[pl.BlockSpec((tm, d), lambda i: (i, 0)), pl.BlockSpec((1, d), lambda i: (0, 0))],
        out_specs=pl.BlockSpec((tm, d), lambda i: (i, 0)),
        compiler_params=_cparams(("arbitrary",)),
        name="final_norm",
    )(x, g.reshape(1, d))


def _lru_gates(xs, wr, br, wi, bi, sp):
    xb = xs.astype(BF16)
    r = jax.nn.sigmoid(_dot(xb, wr) + br)
    ig = jax.nn.sigmoid(_dot(xb, wi) + bi)
    log_a = -LRU_C * r * sp
    a = jnp.exp(log_a)
    u = jnp.sqrt(-jnp.tanh(log_a) * (a * a + 1.0)) * (ig * xs)
    return a, u


def _softplus(z):
    return jnp.maximum(z, 0.0) + jnp.log1p(jnp.exp(-jnp.abs(z)))


def _lru_prompt_kernel(xa_ref, ga_ref, cw_ref, cb_ref, wr_ref, br_ref, wi_ref, bi_ref, lam_ref,
                       ya_ref, hl_ref, h_sc, halo_sc, *, tt, nblk):
    i = pl.program_id(1)

    @pl.when(i == 0)
    def _():
        h_sc[...] = jnp.zeros_like(h_sc)
        halo_sc[...] = jnp.zeros_like(halo_sc)

    x = xa_ref[...]
    xe = jnp.concatenate([halo_sc[...], x], axis=0)
    halo_sc[...] = x[tt - SUBLANES:, :]
    cw = cw_ref[...]
    y = cb_ref[...] + xe[SUBLANES:, :] * cw[CONV_W - 1:CONV_W, :]
    for k in range(CONV_W - 1):
        y = y + pltpu.roll(xe, CONV_W - 1 - k, 0)[SUBLANES:, :] * cw[k:k + 1, :]
    sp = _softplus(-lam_ref[...])
    rid = lax.broadcasted_iota(jnp.int32, (tt, LANES), 0) & (SUBLANES - 1)
    for n in range(nblk):
        sl = slice(n * LANES, (n + 1) * LANES)
        a, u = _lru_gates(y[:, sl], wr_ref[n], br_ref[:, sl], wi_ref[n], bi_ref[:, sl], sp[:, sl])
        for s in (1, 2, 4):
            keep = rid >= s
            u = jnp.where(keep, a * pltpu.roll(u, s, 0) + u, u)
            a = jnp.where(keep, a * pltpu.roll(a, s, 0), a)
        h = h_sc[:, sl]
        outs = []
        for g in range(tt // SUBLANES):
            rows = slice(g * SUBLANES, (g + 1) * SUBLANES)
            hg = u[rows, :] + a[rows, :] * h
            outs.append(hg)
            h = hg[SUBLANES - 1:, :]
        h_sc[:, sl] = h
        hs = jnp.concatenate(outs, axis=0)
        ya_ref[:, sl] = (hs * _silu(ga_ref[:, sl])).astype(BF16)
    hl_ref[0] = h_sc[...]


def lru_prompt(z, bsz, t, d_a, cw, cb, wr, br, wi, bi, lam, tt=256):
    nt = t // tt
    nblk = d_a // LANES
    row = lambda b, i: (b * nt + i, 0)
    vec = pl.BlockSpec((1, d_a), lambda b, i: (0, 0))
    wspec = pl.BlockSpec((nblk, LANES, LANES), lambda b, i: (0, 0, 0))
    return pl.pallas_call(
        functools.partial(_lru_prompt_kernel, tt=tt, nblk=nblk),
        out_shape=(jax.ShapeDtypeStruct((bsz * t, d_a), BF16), jax.ShapeDtypeStruct((bsz, 1, d_a), F32)),
        grid=(bsz, nt),
        in_specs=[pl.BlockSpec((tt, d_a), row),
                  pl.BlockSpec((tt, d_a), lambda b, i: (b * nt + i, 1)),
                  pl.BlockSpec((CONV_W, d_a), lambda b, i: (0, 0)), vec, wspec, vec, wspec, vec, vec],
        out_specs=(pl.BlockSpec((tt, d_a), row), pl.BlockSpec((1, 1, d_a), lambda b, i: (b, 0, 0))),
        scratch_shapes=[pltpu.VMEM((1, d_a), F32), pltpu.VMEM((SUBLANES, d_a), F32)],
        compiler_params=_cparams(("arbitrary", "arbitrary")),
        name="lru_prompt",
    )(z, z, cw, cb.reshape(1, d_a), wr, br.reshape(1, d_a), wi, bi.reshape(1, d_a), lam.reshape(1, d_a))


def _lru_sample_kernel(x_ref, ga_ref, c0_ref, h0_ref, cw_ref, cb_ref, wr_ref, br_ref, wi_ref, bi_ref, lam_ref,
                       ya_ref, hl_ref, *, ts, nblk):
    cw = cw_ref[...]
    sp = _softplus(-lam_ref[...])
    xe = [c0_ref[k] for k in range(CONV_W - 1)] + [x_ref[t] for t in range(ts)]
    h = h0_ref[...]
    for t in range(ts):
        y = cb_ref[...]
        for k in range(CONV_W):
            y = y + xe[t + k] * cw[k:k + 1, :]
        a_l, u_l = [], []
        for n in range(nblk):
            sl = slice(n * LANES, (n + 1) * LANES)
            a, u = _lru_gates(y[:, sl], wr_ref[n], br_ref[:, sl], wi_ref[n], bi_ref[:, sl], sp[:, sl])
            a_l.append(a)
            u_l.append(u)
        h = jnp.concatenate(a_l, axis=1) * h + jnp.concatenate(u_l, axis=1)
        ya_ref[t] = (h * _silu(ga_ref[t])).astype(BF16)
    for t in range(ts, SROWS):
        ya_ref[t] = jnp.zeros(ya_ref.shape[1:], BF16)
    hl_ref[...] = h


def lru_sample(xa_t, ga_t, conv_t, h0, cw, cb, wr, br, wi, bi, lam, ts):
    _, bsz, d_a = xa_t.shape
    nblk = d_a // LANES
    return pl.pallas_call(
        functools.partial(_lru_sample_kernel, ts=ts, nblk=nblk),
        out_shape=(jax.ShapeDtypeStruct((SROWS, bsz, d_a), BF16), jax.ShapeDtypeStruct((bsz, d_a), F32)),
        compiler_params=pltpu.CompilerParams(vmem_limit_bytes=VMEM_LIMIT),
        name="lru_sample",
    )(xa_t, ga_t, conv_t, h0, cw, cb.reshape(1, d_a), wr, br.reshape(1, d_a), wi, bi.reshape(1, d_a),
      lam.reshape(1, d_a))


def _swa_prompt_kernel(sink_ref, q_ref, kp_ref, kc_ref, vp_ref, vc_ref, gb_ref, cq_ref, sq_ref, cp_ref, sp_ref,
                       yb_ref, kout_ref, *, nq, gq, tq):
    kv = pl.program_id(1)
    i = pl.program_id(2)
    cq = cq_ref[...]
    sq = sq_ref[...]
    k_cur = _rope(kc_ref[...], cq, sq)
    k_prev = _rope(kp_ref[...], cp_ref[...], sp_ref[...])

    @pl.when(i == nq - 1)
    def _():
        kout_ref[0, 0] = k_cur

    k = jnp.concatenate([k_prev, k_cur], axis=0).astype(BF16)
    v = jnp.concatenate([vp_ref[...], vc_ref[...]], axis=0).astype(BF16)
    r = lax.broadcasted_iota(jnp.int32, (tq, 2 * tq), 0)
    c = lax.broadcasted_iota(jnp.int32, (tq, 2 * tq), 1)
    valid = (c > r) & (c <= r + WIN_B) & ((c >= tq) | (i > 0))
    for g in range(gq):
        sl = slice(g * HEAD_DIM, (g + 1) * HEAD_DIM)
        qg = _rope(q_ref[:, sl], cq, sq).astype(BF16)
        s = _dot_nt(qg, k) * ATTN_SCALE
        sink = sink_ref[kv * gq + g]
        m = jnp.maximum(jnp.max(jnp.where(valid, s, NEG), axis=1, keepdims=True), sink)
        p = jnp.where(valid, jnp.exp(s - m), 0.0)
        den = jnp.sum(p, axis=1, keepdims=True) + jnp.exp(sink - m)
        p = p * (1.0 / jnp.maximum(den, DENOM_FLOOR))
        o = _dot(p.astype(BF16), v)
        yb_ref[:, sl] = (o * _silu(gb_ref[:, sl])).astype(BF16)


def swa_prompt(z, sinks, ctab, stab, bsz, t, q_off, k_off, v_off, g_off, n_heads):
    tq = WIN_B
    nq = t // tq
    gq = n_heads // KVH_B
    wq = gq * HEAD_DIM
    assert q_off % wq == 0 and g_off % wq == 0 and k_off % HEAD_DIM == 0 and v_off % HEAD_DIM == 0
    prev = lambda i: jnp.maximum(i - 1, 0)
    in_specs = [
        pl.BlockSpec(memory_space=pltpu.SMEM),
        pl.BlockSpec((tq, wq), lambda b, h, i: (b * nq + i, q_off // wq + h)),
        pl.BlockSpec((tq, HEAD_DIM), lambda b, h, i: (b * nq + prev(i), k_off // HEAD_DIM + h)),
        pl.BlockSpec((tq, HEAD_DIM), lambda b, h, i: (b * nq + i, k_off // HEAD_DIM + h)),
        pl.BlockSpec((tq, HEAD_DIM), lambda b, h, i: (b * nq + prev(i), v_off // HEAD_DIM + h)),
        pl.BlockSpec((tq, HEAD_DIM), lambda b, h, i: (b * nq + i, v_off // HEAD_DIM + h)),
        pl.BlockSpec((tq, wq), lambda b, h, i: (b * nq + i, g_off // wq + h)),
        pl.BlockSpec((tq, LANES), lambda b, h, i: (i, 0)),
        pl.BlockSpec((tq, LANES), lambda b, h, i: (i, 0)),
        pl.BlockSpec((tq, LANES), lambda b, h, i: (prev(i), 0)),
        pl.BlockSpec((tq, LANES), lambda b, h, i: (prev(i), 0)),
    ]
    return pl.pallas_call(
        functools.partial(_swa_prompt_kernel, nq=nq, gq=gq, tq=tq),
        out_shape=(jax.ShapeDtypeStruct((bsz * t, n_heads * HEAD_DIM), BF16),
                   jax.ShapeDtypeStruct((bsz, KVH_B, tq, HEAD_DIM), F32)),
        grid=(bsz, KVH_B, nq),
        in_specs=in_specs,
        out_specs=(pl.BlockSpec((tq, wq), lambda b, h, i: (b * nq + i, h)),
                   pl.BlockSpec((1, 1, tq, HEAD_DIM), lambda b, h, i: (b, h, 0, 0))),
        compiler_params=_cparams(("arbitrary", "arbitrary", "arbitrary")),
        name="swa_prompt",
    )(sinks, z, z, z, z, z, z, ctab, stab, ctab, stab)


def _swa_sample_kernel(sink_ref, z_ref, kc_ref, vc_ref, c_ref, s_ref, yb_ref, kn_ref, *,
                       gq, q_off, k_off, v_off, g_off, ts, wlen, bb):
    c = c_ref[...]
    s = s_ref[...]
    qi = lax.broadcasted_iota(jnp.int32, (SROWS, wlen + SROWS), 0)
    ci = lax.broadcasted_iota(jnp.int32, (SROWS, wlen + SROWS), 1)
    diff = jnp.where(ci < wlen, qi + wlen - ci, qi - (ci - wlen))
    valid = (diff >= 0) & (diff < WIN_B)
    valid_g = jnp.concatenate([valid] * gq, axis=0)
    for bi in range(bb):
        for h in range(KVH_B):
            ks = slice(k_off + h * HEAD_DIM, k_off + (h + 1) * HEAD_DIM)
            vs = slice(v_off + h * HEAD_DIM, v_off + (h + 1) * HEAD_DIM)
            hs = slice(h * HEAD_DIM, (h + 1) * HEAD_DIM)
            k_new = _rope(z_ref[bi, :, ks], c, s)
            kn_ref[bi, :, hs] = k_new
            k = jnp.concatenate([kc_ref[bi, pl.ds(h, wlen, stride=KVH_B), :], k_new], axis=0).astype(BF16)
            v = jnp.concatenate([vc_ref[bi, pl.ds(h, wlen, stride=KVH_B), :], z_ref[bi, :, vs]], axis=0).astype(BF16)
            qs = jnp.concatenate(
                [_rope(z_ref[bi, :, q_off + (h * gq + g) * HEAD_DIM:q_off + (h * gq + g + 1) * HEAD_DIM], c, s)
                 for g in range(gq)], axis=0).astype(BF16)
            sink = jnp.concatenate([jnp.full((SROWS, 1), sink_ref[h * gq + g], F32) for g in range(gq)], axis=0)
            sc = _dot_nt(qs, k) * ATTN_SCALE
            m = jnp.maximum(jnp.max(jnp.where(valid_g, sc, NEG), axis=1, keepdims=True), sink)
            p = jnp.where(valid_g, jnp.exp(sc - m), 0.0)
            den = jnp.sum(p, axis=1, keepdims=True) + jnp.exp(sink - m)
            p = p * (1.0 / jnp.maximum(den, DENOM_FLOOR))
            o = _dot(p.astype(BF16), v)
            for g in range(gq):
                hd = h * gq + g
                gate = _silu(z_ref[bi, :, g_off + hd * HEAD_DIM:g_off + (hd + 1) * HEAD_DIM])
                yb_ref[bi, :, hd * HEAD_DIM:(hd + 1) * HEAD_DIM] = (o[g * SROWS:(g + 1) * SROWS, :] * gate).astype(BF16)


def swa_sample(z3, sinks, cache_k, cache_v, layer, ctab, stab, q_off, k_off, v_off, g_off, n_heads, ts, bb=4):
    bsz, _, p = z3.shape
    wlen = cache_k.shape[1] // KVH_B
    kvw = KVH_B * HEAD_DIM
    assert bsz % bb == 0
    return pl.pallas_call(
        functools.partial(_swa_sample_kernel, gq=n_heads // KVH_B, q_off=q_off, k_off=k_off, v_off=v_off,
                          g_off=g_off, ts=ts, wlen=wlen, bb=bb),
        out_shape=(jax.ShapeDtypeStruct((bsz, SROWS, n_heads * HEAD_DIM), BF16),
                   jax.ShapeDtypeStruct((bsz, SROWS, kvw), F32)),
        grid=(bsz // bb,),
        in_specs=[pl.BlockSpec(memory_space=pltpu.SMEM),
                  pl.BlockSpec((bb, SROWS, p), lambda b: (b, 0, 0)),
                  pl.BlockSpec((bb, wlen * KVH_B, HEAD_DIM), lambda b: (layer * (bsz // bb) + b, 0, 0)),
                  pl.BlockSpec((bb, wlen * KVH_B, HEAD_DIM), lambda b: (layer * (bsz // bb) + b, 0, 0)),
                  pl.BlockSpec((SROWS, LANES), lambda b: (0, 0)),
                  pl.BlockSpec((SROWS, LANES), lambda b: (0, 0))],
        out_specs=(pl.BlockSpec((bb, SROWS, n_heads * HEAD_DIM), lambda b: (b, 0, 0)),
                   pl.BlockSpec((bb, SROWS, kvw), lambda b: (b, 0, 0))),
        compiler_params=_cparams(("arbitrary",)),
        name="swa_sample",
    )(sinks, z3, cache_k, cache_v, ctab, stab)


def _pool_prompt_kernel(x_ref, gc_ref, w_ref, sc_ref, y_ref, halo_sc, *, tt, ngrp):
    i = pl.program_id(1)

    @pl.when(i == 0)
    def _():
        halo_sc[...] = jnp.zeros_like(halo_sc)

    x = x_ref[...]
    xe = jnp.concatenate([halo_sc[...], x], axis=0)
    halo_sc[...] = x[tt - POOL_MAX:, :]
    pos1 = (i * tt + 1 + lax.broadcasted_iota(jnp.int32, (tt, LANES), 0)).astype(F32)
    for g in range(ngrp):
        sl = slice(g * LANES, (g + 1) * LANES)
        w = POOL_WINDOWS[g]
        s = xe[:, sl]
        step = 1
        while step < w:
            s = s + pltpu.roll(s, step, 0)
            step *= 2
        pooled = s[POOL_MAX:, :] / jnp.minimum(float(w), pos1) - x[:, sl]
        y = _dot(pooled.astype(BF16), w_ref[g]) * sc_ref[:, sl]
        y_ref[:, sl] = (y * _silu(gc_ref[:, sl])).astype(BF16)


def pool_prompt(z, bsz, t, d_c, x_off, g_off, w_pool, scale, tt=256):
    nt = t // tt
    ngrp = d_c // LANES
    assert x_off % d_c == 0 and g_off % d_c == 0
    return pl.pallas_call(
        functools.partial(_pool_prompt_kernel, tt=tt, ngrp=ngrp),
        out_shape=jax.ShapeDtypeStruct((bsz * t, d_c), BF16),
        grid=(bsz, nt),
        in_specs=[pl.BlockSpec((tt, d_c), lambda b, i: (b * nt + i, x_off // d_c)),
                  pl.BlockSpec((tt, d_c), lambda b, i: (b * nt + i, g_off // d_c)),
                  pl.BlockSpec((ngrp, LANES, LANES), lambda b, i: (0, 0, 0)),
                  pl.BlockSpec((1, d_c), lambda b, i: (0, 0))],
        out_specs=pl.BlockSpec((tt, d_c), lambda b, i: (b * nt + i, 0)),
        scratch_shapes=[pltpu.VMEM((POOL_MAX, d_c), F32)],
        compiler_params=_cparams(("arbitrary", "arbitrary")),
        name="pool_prompt",
    )(z, z, w_pool, scale.reshape(1, d_c))


def _pool_sample_kernel(x_ref, gc_ref, buf_ref, w_ref, sc_ref, y_ref, *, ts, ngrp, start_pos):
    nbuf = POOL_MAX - 1
    xe = [buf_ref[k] for k in range(nbuf)] + [x_ref[t] for t in range(ts)]
    for t in range(ts):
        cols = []
        for g in range(ngrp):
            sl = slice(g * LANES, (g + 1) * LANES)
            w = POOL_WINDOWS[g]
            s = xe[nbuf + t][:, sl]
            for k in range(1, w):
                s = s + xe[nbuf + t - k][:, sl]
            pooled = s / min(float(w), float(start_pos + t + 1)) - xe[nbuf + t][:, sl]
            cols.append(_dot(pooled.astype(BF16), w_ref[g]))
        y = jnp.concatenate(cols, axis=1) * sc_ref[...]
        y_ref[t] = (y * _silu(gc_ref[t])).astype(BF16)
    for t in range(ts, SROWS):
        y_ref[t] = jnp.zeros(y_ref.shape[1:], BF16)


def pool_sample(xc_t, gc_t, buf_t, w_pool, scale, ts, start_pos):
    _, bsz, d_c = xc_t.shape
    return pl.pallas_call(
        functools.partial(_pool_sample_kernel, ts=ts, ngrp=d_c // LANES, start_pos=start_pos),
        out_shape=jax.ShapeDtypeStruct((SROWS, bsz, d_c), BF16),
        compiler_params=pltpu.CompilerParams(vmem_limit_bytes=VMEM_LIMIT),
        name="pool_sample",
    )(xc_t, gc_t, buf_t, w_pool, scale.reshape(1, d_c))


def _nsa_prep_kernel(q0_ref, q1_ref, q2_ref, kc_ref, vc_ref, ks_ref, vs_ref, kw_ref, vw_ref, c_ref, s_ref,
                     cwk_ref, cwv_ref, qr_ref, kcr_ref, vcf_ref, ksr_ref, vsf_ref, kwr_ref, vwf_ref, kvb_ref,
                     kcmp_ref, vcmp_ref, *, tt, nh):
    c = c_ref[...]
    s = s_ref[...]
    hpb = nh // 3
    for h in range(nh):
        q_ref = (q0_ref, q1_ref, q2_ref)[h // hpb]
        src = slice((h % hpb) * HEAD_DIM, (h % hpb + 1) * HEAD_DIM)
        qr_ref[:, h * HEAD_DIM:(h + 1) * HEAD_DIM] = _rope(q_ref[:, src], c, s).astype(BF16)
    nblk = tt // CMP_BLOCK
    kvw = KVH_D * HEAD_DIM
    for h in range(KVH_D):
        sl = slice(h * HEAD_DIM, (h + 1) * HEAD_DIM)
        kcr = _rope(kc_ref[:, sl], c, s)
        ksr = _rope(ks_ref[:, sl], c, s)
        kwr = _rope(kw_ref[:, sl], c, s)
        for out_ref, val in ((kcr_ref, kcr), (vcf_ref, vc_ref[:, sl]), (ksr_ref, ksr), (vsf_ref, vs_ref[:, sl]),
                             (kwr_ref, kwr), (vwf_ref, vw_ref[:, sl])):
            out_ref[pl.ds(h, tt, stride=KVH_D), :] = val
        for seg, val in enumerate((ksr, vs_ref[:, sl], kwr, vw_ref[:, sl])):
            kvb_ref[:, seg * kvw + h * HEAD_DIM:seg * kvw + (h + 1) * HEAD_DIM] = val.astype(BF16)
        kcmp_ref[0, :, sl] = jnp.sum(kcr.reshape(nblk, CMP_BLOCK, HEAD_DIM) * cwk_ref[:, sl][None], axis=1)
        vcmp_ref[0, :, sl] = jnp.sum(vc_ref[:, sl].reshape(nblk, CMP_BLOCK, HEAD_DIM) * cwv_ref[:, sl][None], axis=1)


def nsa_prep(z, ctab, stab, cwk2, cwv2, bsz, t, offs, n_heads, tt=512):
    nt = t // tt
    qw = n_heads * HEAD_DIM
    kvw = KVH_D * HEAD_DIM
    qb = qw // 3
    assert n_heads % 3 == 0 and offs["q"] % qb == 0
    assert all(offs[k] % kvw == 0 for k in ("kc", "vc", "ks", "vs", "kw", "vw"))
    qspec = lambda k: pl.BlockSpec((tt, qb), lambda b, i: (b * nt + i, offs["q"] // qb + k))
    row = lambda b, i: (b * nt + i, 0)
    kvspec = lambda name: pl.BlockSpec((tt, kvw), lambda b, i: (b * nt + i, offs[name] // kvw))
    return pl.pallas_call(
        functools.partial(_nsa_prep_kernel, tt=tt, nh=n_heads),
        out_shape=(jax.ShapeDtypeStruct((bsz * t, qw), BF16),)
                  + (jax.ShapeDtypeStruct((bsz * t * KVH_D, HEAD_DIM), F32),) * 6
                  + (jax.ShapeDtypeStruct((bsz * t, 4 * kvw), BF16),
                   jax.ShapeDtypeStruct((bsz, t // CMP_BLOCK, kvw), F32),
                   jax.ShapeDtypeStruct((bsz, t // CMP_BLOCK, kvw), F32)),
        grid=(bsz, nt),
        in_specs=[qspec(0), qspec(1), qspec(2),
                  kvspec("kc"), kvspec("vc"), kvspec("ks"), kvspec("vs"), kvspec("kw"), kvspec("vw"),
                  pl.BlockSpec((tt, LANES), lambda b, i: (i, 0)),
                  pl.BlockSpec((tt, LANES), lambda b, i: (i, 0)),
                  pl.BlockSpec((CMP_BLOCK, kvw), lambda b, i: (0, 0)),
                  pl.BlockSpec((CMP_BLOCK, kvw), lambda b, i: (0, 0))],
        out_specs=(pl.BlockSpec((tt, qw), row),) + (pl.BlockSpec((tt * KVH_D, HEAD_DIM), row),) * 6
                  + (pl.BlockSpec((tt, 4 * kvw), row),
                   pl.BlockSpec((1, tt // CMP_BLOCK, kvw), lambda b, i: (b, i, 0)),
                   pl.BlockSpec((1, tt // CMP_BLOCK, kvw), lambda b, i: (b, i, 0))),
        compiler_params=_cparams(("arbitrary", "arbitrary")),
        name="nsa_prep",
    )(z, z, z, z, z, z, z, z, z, ctab, stab, cwk2, cwv2)


def _masked_softmax_rows(s, valid):
    m = jnp.max(jnp.where(valid, s, NEG), axis=1, keepdims=True)
    p = jnp.where(valid, jnp.exp(s - m), 0.0)
    den = jnp.sum(p, axis=1, keepdims=True)
    return p * (1.0 / jnp.maximum(den, DENOM_FLOOR))


def _nsa_prompt_kernel(q_ref, kcmp_ref, vcmp_ref, ks_ref, vs_ref, kw_ref, vw_ref, gd0_ref, gd1_ref, gd2_ref,
                       gl_ref, gbias_ref, *rest, t_len, tq, tk, gq, n_heads, stream):
    if stream is None:
        o_ref, s_sc, mrun_sc, m_sc, lrun_sc, acc_sc, score_sc = rest
    else:
        (pt_ref, kpool, vpool, cwk_ref, cwv_ref, o_ref, kcp_ref, vcp_ref,
         s_sc, mrun_sc, m_sc, lrun_sc, acc_sc, score_sc, kbuf, vbuf, sem, part_sc) = rest
    kv = pl.program_id(1)
    i = pl.program_id(2)
    if stream is not None:
        ppc, cps, page_base, npages = stream
        step = (pl.program_id(0) * pl.num_programs(1) + kv) * pl.num_programs(2) + i
        nsteps = pl.num_programs(0) * pl.num_programs(1) * pl.num_programs(2)
        slot = lax.rem(step, 2)

        def chunk_copies(st, sl):
            seq = lax.div(st, cps)
            chunk = lax.rem(st, cps)
            out = []
            for p in range(ppc):
                page = page_base + pt_ref[seq * npages + chunk * ppc + p]
                out.append(pltpu.make_async_copy(kpool.at[page], kbuf.at[sl, p], sem.at[0, sl]))
                out.append(pltpu.make_async_copy(vpool.at[page], vbuf.at[sl, p], sem.at[1, sl]))
            return out

        @pl.when(step == 0)
        def _():
            for cp in chunk_copies(step, slot):
                cp.start()

        @pl.when(step + 1 < nsteps)
        def _():
            for cp in chunk_copies(step + 1, 1 - slot):
                cp.start()
    nb = t_len // CMP_BLOCK
    q0 = i * tq
    bpt = tk // CMP_BLOCK

    rows = gq * tq
    qs = jnp.concatenate([q_ref[:, g * HEAD_DIM:(g + 1) * HEAD_DIM] for g in range(gq)], axis=0)
    kc = kcmp_ref[0].astype(BF16)
    vc = vcmp_ref[0].astype(BF16)

    blk_r = lax.broadcasted_iota(jnp.int32, (rows, nb), 1)
    qp_r = q0 + (lax.broadcasted_iota(jnp.int32, (rows, nb), 0) & (tq - 1))
    ok_r = (blk_r + 1) * CMP_BLOCK - 1 <= qp_r
    p_cmp = _masked_softmax_rows(_dot_nt(qs, kc) * ATTN_SCALE, ok_r)
    o_cmp = _dot(p_cmp.astype(BF16), vc)
    blk_a = lax.broadcasted_iota(jnp.int32, (nb, rows), 0)
    qp_a = q0 + (lax.broadcasted_iota(jnp.int32, (nb, rows), 1) & (tq - 1))
    ok_a = (blk_a + 1) * CMP_BLOCK - 1 <= qp_a
    st = _dot_nt(kc, qs) * ATTN_SCALE
    mt = jnp.max(jnp.where(ok_a, st, NEG), axis=0, keepdims=True)
    pt = jnp.where(ok_a, jnp.exp(st - mt), 0.0)
    dt = jnp.sum(pt, axis=0, keepdims=True)
    pt = pt * (1.0 / jnp.maximum(dt, DENOM_FLOOR))
    imp = pt[:, 0:tq]
    for g in range(1, gq):
        imp = imp + pt[:, g * tq:(g + 1) * tq]
    blk_c = lax.broadcasted_iota(jnp.int32, (nb, tq), 0)
    qp_c = q0 + lax.broadcasted_iota(jnp.int32, (nb, tq), 1)

    cur = jnp.right_shift(qp_c, CMP_SHIFT)
    forced = (blk_c == 0) | (blk_c == cur) | (blk_c == cur - 1)
    score = jnp.where(blk_c > cur, -1.0, jnp.where(forced, FORCE_SCORE, imp))
    score_sc[...] = score
    n_vis = jnp.minimum((q0 + tq - 1) // CMP_BLOCK + 1, nb)

    def rank_body(k, cnt):
        rk = score_sc[pl.ds(k, 1), :]
        tie = jnp.where(blk_c > k, 1.0, 0.0)
        return cnt + jnp.where(rk > score, 1.0, jnp.where(rk == score, tie, 0.0))

    cnt = lax.fori_loop(0, n_vis, rank_body, jnp.zeros((nb, tq), F32))
    sel_t = jnp.where((cnt < N_SEL) & (score >= 0.0), 1.0, 0.0).astype(BF16)
    eye = jnp.where(lax.broadcasted_iota(jnp.int32, (tq, tq), 0) == lax.broadcasted_iota(jnp.int32, (tq, tq), 1),
                    1.0, 0.0).astype(BF16)
    sel = _dot_nt(eye, sel_t).astype(BF16)

    n_kt = (q0 + tq + tk - 1) // tk
    qp_k = q0 + lax.broadcasted_iota(jnp.int32, (tq, tk), 0)
    col_k = lax.broadcasted_iota(jnp.int32, (tq, tk), 1)
    e_row = lax.broadcasted_iota(jnp.int32, (nb, tk), 0)
    e_col = jnp.right_shift(lax.broadcasted_iota(jnp.int32, (nb, tk), 1), CMP_SHIFT)
    nlt = tk // LANES
    mrun_sc[...] = jnp.full(mrun_sc.shape, NEG, F32)

    def score_pass(kt, carry):
        k0 = pl.multiple_of(kt * tk, tk)
        s = _dot_nt(qs, ks_ref[pl.ds(k0, tk), :])
        expand = jnp.where(e_row == kt * bpt + e_col, 1.0, 0.0).astype(BF16)
        ok = (_dot(sel, expand) > 0.5) & (k0 + col_k <= qp_k)
        bias = jnp.where(ok, 0.0, NEG)
        sb = s * ATTN_SCALE + jnp.concatenate([bias] * gq, axis=0)
        s_sc[kt] = sb
        mx = sb[:, 0:LANES]
        for c in range(1, nlt):
            mx = jnp.maximum(mx, sb[:, c * LANES:(c + 1) * LANES])
        mrun_sc[...] = jnp.maximum(mrun_sc[...], mx)
        return carry

    lax.fori_loop(0, n_kt, score_pass, 0)
    m_sc[...] = jnp.broadcast_to(jnp.max(mrun_sc[...], axis=1, keepdims=True), m_sc.shape)
    lrun_sc[...] = jnp.zeros(lrun_sc.shape, F32)
    acc_sc[...] = jnp.zeros(acc_sc.shape, F32)

    def value_pass(kt, carry):
        k0 = pl.multiple_of(kt * tk, tk)
        sb = s_sc[kt]
        mrep = m_sc[...]
        ps = [jnp.exp(sb[:, c * LANES:(c + 1) * LANES] - mrep) for c in range(nlt)]
        lsum = ps[0]
        for c in range(1, nlt):
            lsum = lsum + ps[c]
        lrun_sc[...] = lrun_sc[...] + lsum
        p = jnp.concatenate(ps, axis=1).astype(BF16)
        acc_sc[...] = acc_sc[...] + _dot(p, vs_ref[pl.ds(k0, tk), :])
        return carry

    lax.fori_loop(0, n_kt, value_pass, 0)
    o_sel = acc_sc[...] * (1.0 / jnp.sum(lrun_sc[...], axis=1, keepdims=True))

    nwb = (WIN_D + tq - 2) // tq + 1
    wlen = nwb * tq
    w0 = pl.multiple_of(jnp.maximum(i - (nwb - 1), 0) * tq, tq)
    dw = (q0 + lax.broadcasted_iota(jnp.int32, (tq, wlen), 0)) - (w0 + lax.broadcasted_iota(jnp.int32, (tq, wlen), 1))
    bias_w = jnp.where((dw >= 0) & (dw < WIN_D), 0.0, NEG)
    sw = _dot_nt(qs, kw_ref[pl.ds(w0, wlen), :]) * ATTN_SCALE + jnp.concatenate([bias_w] * gq, axis=0)
    pw = jnp.exp(sw - jnp.max(sw, axis=1, keepdims=True))
    lw = jnp.sum(pw, axis=1, keepdims=True)
    o_win = _dot(pw.astype(BF16), vw_ref[pl.ds(w0, wlen), :]) * (1.0 / lw)

    gate = jax.nn.sigmoid(gl_ref[...] + gbias_ref[...])
    hpb = gq // 3
    for g in range(gq):
        rs = slice(g * tq, (g + 1) * tq)
        gs = []
        for br in range(3):
            c0 = br * n_heads + g
            c1 = c0 + gq
            gs.append(jnp.where(kv == 0, gate[:, c0:c0 + 1], gate[:, c1:c1 + 1]))
        yd = gs[0] * o_cmp[rs, :] + gs[1] * o_sel[rs, :] + gs[2] * o_win[rs, :]
        gd_ref = (gd0_ref, gd1_ref, gd2_ref)[g // hpb]
        gd = gd_ref[:, (g % hpb) * HEAD_DIM:(g % hpb + 1) * HEAD_DIM]
        o_ref[:, g * HEAD_DIM:(g + 1) * HEAD_DIM] = (yd * _silu(gd)).astype(BF16)

    if stream is not None:
        for cp in chunk_copies(step, slot):
            cp.wait()
        _compress_pages(kbuf, slot, cwk_ref, kcp_ref, part_sc, ppc)
        _compress_pages(vbuf, slot, cwv_ref, vcp_ref, part_sc, ppc)


def nsa_prompt(z, qr, kcmp, vcmp, kvb, gbias, bsz, t, offs, n_heads, stream_args=None, tq=128, tk=512):
    nq = t // tq
    gq = n_heads // KVH_D
    wq = gq * HEAD_DIM
    nb = t // CMP_BLOCK
    gw = wq // 3
    assert gq % 3 == 0 and offs["gd"] % gw == 0 and offs["gl"] % LANES == 0 and t % tk == 0
    assert KVH_D == 2
    gdspec = lambda k: pl.BlockSpec((tq, gw), lambda b, h, i: (b * nq + i, offs["gd"] // gw + 3 * h + k))
    res = lambda seg: pl.BlockSpec((t, HEAD_DIM), lambda b, h, i: (b, seg * KVH_D + h))
    rows = gq * tq
    stat = pltpu.VMEM((rows, LANES), F32)
    in_specs = [pl.BlockSpec((tq, wq), lambda b, h, i: (b * nq + i, h)),
                pl.BlockSpec((1, nb, HEAD_DIM), lambda b, h, i: (b, 0, h)),
                pl.BlockSpec((1, nb, HEAD_DIM), lambda b, h, i: (b, 0, h)),
                res(0), res(1), res(2), res(3),
                gdspec(0), gdspec(1), gdspec(2),
                pl.BlockSpec((tq, LANES), lambda b, h, i: (b * nq + i, offs["gl"] // LANES)),
                pl.BlockSpec((1, LANES), lambda b, h, i: (0, 0))]
    out_shape = [jax.ShapeDtypeStruct((bsz * t, n_heads * HEAD_DIM), BF16)]
    out_specs = [pl.BlockSpec((tq, wq), lambda b, h, i: (b * nq + i, h))]
    scratch = [pltpu.VMEM((t // tk, rows, tk), F32), stat, stat, stat, stat, pltpu.VMEM((nb, tq), F32)]
    args = [qr, kcmp, vcmp, kvb, kvb, kvb, kvb, z, z, z, z, gbias]
    stream = None
    if stream_args is not None:
        pt_flat, kpool, vpool, cwk_page, cwv_page, n_seq, npages, page_base, ppc = stream_args
        cps = npages // ppc
        assert npages % ppc == 0 and n_seq * cps == bsz * KVH_D * nq
        prow = kpool.shape[1]
        bpp = PAGE_SIZE // CMP_BLOCK
        kvw = KVH_D * HEAD_DIM
        stream = (ppc, cps, page_base, npages)
        in_specs += [pl.BlockSpec(memory_space=pltpu.SMEM), pl.BlockSpec(memory_space=pl.ANY),
                     pl.BlockSpec(memory_space=pl.ANY),
                     pl.BlockSpec((prow, LANES), lambda b, h, i: (0, 0)),
                     pl.BlockSpec((prow, LANES), lambda b, h, i: (0, 0))]
        args += [pt_flat, kpool, vpool, cwk_page, cwv_page]

        def cmp_map(b, h, i):
            st = (b * KVH_D + h) * nq + i
            return (st // cps, st % cps, 0)

        out_shape += [jax.ShapeDtypeStruct((n_seq, npages * bpp, kvw), F32)] * 2
        out_specs += [pl.BlockSpec((1, ppc * bpp, kvw), cmp_map)] * 2
        scratch += [pltpu.VMEM((2, ppc, prow, LANES), F32), pltpu.VMEM((2, ppc, prow, LANES), F32),
                    pltpu.SemaphoreType.DMA((2, 2)), pltpu.VMEM((ppc * bpp * SUBLANES, LANES), F32)]
    out = pl.pallas_call(
        functools.partial(_nsa_prompt_kernel, t_len=t, tq=tq, tk=tk, gq=gq, n_heads=n_heads, stream=stream),
        out_shape=tuple(out_shape),
        grid=(bsz, KVH_D, nq),
        in_specs=in_specs,
        out_specs=tuple(out_specs),
        scratch_shapes=scratch,
        compiler_params=pltpu.CompilerParams(dimension_semantics=("arbitrary", "arbitrary", "arbitrary"),
                                             vmem_limit_bytes=NSA_VMEM_LIMIT),
        name="nsa_prompt",
    )(*args)
    return out if stream_args is not None else out[0]


def _compress_pages(buf, slot, w_ref, out_ref, part_sc, ppc):
    nblk = ppc * (PAGE_SIZE // CMP_BLOCK)
    vregs_per_blk = CMP_BLOCK * KVH_D // SUBLANES
    x = buf[slot] * w_ref[...][None]
    part_sc[...] = jnp.sum(x.reshape(nblk, vregs_per_blk, SUBLANES, LANES), axis=1).reshape(nblk * SUBLANES, LANES)
    for h in range(KVH_D):
        acc = part_sc[pl.ds(h, nblk, stride=SUBLANES), :]
        for k in range(1, SUBLANES // KVH_D):
            acc = acc + part_sc[pl.ds(k * KVH_D + h, nblk, stride=SUBLANES), :]
        out_ref[0, :, h * HEAD_DIM:(h + 1) * HEAD_DIM] = acc


def _cmp_stream_kernel(pt_ref, kpool, vpool, cwk_ref, cwv_ref, kcmp_ref, vcmp_ref, kbuf, vbuf, sem, part_sc, *,
                       ppc, nch, nsteps, page_base, npages):
    b = pl.program_id(0)
    c = pl.program_id(1)
    step = b * nch + c
    slot = lax.rem(step, 2)

    def copies(bb, cc, sl):
        out = []
        for p in range(ppc):
            page = page_base + pt_ref[bb * npages + cc * ppc + p]
            out.append(pltpu.make_async_copy(kpool.at[page], kbuf.at[sl, p], sem.at[0, sl]))
            out.append(pltpu.make_async_copy(vpool.at[page], vbuf.at[sl, p], sem.at[1, sl]))
        return out

    @pl.when(step == 0)
    def _():
        for cp in copies(0, 0, 0):
            cp.start()

    @pl.when(step + 1 < nsteps)
    def _():
        nxt = step + 1
        for cp in copies(nxt // nch, lax.rem(nxt, nch), 1 - slot):
            cp.start()

    for cp in copies(b, c, slot):
        cp.wait()

    _compress_pages(kbuf, slot, cwk_ref, kcmp_ref, part_sc, ppc)
    _compress_pages(vbuf, slot, cwv_ref, vcmp_ref, part_sc, ppc)


def cmp_stream(pt_flat, kpool, vpool, cwk_page, cwv_page, bsz, npages, page_base, ppc=16):
    prow = kpool.shape[1]
    width = KVH_D * HEAD_DIM
    nch = npages // ppc
    bpp = PAGE_SIZE // CMP_BLOCK
    grid_spec = pltpu.PrefetchScalarGridSpec(
        num_scalar_prefetch=1,
        grid=(bsz, nch),
        in_specs=[pl.BlockSpec(memory_space=pl.ANY), pl.BlockSpec(memory_space=pl.ANY),
                  pl.BlockSpec((prow, LANES), lambda b, c, pt: (0, 0)),
                  pl.BlockSpec((prow, LANES), lambda b, c, pt: (0, 0))],
        out_specs=(pl.BlockSpec((1, ppc * bpp, width), lambda b, c, pt: (b, c, 0)),) * 2,
        scratch_shapes=[pltpu.VMEM((2, ppc, prow, LANES), F32), pltpu.VMEM((2, ppc, prow, LANES), F32),
                        pltpu.SemaphoreType.DMA((2, 2)), pltpu.VMEM((ppc * bpp * SUBLANES, LANES), F32)],
    )
    return pl.pallas_call(
        functools.partial(_cmp_stream_kernel, ppc=ppc, nch=nch, nsteps=bsz * nch, page_base=page_base,
                          npages=npages),
        out_shape=(jax.ShapeDtypeStruct((bsz, npages * bpp, width), F32),) * 2,
        grid_spec=grid_spec,
        compiler_params=_cparams(("arbitrary", "arbitrary")),
        name="cmp_stream",
    )(pt_flat, kpool, vpool, cwk_page, cwv_page)


def _nsa_sample_cmp_kernel(z_ref, kcp_ref, vcp_ref, c_ref, s_ref, cwk_ref, cwv_ref,
                           ocmp_ref, imp_ref, knew_ref, *, offs, gq, ts, n_past, past, bb):
    c = c_ref[...]
    s = s_ref[...]
    nbp = n_past + SUBLANES
    blk_r = lax.broadcasted_iota(jnp.int32, (gq * SROWS, nbp), 1)
    qp_r = past + (lax.broadcasted_iota(jnp.int32, (gq * SROWS, nbp), 0) & (SROWS - 1))
    ok_r = (blk_r + 1) * CMP_BLOCK - 1 <= qp_r
    blk_c = lax.broadcasted_iota(jnp.int32, (nbp, LANES), 0)
    qp_c = past + (lax.broadcasted_iota(jnp.int32, (nbp, LANES), 1) & (SROWS - 1))
    ok_c = (blk_c + 1) * CMP_BLOCK - 1 <= qp_c
    row8 = lax.broadcasted_iota(jnp.int32, (SROWS, HEAD_DIM), 0)
    is_new = row8 < ts
    qpad = jnp.zeros((LANES - gq * SROWS, HEAD_DIM), BF16)
    kvw = KVH_D * HEAD_DIM
    for bi in range(bb):
        for h in range(KVH_D):
            hs = slice(h * HEAD_DIM, (h + 1) * HEAD_DIM)
            kcr = _rope(z_ref[bi, :, offs["kc"] + h * HEAD_DIM:offs["kc"] + (h + 1) * HEAD_DIM], c, s)
            ksr = _rope(z_ref[bi, :, offs["ks"] + h * HEAD_DIM:offs["ks"] + (h + 1) * HEAD_DIM], c, s)
            kwr = _rope(z_ref[bi, :, offs["kw"] + h * HEAD_DIM:offs["kw"] + (h + 1) * HEAD_DIM], c, s)
            knew_ref[bi, :, h * HEAD_DIM:(h + 1) * HEAD_DIM] = kcr
            knew_ref[bi, :, kvw + h * HEAD_DIM:kvw + (h + 1) * HEAD_DIM] = ksr
            knew_ref[bi, :, 2 * kvw + h * HEAD_DIM:2 * kvw + (h + 1) * HEAD_DIM] = kwr
            vcn = z_ref[bi, :, offs["vc"] + h * HEAD_DIM:offs["vc"] + (h + 1) * HEAD_DIM]
            nk = jnp.sum(jnp.where(is_new, kcr * cwk_ref[0:SROWS, hs], 0.0), axis=0, keepdims=True)
            nv = jnp.sum(jnp.where(is_new, vcn * cwv_ref[0:SROWS, hs], 0.0), axis=0, keepdims=True)
            nk8 = jnp.where(row8 == 0, jnp.broadcast_to(nk, (SROWS, HEAD_DIM)), 0.0)
            nv8 = jnp.where(row8 == 0, jnp.broadcast_to(nv, (SROWS, HEAD_DIM)), 0.0)
            kall = jnp.concatenate([kcp_ref[bi, :, hs], nk8], axis=0).astype(BF16)
            vall = jnp.concatenate([vcp_ref[bi, :, hs], nv8], axis=0).astype(BF16)
            qs = jnp.concatenate(
                [_rope(z_ref[bi, :, offs["q"] + (h * gq + g) * HEAD_DIM:offs["q"] + (h * gq + g + 1) * HEAD_DIM], c, s)
                 for g in range(gq)], axis=0).astype(BF16)
            p = _masked_softmax_rows(_dot_nt(qs, kall) * ATTN_SCALE, ok_r)
            o = _dot(p.astype(BF16), vall)
            for g in range(gq):
                hd = h * gq + g
                ocmp_ref[bi, :, hd * HEAD_DIM:(hd + 1) * HEAD_DIM] = o[g * SROWS:(g + 1) * SROWS, :]
            st = _dot_nt(kall, jnp.concatenate([qs, qpad], axis=0)) * ATTN_SCALE
            mt = jnp.max(jnp.where(ok_c, st, NEG), axis=0, keepdims=True)
            pt = jnp.where(ok_c, jnp.exp(st - mt), 0.0)
            dt = jnp.sum(pt, axis=0, keepdims=True)
            pt = pt * (1.0 / jnp.maximum(dt, DENOM_FLOOR))
            imp = pt
            for g in range(1, gq):
                imp = imp + pltpu.roll(pt, LANES - g * SROWS, 1)
            imp_ref[bi, h] = imp


def nsa_sample_cmp(z3, kcp, vcp, ctab, stab, cwk2, cwv2, offs, n_heads, ts, past, bb=4):
    bsz, _, p = z3.shape
    n_past = kcp.shape[1]
    kvw = KVH_D * HEAD_DIM
    nbp = n_past + SUBLANES
    assert bsz % bb == 0
    return pl.pallas_call(
        functools.partial(_nsa_sample_cmp_kernel, offs=offs, gq=n_heads // KVH_D, ts=ts, n_past=n_past, past=past,
                          bb=bb),
        out_shape=(jax.ShapeDtypeStruct((bsz, SROWS, n_heads * HEAD_DIM), F32),
                   jax.ShapeDtypeStruct((bsz, KVH_D, nbp, LANES), F32),
                   jax.ShapeDtypeStruct((bsz, SROWS, 3 * kvw), F32)),
        grid=(bsz // bb,),
        in_specs=[pl.BlockSpec((bb, SROWS, p), lambda b: (b, 0, 0)),
                  pl.BlockSpec((bb, n_past, kvw), lambda b: (b, 0, 0)),
                  pl.BlockSpec((bb, n_past, kvw), lambda b: (b, 0, 0)),
                  pl.BlockSpec((SROWS, LANES), lambda b: (0, 0)),
                  pl.BlockSpec((SROWS, LANES), lambda b: (0, 0)),
                  pl.BlockSpec((CMP_BLOCK, kvw), lambda b: (0, 0)),
                  pl.BlockSpec((CMP_BLOCK, kvw), lambda b: (0, 0))],
        out_specs=(pl.BlockSpec((bb, SROWS, n_heads * HEAD_DIM), lambda b: (b, 0, 0)),
                   pl.BlockSpec((bb, KVH_D, nbp, LANES), lambda b: (b, 0, 0, 0)),
                   pl.BlockSpec((bb, SROWS, 3 * kvw), lambda b: (b, 0, 0))),
        compiler_params=_cparams(("arbitrary",)),
        name="nsa_sample_cmp",
    )(z3, kcp, vcp, ctab, stab, cwk2, cwv2)


def _select_sample_kernel(imp_ref, qpos_ref, idx_ref, score_sc, cnt_sc, *, nbp, nsel):
    shape = score_sc.shape
    blk = lax.broadcasted_iota(jnp.int32, shape, 0)
    cur = jnp.right_shift(qpos_ref[...], CMP_SHIFT)
    forced = (blk == 0) | (blk == cur) | (blk == cur - 1)
    score_sc[...] = jnp.where(blk > cur, -1.0, jnp.where(forced, FORCE_SCORE, imp_ref[...]))
    cnt_sc[...] = jnp.zeros(shape, F32)

    def body(k, carry):
        rk = score_sc[pl.ds(k, 1), :]
        sc = score_sc[...]
        tie = jnp.where(blk > k, 1.0, 0.0)
        cnt_sc[...] = cnt_sc[...] + jnp.where(rk > sc, 1.0, jnp.where(rk == sc, tie, 0.0))
        return carry

    lax.fori_loop(0, nbp, body, 0)
    cnt = cnt_sc[...]
    ok = score_sc[...] >= 0.0
    blk_f = blk.astype(F32)
    for r in range(nsel):
        picked = jnp.sum(jnp.where((cnt == float(r)) & ok, blk_f, 0.0), axis=0, keepdims=True)
        idx_ref[r:r + 1, :] = picked.astype(jnp.int32)


def select_sample(imp_t, qpos):
    nbp, nc = imp_t.shape
    return pl.pallas_call(
        functools.partial(_select_sample_kernel, nbp=nbp, nsel=N_SEL),
        out_shape=jax.ShapeDtypeStruct((N_SEL, nc), jnp.int32),
        scratch_shapes=[pltpu.VMEM((nbp, nc), F32), pltpu.VMEM((nbp, nc), F32)],
        compiler_params=pltpu.CompilerParams(vmem_limit_bytes=VMEM_LIMIT),
        name="select_sample",
    )(imp_t, qpos)


def _nsa_sample_sel_kernel(idx_ref, pt_ref, z_ref, ocmp_ref, knew_ref, wk_ref, wv_ref, c_ref, s_ref, gbias_ref,
                           kpool, vpool, y_ref, kbuf, vbuf, sem, *, offs, gq, n_heads, ts, n_past, past,
                           page_base, npages, wlen):
    b = pl.program_id(0)
    c = c_ref[...]
    s = s_ref[...]
    bpp = PAGE_SIZE // CMP_BLOCK
    kvw = KVH_D * HEAD_DIM
    blk_rows = CMP_BLOCK * KVH_D

    nslots = KVH_D * ts * N_SEL
    nseq = pl.num_programs(0)
    half = lax.rem(b, 2)

    def slot_copies(seq, hf, sidx):
        blk = idx_ref[seq * nslots + sidx]
        past_blk = jnp.minimum(blk, n_past - 1)
        page = page_base + pt_ref[seq * npages + lax.div(past_blk, bpp)]
        row0 = pl.multiple_of(lax.rem(past_blk, bpp) * blk_rows, blk_rows)
        return (pltpu.make_async_copy(kpool.at[page, pl.ds(row0, blk_rows), :], kbuf.at[hf, sidx], sem.at[0, hf]),
                pltpu.make_async_copy(vpool.at[page, pl.ds(row0, blk_rows), :], vbuf.at[hf, sidx], sem.at[1, hf]))

    def issue_all(seq, hf):
        def issue(sidx, carry):
            for cp in slot_copies(seq, hf, sidx):
                cp.start()
            return carry

        lax.fori_loop(0, nslots, issue, 0)

    @pl.when(b == 0)
    def _():
        issue_all(b, half)

    @pl.when(b + 1 < nseq)
    def _():
        issue_all(b + 1, 1 - half)

    row8 = lax.broadcasted_iota(jnp.int32, (SROWS, HEAD_DIM), 0)
    kpad = jnp.zeros((LANES - SROWS, HEAD_DIM), F32)
    qi = lax.broadcasted_iota(jnp.int32, (SROWS, wlen + LANES), 0)
    ci = lax.broadcasted_iota(jnp.int32, (SROWS, wlen + LANES), 1)
    dw = jnp.where(ci < wlen, qi + wlen - ci, qi - (ci - wlen))
    ok_w = (dw >= 0) & (dw < WIN_D) & ((ci < wlen) | (ci - wlen < ts))
    ok_wg = jnp.concatenate([ok_w] * gq, axis=0)
    qr = []
    o_win = []
    for h in range(KVH_D):
        hs = slice(h * HEAD_DIM, (h + 1) * HEAD_DIM)
        kwn = knew_ref[0, :, 2 * kvw + h * HEAD_DIM:2 * kvw + (h + 1) * HEAD_DIM]
        vwn = z_ref[0, :, offs["vw"] + h * HEAD_DIM:offs["vw"] + (h + 1) * HEAD_DIM]
        kw = jnp.concatenate([wk_ref[0, pl.ds(h, wlen, stride=KVH_D), :], kwn, kpad], axis=0).astype(BF16)
        vw = jnp.concatenate([wv_ref[0, pl.ds(h, wlen, stride=KVH_D), :], vwn, kpad], axis=0).astype(BF16)
        qh = [_rope(z_ref[0, :, offs["q"] + (h * gq + g) * HEAD_DIM:offs["q"] + (h * gq + g + 1) * HEAD_DIM], c, s)
              for g in range(gq)]
        qr.extend(qh)
        qs = jnp.concatenate(qh, axis=0).astype(BF16)
        pw = _masked_softmax_rows(_dot_nt(qs, kw) * ATTN_SCALE, ok_wg)
        ow = _dot(pw.astype(BF16), vw)
        o_win.extend([ow[g * SROWS:(g + 1) * SROWS, :] for g in range(gq)])

    def drain(sidx, carry):
        for cp in slot_copies(b, half, sidx):
            cp.wait()
        return carry

    lax.fori_loop(0, nslots, drain, 0)

    nkeys = N_SEL * CMP_BLOCK
    colk = lax.broadcasted_iota(jnp.int32, (SROWS, nkeys + LANES), 1)
    slot_of_col = jnp.right_shift(colk, CMP_SHIFT)
    in_blk = colk & (CMP_BLOCK - 1)
    o_sel = [jnp.zeros((SROWS, HEAD_DIM), F32) for _ in range(n_heads)]
    for h in range(KVH_D):
        ksn = knew_ref[0, :, kvw + h * HEAD_DIM:kvw + (h + 1) * HEAD_DIM]
        vsn = z_ref[0, :, offs["vs"] + h * HEAD_DIM:offs["vs"] + (h + 1) * HEAD_DIM]
        for t in range(ts):
            qt = jnp.zeros((SROWS, HEAD_DIM), F32)
            for g in range(gq):
                qt = jnp.where(row8 == g, jnp.broadcast_to(qr[h * gq + g][t:t + 1, :], (SROWS, HEAD_DIM)), qt)
            qt = qt.astype(BF16)
            s0 = (h * ts + t) * N_SEL
            kg = kbuf[half, pl.ds(s0, N_SEL), pl.ds(h, CMP_BLOCK, stride=KVH_D), :].reshape(nkeys, HEAD_DIM)
            vg = vbuf[half, pl.ds(s0, N_SEL), pl.ds(h, CMP_BLOCK, stride=KVH_D), :].reshape(nkeys, HEAD_DIM)
            kall = jnp.concatenate([kg, ksn, kpad], axis=0).astype(BF16)
            vall = jnp.concatenate([vg, vsn, kpad], axis=0).astype(BF16)
            far = past + ts + SROWS
            kpos = jnp.where(colk >= nkeys, past + (colk - nkeys), 0)
            kpos = jnp.where((colk >= nkeys) & (colk - nkeys >= ts), far, kpos)
            n_new = jnp.int32(0)
            for r in range(N_SEL):
                blk = idx_ref[((b * KVH_D + h) * ts + t) * N_SEL + r]
                n_new = n_new + jnp.where(blk >= n_past, 1, 0)
                base = jnp.where(blk < n_past, blk * CMP_BLOCK, far)
                kpos = jnp.where(slot_of_col == r, base + in_blk, kpos)
            kpos = jnp.where((colk >= nkeys) & (n_new == 0), far, kpos)
            ok = kpos <= past + t
            p = _masked_softmax_rows(_dot_nt(qt, kall) * ATTN_SCALE, ok)
            res = _dot(p.astype(BF16), vall)
            for g in range(gq):
                hd = h * gq + g
                o_sel[hd] = jnp.where(row8 == t, jnp.broadcast_to(res[g:g + 1, :], (SROWS, HEAD_DIM)), o_sel[hd])

    ngate = 3 * n_heads
    gate = jax.nn.sigmoid(z_ref[0, :, offs["gl"]:offs["gl"] + ngate] + gbias_ref[:, 0:ngate])
    for hd in range(n_heads):
        sl = slice(hd * HEAD_DIM, (hd + 1) * HEAD_DIM)
        yd = (gate[:, hd:hd + 1] * ocmp_ref[0, :, sl]
              + gate[:, n_heads + hd:n_heads + hd + 1] * o_sel[hd]
              + gate[:, 2 * n_heads + hd:2 * n_heads + hd + 1] * o_win[hd])
        y_ref[0, :, sl] = (yd * _silu(z_ref[0, :, offs["gd"] + hd * HEAD_DIM:offs["gd"] + (hd + 1) * HEAD_DIM])
                           ).astype(BF16)


def nsa_sample_sel(idx_flat, pt_flat, z3, ocmp, knew, wk, wv, ctab, stab, gbias, kpool, vpool, offs, n_heads,
                   ts, past, page_base, npages, layer, n_layers_b):
    bsz, _, p = z3.shape
    wlen = wk.shape[1] // KVH_D
    kvw = KVH_D * HEAD_DIM
    qw = n_heads * HEAD_DIM
    n_past = npages * (PAGE_SIZE // CMP_BLOCK)
    grid_spec = pltpu.PrefetchScalarGridSpec(
        num_scalar_prefetch=2,
        grid=(bsz,),
        in_specs=[pl.BlockSpec((1, SROWS, p), lambda b, i, t: (b, 0, 0)),
                  pl.BlockSpec((1, SROWS, qw), lambda b, i, t: (b, 0, 0)),
                  pl.BlockSpec((1, SROWS, 3 * kvw), lambda b, i, t: (b, 0, 0)),
                  pl.BlockSpec((1, wlen * KVH_D, HEAD_DIM), lambda b, i, t: (layer * n_layers_b + b, 0, 0)),
                  pl.BlockSpec((1, wlen * KVH_D, HEAD_DIM), lambda b, i, t: (layer * n_layers_b + b, 0, 0)),
                  pl.BlockSpec((SROWS, LANES), lambda b, i, t: (0, 0)),
                  pl.BlockSpec((SROWS, LANES), lambda b, i, t: (0, 0)),
                  pl.BlockSpec((1, LANES), lambda b, i, t: (0, 0)),
                  pl.BlockSpec(memory_space=pl.ANY), pl.BlockSpec(memory_space=pl.ANY)],
        out_specs=pl.BlockSpec((1, SROWS, qw), lambda b, i, t: (b, 0, 0)),
        scratch_shapes=[pltpu.VMEM((2, KVH_D * ts * N_SEL, CMP_BLOCK * KVH_D, HEAD_DIM), F32),
                        pltpu.VMEM((2, KVH_D * ts * N_SEL, CMP_BLOCK * KVH_D, HEAD_DIM), F32),
                        pltpu.SemaphoreType.DMA((2, 2))],
    )
    return pl.pallas_call(
        functools.partial(_nsa_sample_sel_kernel, offs=offs, gq=n_heads // KVH_D, n_heads=n_heads, ts=ts,
                          n_past=n_past, past=past, page_base=page_base, npages=npages, wlen=wlen),
        out_shape=jax.ShapeDtypeStruct((bsz, SROWS, qw), BF16),
        grid_spec=grid_spec,
        compiler_params=_cparams(("arbitrary",)),
        name="nsa_sample_sel",
    )(idx_flat, pt_flat, z3, ocmp, knew, wk, wv, ctab, stab, gbias, kpool, vpool)


EVEN_OFFS = dict(xa=0, ga=1024, q=2048, k=3072, v=3328, gb=3584)


def _odd_layout(d_c, d_d, kvw, n_gate):
    offs = {}
    pos = 0
    for name, width in (("xc", d_c), ("gc", d_c), ("q", d_d), ("kc", kvw), ("vc", kvw), ("ks", kvw), ("vs", kvw),
                        ("kw", kvw), ("vw", kvw), ("gd", d_d), ("gl", n_gate)):
        offs[name] = pos
        pos += width
    return offs, pos


def _pick_tile(n, prefs):
    for t in prefs:
        if n % t == 0:
            return t
    raise ValueError(n)


def kernel(x_prompt, x_sample, state_lru_h, state_lru_conv, cache_swa_k, cache_swa_v, state_pool, cache_nsa_cmp_k, cache_nsa_cmp_v, cache_nsa_sel_k, cache_nsa_sel_v, cache_nsa_win_k, cache_nsa_win_v, page_table, norm_g, final_g, w_in_even, conv_w, conv_b, w_rgate, b_rgate, w_igate, b_igate, lru_lambda, swa_sinks, w_out_even, w_in_odd, w_pool, pool_scale, cmp_wk, cmp_wv, nsa_gate_b, w_out_odd):
    bp, tp, d = x_prompt.shape
    bs, ts, _ = x_sample.shape
    depth = norm_g.shape[0]
    npages = page_table.shape[1]
    past = npages * PAGE_SIZE
    n_phys = cache_nsa_cmp_k.shape[1]
    d_a = state_lru_h.shape[-1]
    n_heads_b = swa_sinks.shape[1]
    d_b = n_heads_b * HEAD_DIM
    d_c = state_pool.shape[-1]
    n_heads_d = nsa_gate_b.shape[1] // 3
    d_d = n_heads_d * HEAD_DIM
    kvw_b = KVH_B * HEAD_DIM
    kvw_d = KVH_D * HEAD_DIM
    assert ts <= SROWS and past % CMP_BLOCK == 0 and (past + ts - 1) // CMP_BLOCK == past // CMP_BLOCK

    hp = x_prompt.reshape(bp * tp, d)
    hs = jnp.pad(x_sample, ((0, 0), (0, SROWS - ts), (0, 0))).reshape(bs * SROWS, d)
    ms = bs * SROWS
    ctab_p, stab_p = rope_tables(tp, 0)
    ctab_s, stab_s = rope_tables(SROWS, past)
    tm_p = _pick_tile(bp * tp, (1024, 512, 256, 128))
    pt_flat = page_table.reshape(-1)

    odd_offs, odd_total = _odd_layout(d_c, d_d, kvw_d, 3 * n_heads_d)
    w_in_even_b = w_in_even.astype(BF16)
    w_in_odd_b = w_in_odd.astype(BF16)
    w_out_even_b = w_out_even.astype(BF16)
    w_out_odd_b = w_out_odd.astype(BF16)
    ev_p, ev_s, od_p, od_s = [], [], [], []
    for i in range(depth):
        j = i // 2
        if i % 2 == 0:
            p_even = w_in_even.shape[2]
            tn = _pick_tile(p_even, (512, 256, 128))
            zp = norm_proj(hp, norm_g[i], w_in_even_b, j, tm_p, tn)
            zs = norm_proj(hs, norm_g[i], w_in_even_b, j, ms, tn)
            wr = w_rgate[j].astype(BF16)
            wi = w_igate[j].astype(BF16)
            lru_args = (conv_w[j], conv_b[j], wr, b_rgate[j], wi, b_igate[j], lru_lambda[j])
            eo = EVEN_OFFS
            ya_p, hl_p = lru_prompt(zp, bp, tp, d_a, *lru_args)
            yb_p, kout_p = swa_prompt(zp, swa_sinks[j], ctab_p, stab_p, bp, tp, eo["q"], eo["k"], eo["v"], eo["gb"],
                                      n_heads_b)
            tn_o = _pick_tile(d, (512, 256, 128))
            hp = out_proj(ya_p, yb_p, w_out_even_b, j, hp, tm_p, tn_o)
            zp3 = zp.reshape(bp, tp, p_even)
            keep = min(WIN_B, tp)
            ev_p.append((hl_p[:, 0], zp3[:, tp - (CONV_W - 1):, eo["xa"]:eo["xa"] + d_a],
                         kout_p.transpose(0, 2, 1, 3)[:, WIN_B - keep:],
                         zp3[:, tp - keep:, eo["v"]:eo["v"] + kvw_b].reshape(bp, keep, KVH_B, HEAD_DIM)))
            zs3 = zs.reshape(bs, SROWS, p_even)
            zst = zs3.transpose(1, 0, 2)
            ya_t, hl_s = lru_sample(zst[:, :, eo["xa"]:eo["xa"] + d_a], zst[:, :, eo["ga"]:eo["ga"] + d_a],
                                    state_lru_conv[j].transpose(1, 0, 2), state_lru_h[j], *lru_args, ts)
            ya_s = ya_t.transpose(1, 0, 2).reshape(ms, d_a)
            rows3 = lambda a: a.reshape(a.shape[0] * a.shape[1], a.shape[2] * a.shape[3], a.shape[4])
            yb_s, knew = swa_sample(zs3, swa_sinks[j], rows3(cache_swa_k), rows3(cache_swa_v), j, ctab_s, stab_s,
                                    eo["q"], eo["k"], eo["v"], eo["gb"], n_heads_b, ts)
            hs = out_proj(ya_s, yb_s.reshape(ms, d_b), w_out_even_b, j, hs, ms, tn_o)
            wlen = cache_swa_k.shape[2]
            new_conv = jnp.concatenate([state_lru_conv[j], zs3[:, :ts, eo["xa"]:eo["xa"] + d_a]], axis=1)[:, -(CONV_W - 1):]
            new_k = jnp.concatenate([cache_swa_k[j], knew[:, :ts].reshape(bs, ts, KVH_B, HEAD_DIM)], axis=1)[:, -wlen:]
            new_v = jnp.concatenate([cache_swa_v[j], zs3[:, :ts, eo["v"]:eo["v"] + kvw_b].reshape(bs, ts, KVH_B, HEAD_DIM)],
                                    axis=1)[:, -wlen:]
            ev_s.append((hl_s, new_conv, new_k, new_v))
        else:
            oo = odd_offs
            tn = 512
            zp = norm_proj(hp, norm_g[i], w_in_odd_b, j, tm_p, tn)
            zs = norm_proj(hs, norm_g[i], w_in_odd_b, j, ms, tn)
            wp = w_pool[j].astype(BF16)
            cwk2 = jnp.repeat(cmp_wk[j], HEAD_DIM, axis=1)
            cwv2 = jnp.repeat(cmp_wv[j], HEAD_DIM, axis=1)
            gbias = jnp.pad(nsa_gate_b[j], (0, LANES - 3 * n_heads_d)).reshape(1, LANES)
            tn_o = _pick_tile(d, (512, 256, 128))
            yc_p = pool_prompt(zp, bp, tp, d_c, oo["xc"], oo["gc"], wp, pool_scale[j])
            qr, kcr, vcf, ksr, vsf, kwr, vwf, kvb, kcmp, vcmp = nsa_prep(zp, ctab_p, stab_p, cwk2, cwv2, bp, tp, oo,
                                                                         n_heads_d)
            n_layers_odd = cache_nsa_cmp_k.shape[0]
            pool3 = lambda a: a.reshape(n_layers_odd * n_phys, PAGE_SIZE * KVH_D, HEAD_DIM)
            page_w = lambda w: jnp.broadcast_to(jnp.tile(w, (PAGE_SIZE // CMP_BLOCK, 1)).reshape(-1, 1),
                                                (PAGE_SIZE * KVH_D, LANES))
            cwk_page = page_w(cmp_wk[j])
            cwv_page = page_w(cmp_wv[j])
            nsteps_nsa = bp * KVH_D * (tp // 128)
            ppc = (bs * npages) // nsteps_nsa if (bs * npages) % nsteps_nsa == 0 else 0
            fused = 0 < ppc <= 32 and npages % ppc == 0
            if fused:
                yd_p, kcp, vcp = nsa_prompt(zp, qr, kcmp, vcmp, kvb, gbias, bp, tp, oo, n_heads_d,
                                            (pt_flat, pool3(cache_nsa_cmp_k), pool3(cache_nsa_cmp_v), cwk_page, cwv_page,
                                             bs, npages, j * n_phys, ppc))
            else:
                yd_p = nsa_prompt(zp, qr, kcmp, vcmp, kvb, gbias, bp, tp, oo, n_heads_d)
                kcp, vcp = cmp_stream(pt_flat, pool3(cache_nsa_cmp_k), pool3(cache_nsa_cmp_v), cwk_page, cwv_page,
                                      bs, npages, j * n_phys)
            hp = out_proj(yc_p, yd_p, w_out_odd_b, j, hp, tm_p, tn_o)
            zp3 = zp.reshape(bp, tp, odd_total)
            kv4 = lambda a: a.reshape(bp, tp, KVH_D, HEAD_DIM)
            keep = min(WIN_D, tp)
            od_p.append((zp3[:, tp - (POOL_MAX - 1):, oo["xc"]:oo["xc"] + d_c],
                         kv4(kcr), kv4(vcf), kv4(ksr), kv4(vsf), kv4(kwr)[:, tp - keep:], kv4(vwf)[:, tp - keep:]))
            zs3 = zs.reshape(bs, SROWS, odd_total)
            zst = zs3.transpose(1, 0, 2)
            yc_t = pool_sample(zst[:, :, oo["xc"]:oo["xc"] + d_c], zst[:, :, oo["gc"]:oo["gc"] + d_c],
                               state_pool[j].transpose(1, 0, 2), wp, pool_scale[j], ts, past)
            yc_s = yc_t.transpose(1, 0, 2).reshape(ms, d_c)
            ocmp, imp, knew = nsa_sample_cmp(zs3, kcp, vcp, ctab_s, stab_s, cwk2, cwv2, oo, n_heads_d, ts, past)
            nbp = imp.shape[2]
            assert nbp >= N_SEL
            imp_t = imp[:, :, :, :ts].transpose(2, 0, 1, 3).reshape(nbp, bs * KVH_D * ts)
            qpos = jnp.tile(past + jnp.arange(ts, dtype=jnp.int32), bs * KVH_D).reshape(1, -1)
            idx = select_sample(imp_t, qpos)
            idx_flat = idx.T.reshape(-1)
            wlen = cache_nsa_win_k.shape[2]
            win3 = lambda a: a.reshape(n_layers_odd * bs, wlen * KVH_D, HEAD_DIM)
            yd_s = nsa_sample_sel(idx_flat, pt_flat, zs3, ocmp, knew, win3(cache_nsa_win_k), win3(cache_nsa_win_v),
                                  ctab_s, stab_s, gbias, pool3(cache_nsa_sel_k), pool3(cache_nsa_sel_v), oo,
                                  n_heads_d, ts, past, j * n_phys, npages, j, bs)
            hs = out_proj(yc_s, yd_s.reshape(ms, d_d), w_out_odd_b, j, hs, ms, tn_o)
            kv4s = lambda a: a[:, :ts].reshape(bs, ts, KVH_D, HEAD_DIM)
            new_pool = jnp.concatenate([state_pool[j], zs3[:, :ts, oo["xc"]:oo["xc"] + d_c]], axis=1)[:, -(POOL_MAX - 1):]
            kwn = kv4s(knew[:, :, 2 * kvw_d:3 * kvw_d])
            vwn = kv4s(zs3[:, :, oo["vw"]:oo["vw"] + kvw_d])
            od_s.append((new_pool,
                         kv4s(knew[:, :, 0:kvw_d]), kv4s(zs3[:, :, oo["vc"]:oo["vc"] + kvw_d]),
                         kv4s(knew[:, :, kvw_d:2 * kvw_d]), kv4s(zs3[:, :, oo["vs"]:oo["vs"] + kvw_d]),
                         jnp.concatenate([cache_nsa_win_k[j], kwn], axis=1)[:, -wlen:],
                         jnp.concatenate([cache_nsa_win_v[j], vwn], axis=1)[:, -wlen:]))

    y_prompt = final_norm(hp, final_g, tm_p).reshape(bp, tp, d)
    y_sample = final_norm(hs, final_g, ms).reshape(bs, SROWS, d)[:, :ts]

    def field(lst, k):
        return jnp.stack([st[k] for st in lst], axis=0)

    return (y_prompt, y_sample,
            field(ev_p, 0), field(ev_p, 1), field(ev_p, 2), field(ev_p, 3),
            field(od_p, 0), field(od_p, 1), field(od_p, 2), field(od_p, 3), field(od_p, 4), field(od_p, 5), field(od_p, 6),
            field(ev_s, 0), field(ev_s, 1), field(ev_s, 2), field(ev_s, 3),
            field(od_s, 0), field(od_s, 1), field(od_s, 2), field(od_s, 3), field(od_s, 4), field(od_s, 5), field(od_s, 6))
```

```python
import functools

import jax
import jax.numpy as jnp
import numpy as np
from jax import lax
from jax.experimental import pallas as pl
from jax.experimental.pallas import tpu as pltpu

F32 = jnp.float32
BF16 = jnp.bfloat16

LANES = 128
SUBLANES = 8
VMEM_LIMIT = 48 * 1024 * 1024
NSA_VMEM_LIMIT = 56 * 1024 * 1024

HEAD_DIM = 128
ATTN_SCALE = HEAD_DIM ** -0.5
ROPE_THETA = 10000.0
NORM_EPS = 1e-6
PAGE_SIZE = 128
CONV_W = 4
LRU_C = 8.0
KVH_B = 2
WIN_B = 128
POOL_WINDOWS = (2, 4, 8, 16)
POOL_MAX = 16
KVH_D = 2
CMP_BLOCK = 64
CMP_SHIFT = 6
N_SEL = 16
WIN_D = 512
FORCE_SCORE = 1e6
DENOM_FLOOR = 1e-30
NEG = -1e30
SROWS = 8


def _cparams(sem):
    return pltpu.CompilerParams(dimension_semantics=sem, vmem_limit_bytes=VMEM_LIMIT)


def _dot(a, b):
    return jnp.dot(a, b, preferred_element_type=F32)


def _dot_nt(a, b):
    return lax.dot_general(a, b, (((1,), (1,)), ((), ())), preferred_element_type=F32)


def _silu(x):
    return x * jax.nn.sigmoid(x)


def _rope(x, c, s):
    return x * c + pltpu.roll(x, HEAD_DIM // 2, 1) * s


def _rope_tab_kernel(inv_ref, sgn_ref, c_ref, s_ref, *, start, rows):
    i = pl.program_id(0)
    pos = start + i * rows + lax.broadcasted_iota(jnp.int32, (rows, LANES), 0)
    ang = pos.astype(F32) * inv_ref[...]
    c_ref[...] = jnp.cos(ang)
    s_ref[...] = jnp.sin(ang) * sgn_ref[...]


def rope_tables(n_rows, start):
    half = HEAD_DIM // 2
    inv = ROPE_THETA ** (-jnp.arange(half, dtype=F32) / half)
    inv2 = jnp.concatenate([inv, inv]).reshape(1, LANES)
    sgn = jnp.concatenate([-jnp.ones((half,), F32), jnp.ones((half,), F32)]).reshape(1, LANES)
    rows = min(n_rows, 512)
    assert n_rows % rows == 0
    return pl.pallas_call(
        functools.partial(_rope_tab_kernel, start=start, rows=rows),
        out_shape=(jax.ShapeDtypeStruct((n_rows, LANES), F32),) * 2,
        grid=(n_rows // rows,),
        in_specs=[pl.BlockSpec((1, LANES), lambda i: (0, 0))] * 2,
        out_specs=(pl.BlockSpec((rows, LANES), lambda i: (i, 0)),) * 2,
        compiler_params=_cparams(("arbitrary",)),
        name="rope_tables",
    )(inv2, sgn)


def _norm_proj_kernel(x_ref, g_ref, w_ref, o_ref, xn_ref):
    @pl.when(pl.program_id(1) == 0)
    def _():
        x = x_ref[...]
        ms = jnp.mean(x * x, axis=-1, keepdims=True)
        xn_ref[...] = (x * lax.rsqrt(ms + NORM_EPS) * g_ref[...]).astype(BF16)

    o_ref[...] = _dot(xn_ref[...], w_ref[...])


def norm_proj(x, g, w_all, layer, tm, tn):
    m, d = x.shape
    n = w_all.shape[2]
    assert m % tm == 0
    return pl.pallas_call(
        _norm_proj_kernel,
        out_shape=jax.ShapeDtypeStruct((m, n), F32),
        grid=(m // tm, pl.cdiv(n, tn)),
        in_specs=[pl.BlockSpec((tm, d), lambda i, j: (i, 0)),
                  pl.BlockSpec((1, d), lambda i, j: (0, 0)),
                  pl.BlockSpec((None, d, tn), lambda i, j: (layer, 0, j))],
        out_specs=pl.BlockSpec((tm, tn), lambda i, j: (i, j)),
        scratch_shapes=[pltpu.VMEM((tm, d), BF16)],
        compiler_params=_cparams(("arbitrary", "arbitrary")),
        name="norm_proj",
    )(x, g.reshape(1, d), w_all)


OUT_KC = 512


def _out_proj_kernel(ya_ref, yb_ref, *rest, ka, nchunk):
    w_refs, r_ref, o_ref = rest[:nchunk], rest[nchunk], rest[nchunk + 1]
    acc = r_ref[...]
    for k in range(nchunk):
        lo = k * OUT_KC
        y = ya_ref[:, lo:lo + OUT_KC] if lo < ka else yb_ref[:, lo - ka:lo - ka + OUT_KC]
        acc = acc + _dot(y, w_refs[k][...])
    o_ref[...] = acc


def out_proj(ya, yb, w_all, layer, resid, tm, tn):
    m, ka = ya.shape
    kb = yb.shape[1]
    n = w_all.shape[2]
    assert m % tm == 0 and n % tn == 0 and ka % OUT_KC == 0 and kb % OUT_KC == 0
    nchunk = (ka + kb) // OUT_KC
    wspec = lambda k: pl.BlockSpec((None, OUT_KC, tn), lambda i, j: (layer, k, j))
    return pl.pallas_call(
        functools.partial(_out_proj_kernel, ka=ka, nchunk=nchunk),
        out_shape=jax.ShapeDtypeStruct((m, n), F32),
        grid=(m // tm, n // tn),
        in_specs=[pl.BlockSpec((tm, ka), lambda i, j: (i, 0)),
                  pl.BlockSpec((tm, kb), lambda i, j: (i, 0))]
                 + [wspec(k) for k in range(nchunk)]
                 + [pl.BlockSpec((tm, tn), lambda i, j: (i, j))],
        out_specs=pl.BlockSpec((tm, tn), lambda i, j: (i, j)),
        compiler_params=_cparams(("arbitrary", "arbitrary")),
        name="out_proj",
    )(ya, yb, *([w_all] * nchunk), resid)


def _final_norm_kernel(x_ref, g_ref, o_ref):
    x = x_ref[...]
    ms = jnp.mean(x * x, axis=-1, keepdims=True)
    o_ref[...] = x * lax.rsqrt(ms + NORM_EPS) * g_ref[...]


def final_norm(x, g, tm):
    m, d = x.shape
    return pl.pallas_call(
        _final_norm_kernel,
        out_shape=jax.ShapeDtypeStruct((m, d), F32),
        grid=(m // tm,),
        in_specs=[pl.BlockSpec((tm, d), lambda i: (i, 0)), pl.BlockSpec((1, d), lambda i: (0, 0))],
        out_specs=pl.BlockSpec((tm, d), lambda i: (i, 0)),
        compiler_params=_cparams(("arbitrary",)),
        name="final_norm",
    )(x, g.reshape(1, d))


def _lru_gates(xs, wr, br, wi, bi, sp):
    xb = xs.astype(BF16)
    r = jax.nn.sigmoid(_dot(xb, wr) + br)
    ig = jax.nn.sigmoid(_dot(xb, wi) + bi)
    log_a = -LRU_C * r * sp
    a = jnp.exp(log_a)
    u = jnp.sqrt(-jnp.tanh(log_a) * (a * a + 1.0)) * (ig * xs)
    return a, u


def _softplus(z):
    return jnp.maximum(z, 0.0) + jnp.log1p(jnp.exp(-jnp.abs(z)))


def _lru_prompt_kernel(xa_ref, ga_ref, cw_ref, cb_ref, wr_ref, br_ref, wi_ref, bi_ref, lam_ref,
                       ya_ref, hl_ref, h_sc, halo_sc, *, tt, nblk):
    i = pl.program_id(1)

    @pl.when(i == 0)
    def _():
        h_sc[...] = jnp.zeros_like(h_sc)
        halo_sc[...] = jnp.zeros_like(halo_sc)

    x = xa_ref[...]
    xe = jnp.concatenate([halo_sc[...], x], axis=0)
    halo_sc[...] = x[tt - SUBLANES:, :]
    cw = cw_ref[...]
    y = cb_ref[...] + xe[SUBLANES:, :] * cw[CONV_W - 1:CONV_W, :]
    for k in range(CONV_W - 1):
        y = y + pltpu.roll(xe, CONV_W - 1 - k, 0)[SUBLANES:, :] * cw[k:k + 1, :]
    sp = _softplus(-lam_ref[...])
    rid = lax.broadcasted_iota(jnp.int32, (tt, LANES), 0) & (SUBLANES - 1)
    for n in range(nblk):
        sl = slice(n * LANES, (n + 1) * LANES)
        a, u = _lru_gates(y[:, sl], wr_ref[n], br_ref[:, sl], wi_ref[n], bi_ref[:, sl], sp[:, sl])
        for s in (1, 2, 4):
            keep = rid >= s
            u = jnp.where(keep, a * pltpu.roll(u, s, 0) + u, u)
            a = jnp.where(keep, a * pltpu.roll(a, s, 0), a)
        h = h_sc[:, sl]
        outs = []
        for g in range(tt // SUBLANES):
            rows = slice(g * SUBLANES, (g + 1) * SUBLANES)
            hg = u[rows, :] + a[rows, :] * h
            outs.append(hg)
            h = hg[SUBLANES - 1:, :]
        h_sc[:, sl] = h
        hs = jnp.concatenate(outs, axis=0)
        ya_ref[:, sl] = (hs * _silu(ga_ref[:, sl])).astype(BF16)
    hl_ref[0] = h_sc[...]


def lru_prompt(z, bsz, t, d_a, cw, cb, wr, br, wi, bi, lam, tt=256):
    nt = t // tt
    nblk = d_a // LANES
    row = lambda b, i: (b * nt + i, 0)
    vec = pl.BlockSpec((1, d_a), lambda b, i: (0, 0))
    wspec = pl.BlockSpec((nblk, LANES, LANES), lambda b, i: (0, 0, 0))
    return pl.pallas_call(
        functools.partial(_lru_prompt_kernel, tt=tt, nblk=nblk),
        out_shape=(jax.ShapeDtypeStruct((bsz * t, d_a), BF16), jax.ShapeDtypeStruct((bsz, 1, d_a), F32)),
        grid=(bsz, nt),
        in_specs=[pl.BlockSpec((tt, d_a), row),
                  pl.BlockSpec((tt, d_a), lambda b, i: (b * nt + i, 1)),
                  pl.BlockSpec((CONV_W, d_a), lambda b, i: (0, 0)), vec, wspec, vec, wspec, vec, vec],
        out_specs=(pl.BlockSpec((tt, d_a), row), pl.BlockSpec((1, 1, d_a), lambda b, i: (b, 0, 0))),
        scratch_shapes=[pltpu.VMEM((1, d_a), F32), pltpu.VMEM((SUBLANES, d_a), F32)],
        compiler_params=_cparams(("arbitrary", "arbitrary")),
        name="lru_prompt",
    )(z, z, cw, cb.reshape(1, d_a), wr, br.reshape(1, d_a), wi, bi.reshape(1, d_a), lam.reshape(1, d_a))


def _lru_sample_kernel(x_ref, ga_ref, c0_ref, h0_ref, cw_ref, cb_ref, wr_ref, br_ref, wi_ref, bi_ref, lam_ref,
                       ya_ref, hl_ref, *, ts, nblk):
    cw = cw_ref[...]
    sp = _softplus(-lam_ref[...])
    xe = [c0_ref[k] for k in range(CONV_W - 1)] + [x_ref[t] for t in range(ts)]
    h = h0_ref[...]
    for t in range(ts):
        y = cb_ref[...]
        for k in range(CONV_W):
            y = y + xe[t + k] * cw[k:k + 1, :]
        a_l, u_l = [], []
        for n in range(nblk):
            sl = slice(n * LANES, (n + 1) * LANES)
            a, u = _lru_gates(y[:, sl], wr_ref[n], br_ref[:, sl], wi_ref[n], bi_ref[:, sl], sp[:, sl])
            a_l.append(a)
            u_l.append(u)
        h = jnp.concatenate(a_l, axis=1) * h + jnp.concatenate(u_l, axis=1)
        ya_ref[t] = (h * _silu(ga_ref[t])).astype(BF16)
    for t in range(ts, SROWS):
        ya_ref[t] = jnp.zeros(ya_ref.shape[1:], BF16)
    hl_ref[...] = h


def lru_sample(xa_t, ga_t, conv_t, h0, cw, cb, wr, br, wi, bi, lam, ts):
    _, bsz, d_a = xa_t.shape
    nblk = d_a // LANES
    return pl.pallas_call(
        functools.partial(_lru_sample_kernel, ts=ts, nblk=nblk),
        out_shape=(jax.ShapeDtypeStruct((SROWS, bsz, d_a), BF16), jax.ShapeDtypeStruct((bsz, d_a), F32)),
        compiler_params=pltpu.CompilerParams(vmem_limit_bytes=VMEM_LIMIT),
        name="lru_sample",
    )(xa_t, ga_t, conv_t, h0, cw, cb.reshape(1, d_a), wr, br.reshape(1, d_a), wi, bi.reshape(1, d_a),
      lam.reshape(1, d_a))


def _swa_prompt_kernel(sink_ref, q_ref, kp_ref, kc_ref, vp_ref, vc_ref, gb0_ref, gb1_ref, cq_ref, sq_ref, cp_ref, sp_ref,
                       yb_ref, kout_ref, *, nq, gq, tq):
    i = pl.program_id(1)
    cq = cq_ref[...]
    sq = sq_ref[...]
    cp = cp_ref[...]
    sp = sp_ref[...]
    r = lax.broadcasted_iota(jnp.int32, (tq, 2 * tq), 0)
    c = lax.broadcasted_iota(jnp.int32, (tq, 2 * tq), 1)
    valid = (c > r) & (c <= r + WIN_B) & ((c >= tq) | (i > 0))
    for kv in range(KVH_B):
        hs = slice(kv * HEAD_DIM, (kv + 1) * HEAD_DIM)
        k_cur = _rope(kc_ref[:, hs], cq, sq)
        k_prev = _rope(kp_ref[:, hs], cp, sp)

        @pl.when(i == nq - 1)
        def _():
            kout_ref[0, kv] = k_cur

        k = jnp.concatenate([k_prev, k_cur], axis=0).astype(BF16)
        v = jnp.concatenate([vp_ref[:, hs], vc_ref[:, hs]], axis=0).astype(BF16)
        gb_ref = (gb0_ref, gb1_ref)[kv]
        for g in range(gq):
            hd = kv * gq + g
            qg = _rope(q_ref[:, hd * HEAD_DIM:(hd + 1) * HEAD_DIM], cq, sq).astype(BF16)
            s = _dot_nt(qg, k) * ATTN_SCALE
            sink = sink_ref[hd]
            m = jnp.maximum(jnp.max(jnp.where(valid, s, NEG), axis=1, keepdims=True), sink)
            p = jnp.where(valid, jnp.exp(s - m), 0.0)
            den = jnp.sum(p, axis=1, keepdims=True) + jnp.exp(sink - m)
            p = p * (1.0 / jnp.maximum(den, DENOM_FLOOR))
            o = _dot(p.astype(BF16), v)
            gate = _silu(gb_ref[:, g * HEAD_DIM:(g + 1) * HEAD_DIM])
            yb_ref[:, hd * HEAD_DIM:(hd + 1) * HEAD_DIM] = (o * gate).astype(BF16)


def swa_prompt(z, sinks, ctab, stab, bsz, t, q_off, k_off, v_off, g_off, n_heads):
    tq = WIN_B
    nq = t // tq
    gq = n_heads // KVH_B
    wq = gq * HEAD_DIM
    wall = n_heads * HEAD_DIM
    kvw = KVH_B * HEAD_DIM
    assert KVH_B == 2 and q_off % wall == 0 and g_off % wq == 0 and k_off % kvw == 0 and v_off % kvw == 0
    prev = lambda i: jnp.maximum(i - 1, 0)
    in_specs = [
        pl.BlockSpec(memory_space=pltpu.SMEM),
        pl.BlockSpec((tq, wall), lambda b, i: (b * nq + i, q_off // wall)),
        pl.BlockSpec((tq, kvw), lambda b, i: (b * nq + prev(i), k_off // kvw)),
        pl.BlockSpec((tq, kvw), lambda b, i: (b * nq + i, k_off // kvw)),
        pl.BlockSpec((tq, kvw), lambda b, i: (b * nq + prev(i), v_off // kvw)),
        pl.BlockSpec((tq, kvw), lambda b, i: (b * nq + i, v_off // kvw)),
        pl.BlockSpec((tq, wq), lambda b, i: (b * nq + i, g_off // wq)),
        pl.BlockSpec((tq, wq), lambda b, i: (b * nq + i, g_off // wq + 1)),
        pl.BlockSpec((tq, LANES), lambda b, i: (i, 0)),
        pl.BlockSpec((tq, LANES), lambda b, i: (i, 0)),
        pl.BlockSpec((tq, LANES), lambda b, i: (prev(i), 0)),
        pl.BlockSpec((tq, LANES), lambda b, i: (prev(i), 0)),
    ]
    return pl.pallas_call(
        functools.partial(_swa_prompt_kernel, nq=nq, gq=gq, tq=tq),
        out_shape=(jax.ShapeDtypeStruct((bsz * t, wall), BF16),
                   jax.ShapeDtypeStruct((bsz, KVH_B, tq, HEAD_DIM), F32)),
        grid=(bsz, nq),
        in_specs=in_specs,
        out_specs=(pl.BlockSpec((tq, wall), lambda b, i: (b * nq + i, 0)),
                   pl.BlockSpec((1, KVH_B, tq, HEAD_DIM), lambda b, i: (b, 0, 0, 0))),
        compiler_params=_cparams(("arbitrary", "arbitrary")),
        name="swa_prompt",
    )(sinks, z, z, z, z, z, z, z, ctab, stab, ctab, stab)


def _swa_sample_kernel(sink_ref, z_ref, kc_ref, vc_ref, c_ref, s_ref, yb_ref, kn_ref, *,
                       gq, q_off, k_off, v_off, g_off, ts, wlen, bb):
    c = c_ref[...]
    s = s_ref[...]
    qi = lax.broadcasted_iota(jnp.int32, (SROWS, wlen + SROWS), 0)
    ci = lax.broadcasted_iota(jnp.int32, (SROWS, wlen + SROWS), 1)
    diff = jnp.where(ci < wlen, qi + wlen - ci, qi - (ci - wlen))
    valid = (diff >= 0) & (diff < WIN_B)
    valid_g = jnp.concatenate([valid] * gq, axis=0)
    for bi in range(bb):
        for h in range(KVH_B):
            ks = slice(k_off + h * HEAD_DIM, k_off + (h + 1) * HEAD_DIM)
            vs = slice(v_off + h * HEAD_DIM, v_off + (h + 1) * HEAD_DIM)
            hs = slice(h * HEAD_DIM, (h + 1) * HEAD_DIM)
            k_new = _rope(z_ref[bi, :, ks], c, s)
            kn_ref[bi, :, hs] = k_new
            k = jnp.concatenate([kc_ref[bi, pl.ds(h, wlen, stride=KVH_B), :], k_new], axis=0).astype(BF16)
            v = jnp.concatenate([vc_ref[bi, pl.ds(h, wlen, stride=KVH_B), :], z_ref[bi, :, vs]], axis=0).astype(BF16)
            qs = jnp.concatenate(
                [_rope(z_ref[bi, :, q_off + (h * gq + g) * HEAD_DIM:q_off + (h * gq + g + 1) * HEAD_DIM], c, s)
                 for g in range(gq)], axis=0).astype(BF16)
            sink = jnp.concatenate([jnp.full((SROWS, 1), sink_ref[h * gq + g], F32) for g in range(gq)], axis=0)
            sc = _dot_nt(qs, k) * ATTN_SCALE
            m = jnp.maximum(jnp.max(jnp.where(valid_g, sc, NEG), axis=1, keepdims=True), sink)
            p = jnp.where(valid_g, jnp.exp(sc - m), 0.0)
            den = jnp.sum(p, axis=1, keepdims=True) + jnp.exp(sink - m)
            p = p * (1.0 / jnp.maximum(den, DENOM_FLOOR))
            o = _dot(p.astype(BF16), v)
            for g in range(gq):
                hd = h * gq + g
                gate = _silu(z_ref[bi, :, g_off + hd * HEAD_DIM:g_off + (hd + 1) * HEAD_DIM])
                yb_ref[bi, :, hd * HEAD_DIM:(hd + 1) * HEAD_DIM] = (o[g * SROWS:(g + 1) * SROWS, :] * gate).astype(BF16)


def swa_sample(z3, sinks, cache_k, cache_v, layer, ctab, stab, q_off, k_off, v_off, g_off, n_heads, ts, bb=4):
    bsz, _, p = z3.shape
    wlen = cache_k.shape[1] // KVH_B
    kvw = KVH_B * HEAD_DIM
    assert bsz % bb == 0
    return pl.pallas_call(
        functools.partial(_swa_sample_kernel, gq=n_heads // KVH_B, q_off=q_off, k_off=k_off, v_off=v_off,
                          g_off=g_off, ts=ts, wlen=wlen, bb=bb),
        out_shape=(jax.ShapeDtypeStruct((bsz, SROWS, n_heads * HEAD_DIM), BF16),
                   jax.ShapeDtypeStruct((bsz, SROWS, kvw), F32)),
        grid=(bsz // bb,),
        in_specs=[pl.BlockSpec(memory_space=pltpu.SMEM),
                  pl.BlockSpec((bb, SROWS, p), lambda b: (b, 0, 0)),
                  pl.BlockSpec((bb, wlen * KVH_B, HEAD_DIM), lambda b: (layer * (bsz // bb) + b, 0, 0)),
                  pl.BlockSpec((bb, wlen * KVH_B, HEAD_DIM), lambda b: (layer * (bsz // bb) + b, 0, 0)),
                  pl.BlockSpec((SROWS, LANES), lambda b: (0, 0)),
                  pl.BlockSpec((SROWS, LANES), lambda b: (0, 0))],
        out_specs=(pl.BlockSpec((bb, SROWS, n_heads * HEAD_DIM), lambda b: (b, 0, 0)),
                   pl.BlockSpec((bb, SROWS, kvw), lambda b: (b, 0, 0))),
        compiler_params=_cparams(("arbitrary",)),
        name="swa_sample",
    )(sinks, z3, cache_k, cache_v, ctab, stab)


def _pool_prompt_kernel(x_ref, gc_ref, w_ref, sc_ref, y_ref, halo_sc, *, tt, ngrp):
    i = pl.program_id(1)

    @pl.when(i == 0)
    def _():
        halo_sc[...] = jnp.zeros_like(halo_sc)

    x = x_ref[...]
    xe = jnp.concatenate([halo_sc[...], x], axis=0)
    halo_sc[...] = x[tt - POOL_MAX:, :]
    pos1 = (i * tt + 1 + lax.broadcasted_iota(jnp.int32, (tt, LANES), 0)).astype(F32)
    for g in range(ngrp):
        sl = slice(g * LANES, (g + 1) * LANES)
        w = POOL_WINDOWS[g]
        s = xe[:, sl]
        step = 1
        while step < w:
            s = s + pltpu.roll(s, step, 0)
            step *= 2
        pooled = s[POOL_MAX:, :] / jnp.minimum(float(w), pos1) - x[:, sl]
        y = _dot(pooled.astype(BF16), w_ref[g]) * sc_ref[:, sl]
        y_ref[:, sl] = (y * _silu(gc_ref[:, sl])).astype(BF16)


def pool_prompt(z, bsz, t, d_c, x_off, g_off, w_pool, scale, tt=256):
    nt = t // tt
    ngrp = d_c // LANES
    assert x_off % d_c == 0 and g_off % d_c == 0
    return pl.pallas_call(
        functools.partial(_pool_prompt_kernel, tt=tt, ngrp=ngrp),
        out_shape=jax.ShapeDtypeStruct((bsz * t, d_c), BF16),
        grid=(bsz, nt),
        in_specs=[pl.BlockSpec((tt, d_c), lambda b, i: (b * nt + i, x_off // d_c)),
                  pl.BlockSpec((tt, d_c), lambda b, i: (b * nt + i, g_off // d_c)),
                  pl.BlockSpec((ngrp, LANES, LANES), lambda b, i: (0, 0, 0)),
                  pl.BlockSpec((1, d_c), lambda b, i: (0, 0))],
        out_specs=pl.BlockSpec((tt, d_c), lambda b, i: (b * nt + i, 0)),
        scratch_shapes=[pltpu.VMEM((POOL_MAX, d_c), F32)],
        compiler_params=_cparams(("arbitrary", "arbitrary")),
        name="pool_prompt",
    )(z, z, w_pool, scale.reshape(1, d_c))


def _pool_sample_kernel(x_ref, gc_ref, buf_ref, w_ref, sc_ref, y_ref, *, ts, ngrp, start_pos):
    nbuf = POOL_MAX - 1
    xe = [buf_ref[k] for k in range(nbuf)] + [x_ref[t] for t in range(ts)]
    for t in range(ts):
        cols = []
        for g in range(ngrp):
            sl = slice(g * LANES, (g + 1) * LANES)
            w = POOL_WINDOWS[g]
            s = xe[nbuf + t][:, sl]
            for k in range(1, w):
                s = s + xe[nbuf + t - k][:, sl]
            pooled = s / min(float(w), float(start_pos + t + 1)) - xe[nbuf + t][:, sl]
            cols.append(_dot(pooled.astype(BF16), w_ref[g]))
        y = jnp.concatenate(cols, axis=1) * sc_ref[...]
        y_ref[t] = (y * _silu(gc_ref[t])).astype(BF16)
    for t in range(ts, SROWS):
        y_ref[t] = jnp.zeros(y_ref.shape[1:], BF16)


def pool_sample(xc_t, gc_t, buf_t, w_pool, scale, ts, start_pos):
    _, bsz, d_c = xc_t.shape
    return pl.pallas_call(
        functools.partial(_pool_sample_kernel, ts=ts, ngrp=d_c // LANES, start_pos=start_pos),
        out_shape=jax.ShapeDtypeStruct((SROWS, bsz, d_c), BF16),
        compiler_params=pltpu.CompilerParams(vmem_limit_bytes=VMEM_LIMIT),
        name="pool_sample",
    )(xc_t, gc_t, buf_t, w_pool, scale.reshape(1, d_c))


def _nsa_prep_kernel(q0_ref, q1_ref, q2_ref, kc_ref, vc_ref, ks_ref, vs_ref, kw_ref, vw_ref, c_ref, s_ref,
                     cwk_ref, cwv_ref, qr_ref, kcr_ref, vcf_ref, ksr_ref, vsf_ref, kwr_ref, vwf_ref, kvb_ref,
                     kcmp_ref, vcmp_ref, *, tt, nh):
    c = c_ref[...]
    s = s_ref[...]
    hpb = nh // 3
    for h in range(nh):
        q_ref = (q0_ref, q1_ref, q2_ref)[h // hpb]
        src = slice((h % hpb) * HEAD_DIM, (h % hpb + 1) * HEAD_DIM)
        qr_ref[:, h * HEAD_DIM:(h + 1) * HEAD_DIM] = _rope(q_ref[:, src], c, s).astype(BF16)
    nblk = tt // CMP_BLOCK
    kvw = KVH_D * HEAD_DIM
    for h in range(KVH_D):
        sl = slice(h * HEAD_DIM, (h + 1) * HEAD_DIM)
        kcr = _rope(kc_ref[:, sl], c, s)
        ksr = _rope(ks_ref[:, sl], c, s)
        kwr = _rope(kw_ref[:, sl], c, s)
        for out_ref, val in ((kcr_ref, kcr), (vcf_ref, vc_ref[:, sl]), (ksr_ref, ksr), (vsf_ref, vs_ref[:, sl]),
                             (kwr_ref, kwr), (vwf_ref, vw_ref[:, sl])):
            out_ref[pl.ds(h, tt, stride=KVH_D), :] = val
        for seg, val in enumerate((ksr, vs_ref[:, sl], kwr, vw_ref[:, sl])):
            kvb_ref[:, seg * kvw + h * HEAD_DIM:seg * kvw + (h + 1) * HEAD_DIM] = val.astype(BF16)
        kcmp_ref[0, :, sl] = jnp.sum(kcr.reshape(nblk, CMP_BLOCK, HEAD_DIM) * cwk_ref[:, sl][None], axis=1)
        vcmp_ref[0, :, sl] = jnp.sum(vc_ref[:, sl].reshape(nblk, CMP_BLOCK, HEAD_DIM) * cwv_ref[:, sl][None], axis=1)


def nsa_prep(z, ctab, stab, cwk2, cwv2, bsz, t, offs, n_heads, tt=512):
    nt = t // tt
    qw = n_heads * HEAD_DIM
    kvw = KVH_D * HEAD_DIM
    qb = qw // 3
    assert n_heads % 3 == 0 and offs["q"] % qb == 0
    assert all(offs[k] % kvw == 0 for k in ("kc", "vc", "ks", "vs", "kw", "vw"))
    qspec = lambda k: pl.BlockSpec((tt, qb), lambda b, i: (b * nt + i, offs["q"] // qb + k))
    row = lambda b, i: (b * nt + i, 0)
    kvspec = lambda name: pl.BlockSpec((tt, kvw), lambda b, i: (b * nt + i, offs[name] // kvw))
    return pl.pallas_call(
        functools.partial(_nsa_prep_kernel, tt=tt, nh=n_heads),
        out_shape=(jax.ShapeDtypeStruct((bsz * t, qw), BF16),)
                  + (jax.ShapeDtypeStruct((bsz * t * KVH_D, HEAD_DIM), F32),) * 6
                  + (jax.ShapeDtypeStruct((bsz * t, 4 * kvw), BF16),
                   jax.ShapeDtypeStruct((bsz, t // CMP_BLOCK, kvw), F32),
                   jax.ShapeDtypeStruct((bsz, t // CMP_BLOCK, kvw), F32)),
        grid=(bsz, nt),
        in_specs=[qspec(0), qspec(1), qspec(2),
                  kvspec("kc"), kvspec("vc"), kvspec("ks"), kvspec("vs"), kvspec("kw"), kvspec("vw"),
                  pl.BlockSpec((tt, LANES), lambda b, i: (i, 0)),
                  pl.BlockSpec((tt, LANES), lambda b, i: (i, 0)),
                  pl.BlockSpec((CMP_BLOCK, kvw), lambda b, i: (0, 0)),
                  pl.BlockSpec((CMP_BLOCK, kvw), lambda b, i: (0, 0))],
        out_specs=(pl.BlockSpec((tt, qw), row),) + (pl.BlockSpec((tt * KVH_D, HEAD_DIM), row),) * 6
                  + (pl.BlockSpec((tt, 4 * kvw), row),
                   pl.BlockSpec((1, tt // CMP_BLOCK, kvw), lambda b, i: (b, i, 0)),
                   pl.BlockSpec((1, tt // CMP_BLOCK, kvw), lambda b, i: (b, i, 0))),
        compiler_params=_cparams(("arbitrary", "arbitrary")),
        name="nsa_prep",
    )(z, z, z, z, z, z, z, z, z, ctab, stab, cwk2, cwv2)


def _masked_softmax_rows(s, valid):
    m = jnp.max(jnp.where(valid, s, NEG), axis=1, keepdims=True)
    p = jnp.where(valid, jnp.exp(s - m), 0.0)
    den = jnp.sum(p, axis=1, keepdims=True)
    return p * (1.0 / jnp.maximum(den, DENOM_FLOOR))


def _nsa_prompt_kernel(q_ref, kcmp_ref, vcmp_ref, ks_ref, vs_ref, kw_ref, vw_ref, gd0_ref, gd1_ref, gd2_ref,
                       gl_ref, gbias_ref, *rest, t_len, tq, tk, gq, n_heads, stream):
    if stream is None:
        o_ref, s_sc, mrun_sc, m_sc, lrun_sc, acc_sc, score_sc = rest
    else:
        (pt_ref, kpool, vpool, cwk_ref, cwv_ref, o_ref, kcp_ref, vcp_ref,
         s_sc, mrun_sc, m_sc, lrun_sc, acc_sc, score_sc, kbuf, vbuf, sem, part_sc) = rest
    kv = pl.program_id(1)
    i = pl.program_id(2)
    if stream is not None:
        ppc, cps, page_base, npages = stream
        step = (pl.program_id(0) * pl.num_programs(1) + kv) * pl.num_programs(2) + i
        nsteps = pl.num_programs(0) * pl.num_programs(1) * pl.num_programs(2)
        slot = lax.rem(step, 2)

        def chunk_copies(st, sl):
            seq = lax.div(st, cps)
            chunk = lax.rem(st, cps)
            out = []
            for p in range(ppc):
                page = page_base + pt_ref[seq * npages + chunk * ppc + p]
                out.append(pltpu.make_async_copy(kpool.at[page], kbuf.at[sl, p], sem.at[0, sl]))
                out.append(pltpu.make_async_copy(vpool.at[page], vbuf.at[sl, p], sem.at[1, sl]))
            return out

        @pl.when(step == 0)
        def _():
            for cp in chunk_copies(step, slot):
                cp.start()

        @pl.when(step + 1 < nsteps)
        def _():
            for cp in chunk_copies(step + 1, 1 - slot):
                cp.start()
    nb = t_len // CMP_BLOCK
    q0 = i * tq
    bpt = tk // CMP_BLOCK

    rows = gq * tq
    qs = jnp.concatenate([q_ref[:, g * HEAD_DIM:(g + 1) * HEAD_DIM] for g in range(gq)], axis=0)
    kc = kcmp_ref[0].astype(BF16)
    vc = vcmp_ref[0].astype(BF16)

    blk_r = lax.broadcasted_iota(jnp.int32, (rows, nb), 1)
    qp_r = q0 + (lax.broadcasted_iota(jnp.int32, (rows, nb), 0) & (tq - 1))
    ok_r = (blk_r + 1) * CMP_BLOCK - 1 <= qp_r
    p_cmp = _masked_softmax_rows(_dot_nt(qs, kc) * ATTN_SCALE, ok_r)
    o_cmp = _dot(p_cmp.astype(BF16), vc)
    blk_a = lax.broadcasted_iota(jnp.int32, (nb, rows), 0)
    qp_a = q0 + (lax.broadcasted_iota(jnp.int32, (nb, rows), 1) & (tq - 1))
    ok_a = (blk_a + 1) * CMP_BLOCK - 1 <= qp_a
    st = _dot_nt(kc, qs) * ATTN_SCALE
    mt = jnp.max(jnp.where(ok_a, st, NEG), axis=0, keepdims=True)
    pt = jnp.where(ok_a, jnp.exp(st - mt), 0.0)
    dt = jnp.sum(pt, axis=0, keepdims=True)
    pt = pt * (1.0 / jnp.maximum(dt, DENOM_FLOOR))
    imp = pt[:, 0:tq]
    for g in range(1, gq):
        imp = imp + pt[:, g * tq:(g + 1) * tq]
    blk_c = lax.broadcasted_iota(jnp.int32, (nb, tq), 0)
    qp_c = q0 + lax.broadcasted_iota(jnp.int32, (nb, tq), 1)

    cur = jnp.right_shift(qp_c, CMP_SHIFT)
    forced = (blk_c == 0) | (blk_c == cur) | (blk_c == cur - 1)
    score = jnp.where(blk_c > cur, -1.0, jnp.where(forced, FORCE_SCORE, imp))
    score_sc[...] = score
    n_vis = jnp.minimum((q0 + tq - 1) // CMP_BLOCK + 1, nb)

    def rank_body(k, cnt):
        rk = score_sc[pl.ds(k, 1), :]
        tie = jnp.where(blk_c > k, 1.0, 0.0)
        return cnt + jnp.where(rk > score, 1.0, jnp.where(rk == score, tie, 0.0))

    cnt = lax.fori_loop(0, n_vis, rank_body, jnp.zeros((nb, tq), F32))
    sel_t = jnp.where((cnt < N_SEL) & (score >= 0.0), 1.0, 0.0).astype(BF16)
    eye = jnp.where(lax.broadcasted_iota(jnp.int32, (tq, tq), 0) == lax.broadcasted_iota(jnp.int32, (tq, tq), 1),
                    1.0, 0.0).astype(BF16)
    sel = _dot_nt(eye, sel_t).astype(BF16)

    n_kt = (q0 + tq + tk - 1) // tk
    qp_k = q0 + lax.broadcasted_iota(jnp.int32, (tq, tk), 0)
    col_k = lax.broadcasted_iota(jnp.int32, (tq, tk), 1)
    e_row = lax.broadcasted_iota(jnp.int32, (nb, tk), 0)
    e_col = jnp.right_shift(lax.broadcasted_iota(jnp.int32, (nb, tk), 1), CMP_SHIFT)
    nlt = tk // LANES
    mrun_sc[...] = jnp.full(mrun_sc.shape, NEG, F32)

    def score_pass(kt, carry):
        k0 = pl.multiple_of(kt * tk, tk)
        s = _dot_nt(qs, ks_ref[pl.ds(k0, tk), :])
        expand = jnp.where(e_row == kt * bpt + e_col, 1.0, 0.0).astype(BF16)
        ok = (_dot(sel, expand) > 0.5) & (k0 + col_k <= qp_k)
        bias = jnp.where(ok, 0.0, NEG)
        sb = s * ATTN_SCALE + jnp.concatenate([bias] * gq, axis=0)
        s_sc[kt] = sb
        mx = sb[:, 0:LANES]
        for c in range(1, nlt):
            mx = jnp.maximum(mx, sb[:, c * LANES:(c + 1) * LANES])
        mrun_sc[...] = jnp.maximum(mrun_sc[...], mx)
        return carry

    lax.fori_loop(0, n_kt, score_pass, 0)
    m_sc[...] = jnp.broadcast_to(jnp.max(mrun_sc[...], axis=1, keepdims=True), m_sc.shape)
    lrun_sc[...] = jnp.zeros(lrun_sc.shape, F32)
    acc_sc[...] = jnp.zeros(acc_sc.shape, F32)

    def value_pass(kt, carry):
        k0 = pl.multiple_of(kt * tk, tk)
        sb = s_sc[kt]
        mrep = m_sc[...]
        ps = [jnp.exp(sb[:, c * LANES:(c + 1) * LANES] - mrep) for c in range(nlt)]
        lsum = ps[0]
        for c in range(1, nlt):
            lsum = lsum + ps[c]
        lrun_sc[...] = lrun_sc[...] + lsum
        p = jnp.concatenate(ps, axis=1).astype(BF16)
        acc_sc[...] = acc_sc[...] + _dot(p, vs_ref[pl.ds(k0, tk), :])
        return carry

    lax.fori_loop(0, n_kt, value_pass, 0)
    o_sel = acc_sc[...] * (1.0 / jnp.sum(lrun_sc[...], axis=1, keepdims=True))

    nwb = (WIN_D + tq - 2) // tq + 1
    wlen = nwb * tq
    w0 = pl.multiple_of(jnp.maximum(i - (nwb - 1), 0) * tq, tq)
    dw = (q0 + lax.broadcasted_iota(jnp.int32, (tq, wlen), 0)) - (w0 + lax.broadcasted_iota(jnp.int32, (tq, wlen), 1))
    bias_w = jnp.where((dw >= 0) & (dw < WIN_D), 0.0, NEG)
    sw = _dot_nt(qs, kw_ref[pl.ds(w0, wlen), :]) * ATTN_SCALE + jnp.concatenate([bias_w] * gq, axis=0)
    pw = jnp.exp(sw - jnp.max(sw, axis=1, keepdims=True))
    lw = jnp.sum(pw, axis=1, keepdims=True)
    o_win = _dot(pw.astype(BF16), vw_ref[pl.ds(w0, wlen), :]) * (1.0 / lw)

    gate = jax.nn.sigmoid(gl_ref[...] + gbias_ref[...])
    hpb = gq // 3
    for g in range(gq):
        rs = slice(g * tq, (g + 1) * tq)
        gs = []
        for br in range(3):
            c0 = br * n_heads + g
            c1 = c0 + gq
            gs.append(jnp.where(kv == 0, gate[:, c0:c0 + 1], gate[:, c1:c1 + 1]))
        yd = gs[0] * o_cmp[rs, :] + gs[1] * o_sel[rs, :] + gs[2] * o_win[rs, :]
        gd_ref = (gd0_ref, gd1_ref, gd2_ref)[g // hpb]
        gd = gd_ref[:, (g % hpb) * HEAD_DIM:(g % hpb + 1) * HEAD_DIM]
        o_ref[:, g * HEAD_DIM:(g + 1) * HEAD_DIM] = (yd * _silu(gd)).astype(BF16)

    if stream is not None:
        for cp in chunk_copies(step, slot):
            cp.wait()
        _compress_pages(kbuf, slot, cwk_ref, kcp_ref, part_sc, ppc)
        _compress_pages(vbuf, slot, cwv_ref, vcp_ref, part_sc, ppc)


def nsa_prompt(z, qr, kcmp, vcmp, kvb, gbias, bsz, t, offs, n_heads, stream_args=None, tq=128, tk=512):
    nq = t // tq
    gq = n_heads // KVH_D
    wq = gq * HEAD_DIM
    nb = t // CMP_BLOCK
    gw = wq // 3
    assert gq % 3 == 0 and offs["gd"] % gw == 0 and offs["gl"] % LANES == 0 and t % tk == 0
    assert KVH_D == 2
    gdspec = lambda k: pl.BlockSpec((tq, gw), lambda b, h, i: (b * nq + i, offs["gd"] // gw + 3 * h + k))
    res = lambda seg: pl.BlockSpec((t, HEAD_DIM), lambda b, h, i: (b, seg * KVH_D + h))
    rows = gq * tq
    stat = pltpu.VMEM((rows, LANES), F32)
    in_specs = [pl.BlockSpec((tq, wq), lambda b, h, i: (b * nq + i, h)),
                pl.BlockSpec((1, nb, HEAD_DIM), lambda b, h, i: (b, 0, h)),
                pl.BlockSpec((1, nb, HEAD_DIM), lambda b, h, i: (b, 0, h)),
                res(0), res(1), res(2), res(3),
                gdspec(0), gdspec(1), gdspec(2),
                pl.BlockSpec((tq, LANES), lambda b, h, i: (b * nq + i, offs["gl"] // LANES)),
                pl.BlockSpec((1, LANES), lambda b, h, i: (0, 0))]
    out_shape = [jax.ShapeDtypeStruct((bsz * t, n_heads * HEAD_DIM), BF16)]
    out_specs = [pl.BlockSpec((tq, wq), lambda b, h, i: (b * nq + i, h))]
    scratch = [pltpu.VMEM((t // tk, rows, tk), F32), stat, stat, stat, stat, pltpu.VMEM((nb, tq), F32)]
    args = [qr, kcmp, vcmp, kvb, kvb, kvb, kvb, z, z, z, z, gbias]
    stream = None
    if stream_args is not None:
        pt_flat, kpool, vpool, cwk_page, cwv_page, n_seq, npages, page_base, ppc = stream_args
        cps = npages // ppc
        assert npages % ppc == 0 and n_seq * cps == bsz * KVH_D * nq
        prow = kpool.shape[1]
        bpp = PAGE_SIZE // CMP_BLOCK
        kvw = KVH_D * HEAD_DIM
        stream = (ppc, cps, page_base, npages)
        in_specs += [pl.BlockSpec(memory_space=pltpu.SMEM), pl.BlockSpec(memory_space=pl.ANY),
                     pl.BlockSpec(memory_space=pl.ANY),
                     pl.BlockSpec((prow, LANES), lambda b, h, i: (0, 0)),
                     pl.BlockSpec((prow, LANES), lambda b, h, i: (0, 0))]
        args += [pt_flat, kpool, vpool, cwk_page, cwv_page]

        def cmp_map(b, h, i):
            st = (b * KVH_D + h) * nq + i
            return (st // cps, st % cps, 0)

        out_shape += [jax.ShapeDtypeStruct((n_seq, npages * bpp, kvw), F32)] * 2
        out_specs += [pl.BlockSpec((1, ppc * bpp, kvw), cmp_map)] * 2
        scratch += [pltpu.VMEM((2, ppc, prow, LANES), F32), pltpu.VMEM((2, ppc, prow, LANES), F32),
                    pltpu.SemaphoreType.DMA((2, 2)), pltpu.VMEM((ppc * bpp * SUBLANES, LANES), F32)]
    out = pl.pallas_call(
        functools.partial(_nsa_prompt_kernel, t_len=t, tq=tq, tk=tk, gq=gq, n_heads=n_heads, stream=stream),
        out_shape=tuple(out_shape),
        grid=(bsz, KVH_D, nq),
        in_specs=in_specs,
        out_specs=tuple(out_specs),
        scratch_shapes=scratch,
        compiler_params=pltpu.CompilerParams(dimension_semantics=("arbitrary", "arbitrary", "arbitrary"),
                                             vmem_limit_bytes=NSA_VMEM_LIMIT),
        name="nsa_prompt",
    )(*args)
    return out if stream_args is not None else out[0]


def _compress_pages(buf, slot, w_ref, out_ref, part_sc, ppc):
    nblk = ppc * (PAGE_SIZE // CMP_BLOCK)
    vregs_per_blk = CMP_BLOCK * KVH_D // SUBLANES
    x = buf[slot] * w_ref[...][None]
    part_sc[...] = jnp.sum(x.reshape(nblk, vregs_per_blk, SUBLANES, LANES), axis=1).reshape(nblk * SUBLANES, LANES)
    for h in range(KVH_D):
        acc = part_sc[pl.ds(h, nblk, stride=SUBLANES), :]
        for k in range(1, SUBLANES // KVH_D):
            acc = acc + part_sc[pl.ds(k * KVH_D + h, nblk, stride=SUBLANES), :]
        out_ref[0, :, h * HEAD_DIM:(h + 1) * HEAD_DIM] = acc


def _cmp_stream_kernel(pt_ref, kpool, vpool, cwk_ref, cwv_ref, kcmp_ref, vcmp_ref, kbuf, vbuf, sem, part_sc, *,
                       ppc, nch, nsteps, page_base, npages):
    b = pl.program_id(0)
    c = pl.program_id(1)
    step = b * nch + c
    slot = lax.rem(step, 2)

    def copies(bb, cc, sl):
        out = []
        for p in range(ppc):
            page = page_base + pt_ref[bb * npages + cc * ppc + p]
            out.append(pltpu.make_async_copy(kpool.at[page], kbuf.at[sl, p], sem.at[0, sl]))
            out.append(pltpu.make_async_copy(vpool.at[page], vbuf.at[sl, p], sem.at[1, sl]))
        return out

    @pl.when(step == 0)
    def _():
        for cp in copies(0, 0, 0):
            cp.start()

    @pl.when(step + 1 < nsteps)
    def _():
        nxt = step + 1
        for cp in copies(nxt // nch, lax.rem(nxt, nch), 1 - slot):
            cp.start()

    for cp in copies(b, c, slot):
        cp.wait()

    _compress_pages(kbuf, slot, cwk_ref, kcmp_ref, part_sc, ppc)
    _compress_pages(vbuf, slot, cwv_ref, vcmp_ref, part_sc, ppc)


def cmp_stream(pt_flat, kpool, vpool, cwk_page, cwv_page, bsz, npages, page_base, ppc=16):
    prow = kpool.shape[1]
    width = KVH_D * HEAD_DIM
    nch = npages // ppc
    bpp = PAGE_SIZE // CMP_BLOCK
    grid_spec = pltpu.PrefetchScalarGridSpec(
        num_scalar_prefetch=1,
        grid=(bsz, nch),
        in_specs=[pl.BlockSpec(memory_space=pl.ANY), pl.BlockSpec(memory_space=pl.ANY),
                  pl.BlockSpec((prow, LANES), lambda b, c, pt: (0, 0)),
                  pl.BlockSpec((prow, LANES), lambda b, c, pt: (0, 0))],
        out_specs=(pl.BlockSpec((1, ppc * bpp, width), lambda b, c, pt: (b, c, 0)),) * 2,
        scratch_shapes=[pltpu.VMEM((2, ppc, prow, LANES), F32), pltpu.VMEM((2, ppc, prow, LANES), F32),
                        pltpu.SemaphoreType.DMA((2, 2)), pltpu.VMEM((ppc * bpp * SUBLANES, LANES), F32)],
    )
    return pl.pallas_call(
        functools.partial(_cmp_stream_kernel, ppc=ppc, nch=nch, nsteps=bsz * nch, page_base=page_base,
                          npages=npages),
        out_shape=(jax.ShapeDtypeStruct((bsz, npages * bpp, width), F32),) * 2,
        grid_spec=grid_spec,
        compiler_params=_cparams(("arbitrary", "arbitrary")),
        name="cmp_stream",
    )(pt_flat, kpool, vpool, cwk_page, cwv_page)


def _nsa_sample_cmp_kernel(z_ref, kcp_ref, vcp_ref, c_ref, s_ref, cwk_ref, cwv_ref,
                           ocmp_ref, imp_ref, knew_ref, *, offs, gq, ts, n_past, past, bb):
    c = c_ref[...]
    s = s_ref[...]
    nbp = n_past + SUBLANES
    blk_r = lax.broadcasted_iota(jnp.int32, (gq * SROWS, nbp), 1)
    qp_r = past + (lax.broadcasted_iota(jnp.int32, (gq * SROWS, nbp), 0) & (SROWS - 1))
    ok_r = (blk_r + 1) * CMP_BLOCK - 1 <= qp_r
    blk_c = lax.broadcasted_iota(jnp.int32, (nbp, LANES), 0)
    qp_c = past + (lax.broadcasted_iota(jnp.int32, (nbp, LANES), 1) & (SROWS - 1))
    ok_c = (blk_c + 1) * CMP_BLOCK - 1 <= qp_c
    row8 = lax.broadcasted_iota(jnp.int32, (SROWS, HEAD_DIM), 0)
    is_new = row8 < ts
    qpad = jnp.zeros((LANES - gq * SROWS, HEAD_DIM), BF16)
    kvw = KVH_D * HEAD_DIM
    for bi in range(bb):
        for h in range(KVH_D):
            hs = slice(h * HEAD_DIM, (h + 1) * HEAD_DIM)
            kcr = _rope(z_ref[bi, :, offs["kc"] + h * HEAD_DIM:offs["kc"] + (h + 1) * HEAD_DIM], c, s)
            ksr = _rope(z_ref[bi, :, offs["ks"] + h * HEAD_DIM:offs["ks"] + (h + 1) * HEAD_DIM], c, s)
            kwr = _rope(z_ref[bi, :, offs["kw"] + h * HEAD_DIM:offs["kw"] + (h + 1) * HEAD_DIM], c, s)
            knew_ref[bi, :, h * HEAD_DIM:(h + 1) * HEAD_DIM] = kcr
            knew_ref[bi, :, kvw + h * HEAD_DIM:kvw + (h + 1) * HEAD_DIM] = ksr
            knew_ref[bi, :, 2 * kvw + h * HEAD_DIM:2 * kvw + (h + 1) * HEAD_DIM] = kwr
            vcn = z_ref[bi, :, offs["vc"] + h * HEAD_DIM:offs["vc"] + (h + 1) * HEAD_DIM]
            nk = jnp.sum(jnp.where(is_new, kcr * cwk_ref[0:SROWS, hs], 0.0), axis=0, keepdims=True)
            nv = jnp.sum(jnp.where(is_new, vcn * cwv_ref[0:SROWS, hs], 0.0), axis=0, keepdims=True)
            nk8 = jnp.where(row8 == 0, jnp.broadcast_to(nk, (SROWS, HEAD_DIM)), 0.0)
            nv8 = jnp.where(row8 == 0, jnp.broadcast_to(nv, (SROWS, HEAD_DIM)), 0.0)
            kall = jnp.concatenate([kcp_ref[bi, :, hs], nk8], axis=0).astype(BF16)
            vall = jnp.concatenate([vcp_ref[bi, :, hs], nv8], axis=0).astype(BF16)
            qs = jnp.concatenate(
                [_rope(z_ref[bi, :, offs["q"] + (h * gq + g) * HEAD_DIM:offs["q"] + (h * gq + g + 1) * HEAD_DIM], c, s)
                 for g in range(gq)], axis=0).astype(BF16)
            p = _masked_softmax_rows(_dot_nt(qs, kall) * ATTN_SCALE, ok_r)
            o = _dot(p.astype(BF16), vall)
            for g in range(gq):
                hd = h * gq + g
                ocmp_ref[bi, :, hd * HEAD_DIM:(hd + 1) * HEAD_DIM] = o[g * SROWS:(g + 1) * SROWS, :]
            st = _dot_nt(kall, jnp.concatenate([qs, qpad], axis=0)) * ATTN_SCALE
            mt = jnp.max(jnp.where(ok_c, st, NEG), axis=0, keepdims=True)
            pt = jnp.where(ok_c, jnp.exp(st - mt), 0.0)
            dt = jnp.sum(pt, axis=0, keepdims=True)
            pt = pt * (1.0 / jnp.maximum(dt, DENOM_FLOOR))
            imp = pt
            for g in range(1, gq):
                imp = imp + pltpu.roll(pt, LANES - g * SROWS, 1)
            imp_ref[bi, h] = imp


def nsa_sample_cmp(z3, kcp, vcp, ctab, stab, cwk2, cwv2, offs, n_heads, ts, past, bb=4):
    bsz, _, p = z3.shape
    n_past = kcp.shape[1]
    kvw = KVH_D * HEAD_DIM
    nbp = n_past + SUBLANES
    assert bsz % bb == 0
    return pl.pallas_call(
        functools.partial(_nsa_sample_cmp_kernel, offs=offs, gq=n_heads // KVH_D, ts=ts, n_past=n_past, past=past,
                          bb=bb),
        out_shape=(jax.ShapeDtypeStruct((bsz, SROWS, n_heads * HEAD_DIM), F32),
                   jax.ShapeDtypeStruct((bsz, KVH_D, nbp, LANES), F32),
                   jax.ShapeDtypeStruct((bsz, SROWS, 3 * kvw), F32)),
        grid=(bsz // bb,),
        in_specs=[pl.BlockSpec((bb, SROWS, p), lambda b: (b, 0, 0)),
                  pl.BlockSpec((bb, n_past, kvw), lambda b: (b, 0, 0)),
                  pl.BlockSpec((bb, n_past, kvw), lambda b: (b, 0, 0)),
                  pl.BlockSpec((SROWS, LANES), lambda b: (0, 0)),
                  pl.BlockSpec((SROWS, LANES), lambda b: (0, 0)),
                  pl.BlockSpec((CMP_BLOCK, kvw), lambda b: (0, 0)),
                  pl.BlockSpec((CMP_BLOCK, kvw), lambda b: (0, 0))],
        out_specs=(pl.BlockSpec((bb, SROWS, n_heads * HEAD_DIM), lambda b: (b, 0, 0)),
                   pl.BlockSpec((bb, KVH_D, nbp, LANES), lambda b: (b, 0, 0, 0)),
                   pl.BlockSpec((bb, SROWS, 3 * kvw), lambda b: (b, 0, 0))),
        compiler_params=_cparams(("arbitrary",)),
        name="nsa_sample_cmp",
    )(z3, kcp, vcp, ctab, stab, cwk2, cwv2)


def _select_sample_kernel(imp_ref, qpos_ref, idx_ref, score_sc, cnt_sc, *, nbp, nsel):
    shape = score_sc.shape
    blk = lax.broadcasted_iota(jnp.int32, shape, 0)
    cur = jnp.right_shift(qpos_ref[...], CMP_SHIFT)
    forced = (blk == 0) | (blk == cur) | (blk == cur - 1)
    score_sc[...] = jnp.where(blk > cur, -1.0, jnp.where(forced, FORCE_SCORE, imp_ref[...]))
    cnt_sc[...] = jnp.zeros(shape, F32)

    def body(k, carry):
        rk = score_sc[pl.ds(k, 1), :]
        sc = score_sc[...]
        tie = jnp.where(blk > k, 1.0, 0.0)
        cnt_sc[...] = cnt_sc[...] + jnp.where(rk > sc, 1.0, jnp.where(rk == sc, tie, 0.0))
        return carry

    lax.fori_loop(0, nbp, body, 0)
    cnt = cnt_sc[...]
    ok = score_sc[...] >= 0.0
    blk_f = blk.astype(F32)
    for r in range(nsel):
        picked = jnp.sum(jnp.where((cnt == float(r)) & ok, blk_f, 0.0), axis=0, keepdims=True)
        idx_ref[r:r + 1, :] = picked.astype(jnp.int32)


def select_sample(imp_t, qpos):
    nbp, nc = imp_t.shape
    return pl.pallas_call(
        functools.partial(_select_sample_kernel, nbp=nbp, nsel=N_SEL),
        out_shape=jax.ShapeDtypeStruct((N_SEL, nc), jnp.int32),
        scratch_shapes=[pltpu.VMEM((nbp, nc), F32), pltpu.VMEM((nbp, nc), F32)],
        compiler_params=pltpu.CompilerParams(vmem_limit_bytes=VMEM_LIMIT),
        name="select_sample",
    )(imp_t, qpos)


def _nsa_sample_sel_kernel(idx_ref, pt_ref, z_ref, ocmp_ref, knew_ref, wk_ref, wv_ref, c_ref, s_ref, gbias_ref,
                           kpool, vpool, y_ref, kbuf, vbuf, sem, *, offs, gq, n_heads, ts, n_past, past,
                           page_base, npages, wlen):
    b = pl.program_id(0)
    c = c_ref[...]
    s = s_ref[...]
    bpp = PAGE_SIZE // CMP_BLOCK
    kvw = KVH_D * HEAD_DIM
    blk_rows = CMP_BLOCK * KVH_D

    nslots = KVH_D * ts * N_SEL
    nseq = pl.num_programs(0)
    half = lax.rem(b, 2)

    def slot_copies(seq, hf, sidx):
        blk = idx_ref[seq * nslots + sidx]
        past_blk = jnp.minimum(blk, n_past - 1)
        page = page_base + pt_ref[seq * npages + lax.div(past_blk, bpp)]
        row0 = pl.multiple_of(lax.rem(past_blk, bpp) * blk_rows, blk_rows)
        return (pltpu.make_async_copy(kpool.at[page, pl.ds(row0, blk_rows), :], kbuf.at[hf, sidx], sem.at[0, hf]),
                pltpu.make_async_copy(vpool.at[page, pl.ds(row0, blk_rows), :], vbuf.at[hf, sidx], sem.at[1, hf]))

    def issue_all(seq, hf):
        def issue(sidx, carry):
            for cp in slot_copies(seq, hf, sidx):
                cp.start()
            return carry

        lax.fori_loop(0, nslots, issue, 0)

    @pl.when(b == 0)
    def _():
        issue_all(b, half)

    @pl.when(b + 1 < nseq)
    def _():
        issue_all(b + 1, 1 - half)

    row8 = lax.broadcasted_iota(jnp.int32, (SROWS, HEAD_DIM), 0)
    kpad = jnp.zeros((LANES - SROWS, HEAD_DIM), F32)
    qi = lax.broadcasted_iota(jnp.int32, (SROWS, wlen + LANES), 0)
    ci = lax.broadcasted_iota(jnp.int32, (SROWS, wlen + LANES), 1)
    dw = jnp.where(ci < wlen, qi + wlen - ci, qi - (ci - wlen))
    ok_w = (dw >= 0) & (dw < WIN_D) & ((ci < wlen) | (ci - wlen < ts))
    ok_wg = jnp.concatenate([ok_w] * gq, axis=0)
    qr = []
    o_win = []
    for h in range(KVH_D):
        hs = slice(h * HEAD_DIM, (h + 1) * HEAD_DIM)
        kwn = knew_ref[0, :, 2 * kvw + h * HEAD_DIM:2 * kvw + (h + 1) * HEAD_DIM]
        vwn = z_ref[0, :, offs["vw"] + h * HEAD_DIM:offs["vw"] + (h + 1) * HEAD_DIM]
        kw = jnp.concatenate([wk_ref[0, pl.ds(h, wlen, stride=KVH_D), :], kwn, kpad], axis=0).astype(BF16)
        vw = jnp.concatenate([wv_ref[0, pl.ds(h, wlen, stride=KVH_D), :], vwn, kpad], axis=0).astype(BF16)
        qh = [_rope(z_ref[0, :, offs["q"] + (h * gq + g) * HEAD_DIM:offs["q"] + (h * gq + g + 1) * HEAD_DIM], c, s)
              for g in range(gq)]
        qr.extend(qh)
        qs = jnp.concatenate(qh, axis=0).astype(BF16)
        pw = _masked_softmax_rows(_dot_nt(qs, kw) * ATTN_SCALE, ok_wg)
        ow = _dot(pw.astype(BF16), vw)
        o_win.extend([ow[g * SROWS:(g + 1) * SROWS, :] for g in range(gq)])

    def drain(sidx, carry):
        for cp in slot_copies(b, half, sidx):
            cp.wait()
        return carry

    lax.fori_loop(0, nslots, drain, 0)

    nkeys = N_SEL * CMP_BLOCK
    colk = lax.broadcasted_iota(jnp.int32, (SROWS, nkeys + LANES), 1)
    slot_of_col = jnp.right_shift(colk, CMP_SHIFT)
    in_blk = colk & (CMP_BLOCK - 1)
    o_sel = [jnp.zeros((SROWS, HEAD_DIM), F32) for _ in range(n_heads)]
    for h in range(KVH_D):
        ksn = knew_ref[0, :, kvw + h * HEAD_DIM:kvw + (h + 1) * HEAD_DIM]
        vsn = z_ref[0, :, offs["vs"] + h * HEAD_DIM:offs["vs"] + (h + 1) * HEAD_DIM]
        for t in range(ts):
            qt = jnp.zeros((SROWS, HEAD_DIM), F32)
            for g in range(gq):
                qt = jnp.where(row8 == g, jnp.broadcast_to(qr[h * gq + g][t:t + 1, :], (SROWS, HEAD_DIM)), qt)
            qt = qt.astype(BF16)
            s0 = (h * ts + t) * N_SEL
            kg = kbuf[half, pl.ds(s0, N_SEL), pl.ds(h, CMP_BLOCK, stride=KVH_D), :].reshape(nkeys, HEAD_DIM)
            vg = vbuf[half, pl.ds(s0, N_SEL), pl.ds(h, CMP_BLOCK, stride=KVH_D), :].reshape(nkeys, HEAD_DIM)
            kall = jnp.concatenate([kg, ksn, kpad], axis=0).astype(BF16)
            vall = jnp.concatenate([vg, vsn, kpad], axis=0).astype(BF16)
            far = past + ts + SROWS
            kpos = jnp.where(colk >= nkeys, past + (colk - nkeys), 0)
            kpos = jnp.where((colk >= nkeys) & (colk - nkeys >= ts), far, kpos)
            n_new = jnp.int32(0)
            for r in range(N_SEL):
                blk = idx_ref[((b * KVH_D + h) * ts + t) * N_SEL + r]
                n_new = n_new + jnp.where(blk >= n_past, 1, 0)
                base = jnp.where(blk < n_past, blk * CMP_BLOCK, far)
                kpos = jnp.where(slot_of_col == r, base + in_blk, kpos)
            kpos = jnp.where((colk >= nkeys) & (n_new == 0), far, kpos)
            ok = kpos <= past + t
            p = _masked_softmax_rows(_dot_nt(qt, kall) * ATTN_SCALE, ok)
            res = _dot(p.astype(BF16), vall)
            for g in range(gq):
                hd = h * gq + g
                o_sel[hd] = jnp.where(row8 == t, jnp.broadcast_to(res[g:g + 1, :], (SROWS, HEAD_DIM)), o_sel[hd])

    ngate = 3 * n_heads
    gate = jax.nn.sigmoid(z_ref[0, :, offs["gl"]:offs["gl"] + ngate] + gbias_ref[:, 0:ngate])
    for hd in range(n_heads):
        sl = slice(hd * HEAD_DIM, (hd + 1) * HEAD_DIM)
        yd = (gate[:, hd:hd + 1] * ocmp_ref[0, :, sl]
              + gate[:, n_heads + hd:n_heads + hd + 1] * o_sel[hd]
              + gate[:, 2 * n_heads + hd:2 * n_heads + hd + 1] * o_win[hd])
        y_ref[0, :, sl] = (yd * _silu(z_ref[0, :, offs["gd"] + hd * HEAD_DIM:offs["gd"] + (hd + 1) * HEAD_DIM])
                           ).astype(BF16)


def nsa_sample_sel(idx_flat, pt_flat, z3, ocmp, knew, wk, wv, ctab, stab, gbias, kpool, vpool, offs, n_heads,
                   ts, past, page_base, npages, layer, n_layers_b):
    bsz, _, p = z3.shape
    wlen = wk.shape[1] // KVH_D
    kvw = KVH_D * HEAD_DIM
    qw = n_heads * HEAD_DIM
    n_past = npages * (PAGE_SIZE // CMP_BLOCK)
    grid_spec = pltpu.PrefetchScalarGridSpec(
        num_scalar_prefetch=2,
        grid=(bsz,),
        in_specs=[pl.BlockSpec((1, SROWS, p), lambda b, i, t: (b, 0, 0)),
                  pl.BlockSpec((1, SROWS, qw), lambda b, i, t: (b, 0, 0)),
                  pl.BlockSpec((1, SROWS, 3 * kvw), lambda b, i, t: (b, 0, 0)),
                  pl.BlockSpec((1, wlen * KVH_D, HEAD_DIM), lambda b, i, t: (layer * n_layers_b + b, 0, 0)),
                  pl.BlockSpec((1, wlen * KVH_D, HEAD_DIM), lambda b, i, t: (layer * n_layers_b + b, 0, 0)),
                  pl.BlockSpec((SROWS, LANES), lambda b, i, t: (0, 0)),
                  pl.BlockSpec((SROWS, LANES), lambda b, i, t: (0, 0)),
                  pl.BlockSpec((1, LANES), lambda b, i, t: (0, 0)),
                  pl.BlockSpec(memory_space=pl.ANY), pl.BlockSpec(memory_space=pl.ANY)],
        out_specs=pl.BlockSpec((1, SROWS, qw), lambda b, i, t: (b, 0, 0)),
        scratch_shapes=[pltpu.VMEM((2, KVH_D * ts * N_SEL, CMP_BLOCK * KVH_D, HEAD_DIM), F32),
                        pltpu.VMEM((2, KVH_D * ts * N_SEL, CMP_BLOCK * KVH_D, HEAD_DIM), F32),
                        pltpu.SemaphoreType.DMA((2, 2))],
    )
    return pl.pallas_call(
        functools.partial(_nsa_sample_sel_kernel, offs=offs, gq=n_heads // KVH_D, n_heads=n_heads, ts=ts,
                          n_past=n_past, past=past, page_base=page_base, npages=npages, wlen=wlen),
        out_shape=jax.ShapeDtypeStruct((bsz, SROWS, qw), BF16),
        grid_spec=grid_spec,
        compiler_params=_cparams(("arbitrary",)),
        name="nsa_sample_sel",
    )(idx_flat, pt_flat, z3, ocmp, knew, wk, wv, ctab, stab, gbias, kpool, vpool)


EVEN_OFFS = dict(xa=0, ga=1024, q=2048, k=3072, v=3328, gb=3584)


def _odd_layout(d_c, d_d, kvw, n_gate):
    offs = {}
    pos = 0
    for name, width in (("xc", d_c), ("gc", d_c), ("q", d_d), ("kc", kvw), ("vc", kvw), ("ks", kvw), ("vs", kvw),
                        ("kw", kvw), ("vw", kvw), ("gd", d_d), ("gl", n_gate)):
        offs[name] = pos
        pos += width
    return offs, pos


def _pick_tile(n, prefs):
    for t in prefs:
        if n % t == 0:
            return t
    raise ValueError(n)


def kernel(x_prompt, x_sample, state_lru_h, state_lru_conv, cache_swa_k, cache_swa_v, state_pool, cache_nsa_cmp_k, cache_nsa_cmp_v, cache_nsa_sel_k, cache_nsa_sel_v, cache_nsa_win_k, cache_nsa_win_v, page_table, norm_g, final_g, w_in_even, conv_w, conv_b, w_rgate, b_rgate, w_igate, b_igate, lru_lambda, swa_sinks, w_out_even, w_in_odd, w_pool, pool_scale, cmp_wk, cmp_wv, nsa_gate_b, w_out_odd):
    bp, tp, d = x_prompt.shape
    bs, ts, _ = x_sample.shape
    depth = norm_g.shape[0]
    npages = page_table.shape[1]
    past = npages * PAGE_SIZE
    n_phys = cache_nsa_cmp_k.shape[1]
    d_a = state_lru_h.shape[-1]
    n_heads_b = swa_sinks.shape[1]
    d_b = n_heads_b * HEAD_DIM
    d_c = state_pool.shape[-1]
    n_heads_d = nsa_gate_b.shape[1] // 3
    d_d = n_heads_d * HEAD_DIM
    kvw_b = KVH_B * HEAD_DIM
    kvw_d = KVH_D * HEAD_DIM
    assert ts <= SROWS and past % CMP_BLOCK == 0 and (past + ts - 1) // CMP_BLOCK == past // CMP_BLOCK

    hp = x_prompt.reshape(bp * tp, d)
    hs = jnp.pad(x_sample, ((0, 0), (0, SROWS - ts), (0, 0))).reshape(bs * SROWS, d)
    ms = bs * SROWS
    ctab_p, stab_p = rope_tables(tp, 0)
    ctab_s, stab_s = rope_tables(SROWS, past)
    tm_p = _pick_tile(bp * tp, (1024, 512, 256, 128))
    pt_flat = page_table.reshape(-1)

    odd_offs, odd_total = _odd_layout(d_c, d_d, kvw_d, 3 * n_heads_d)
    w_in_even_b = w_in_even.astype(BF16)
    w_in_odd_b = w_in_odd.astype(BF16)
    w_out_even_b = w_out_even.astype(BF16)
    w_out_odd_b = w_out_odd.astype(BF16)
    ev_p, ev_s, od_p, od_s = [], [], [], []
    for i in range(depth):
        j = i // 2
        if i % 2 == 0:
            p_even = w_in_even.shape[2]
            tn = _pick_tile(p_even, (512, 256, 128))
            zp = norm_proj(hp, norm_g[i], w_in_even_b, j, tm_p, tn)
            zs = norm_proj(hs, norm_g[i], w_in_even_b, j, ms, tn)
            wr = w_rgate[j].astype(BF16)
            wi = w_igate[j].astype(BF16)
            lru_args = (conv_w[j], conv_b[j], wr, b_rgate[j], wi, b_igate[j], lru_lambda[j])
            eo = EVEN_OFFS
            ya_p, hl_p = lru_prompt(zp, bp, tp, d_a, *lru_args)
            yb_p, kout_p = swa_prompt(zp, swa_sinks[j], ctab_p, stab_p, bp, tp, eo["q"], eo["k"], eo["v"], eo["gb"],
                                      n_heads_b)
            tn_o = _pick_tile(d, (512, 256, 128))
            hp = out_proj(ya_p, yb_p, w_out_even_b, j, hp, tm_p, tn_o)
            zp3 = zp.reshape(bp, tp, p_even)
            keep = min(WIN_B, tp)
            ev_p.append((hl_p[:, 0], zp3[:, tp - (CONV_W - 1):, eo["xa"]:eo["xa"] + d_a],
                         kout_p.transpose(0, 2, 1, 3)[:, WIN_B - keep:],
                         zp3[:, tp - keep:, eo["v"]:eo["v"] + kvw_b].reshape(bp, keep, KVH_B, HEAD_DIM)))
            zs3 = zs.reshape(bs, SROWS, p_even)
            zst = zs3.transpose(1, 0, 2)
            ya_t, hl_s = lru_sample(zst[:, :, eo["xa"]:eo["xa"] + d_a], zst[:, :, eo["ga"]:eo["ga"] + d_a],
                                    state_lru_conv[j].transpose(1, 0, 2), state_lru_h[j], *lru_args, ts)
            ya_s = ya_t.transpose(1, 0, 2).reshape(ms, d_a)
            rows3 = lambda a: a.reshape(a.shape[0] * a.shape[1], a.shape[2] * a.shape[3], a.shape[4])
            yb_s, knew = swa_sample(zs3, swa_sinks[j], rows3(cache_swa_k), rows3(cache_swa_v), j, ctab_s, stab_s,
                                    eo["q"], eo["k"], eo["v"], eo["gb"], n_heads_b, ts)
            hs = out_proj(ya_s, yb_s.reshape(ms, d_b), w_out_even_b, j, hs, ms, tn_o)
            wlen = cache_swa_k.shape[2]
            new_conv = jnp.concatenate([state_lru_conv[j], zs3[:, :ts, eo["xa"]:eo["xa"] + d_a]], axis=1)[:, -(CONV_W - 1):]
            ev_s.append((hl_s, new_conv, knew[:, :ts].reshape(bs, ts, KVH_B, HEAD_DIM),
                         zs3[:, :ts, eo["v"]:eo["v"] + kvw_b].reshape(bs, ts, KVH_B, HEAD_DIM)))
        else:
            oo = odd_offs
            tn = 512
            zp = norm_proj(hp, norm_g[i], w_in_odd_b, j, tm_p, tn)
            zs = norm_proj(hs, norm_g[i], w_in_odd_b, j, ms, tn)
            wp = w_pool[j].astype(BF16)
            cwk2 = jnp.repeat(cmp_wk[j], HEAD_DIM, axis=1)
            cwv2 = jnp.repeat(cmp_wv[j], HEAD_DIM, axis=1)
            gbias = jnp.pad(nsa_gate_b[j], (0, LANES - 3 * n_heads_d)).reshape(1, LANES)
            tn_o = _pick_tile(d, (512, 256, 128))
            yc_p = pool_prompt(zp, bp, tp, d_c, oo["xc"], oo["gc"], wp, pool_scale[j])
            qr, kcr, vcf, ksr, vsf, kwr, vwf, kvb, kcmp, vcmp = nsa_prep(zp, ctab_p, stab_p, cwk2, cwv2, bp, tp, oo,
                                                                         n_heads_d)
            n_layers_odd = cache_nsa_cmp_k.shape[0]
            pool3 = lambda a: a.reshape(n_layers_odd * n_phys, PAGE_SIZE * KVH_D, HEAD_DIM)
            page_w = lambda w: jnp.broadcast_to(jnp.tile(w, (PAGE_SIZE // CMP_BLOCK, 1)).reshape(-1, 1),
                                                (PAGE_SIZE * KVH_D, LANES))
            cwk_page = page_w(cmp_wk[j])
            cwv_page = page_w(cmp_wv[j])
            nsteps_nsa = bp * KVH_D * (tp // 128)
            ppc = (bs * npages) // nsteps_nsa if (bs * npages) % nsteps_nsa == 0 else 0
            fused = 0 < ppc <= 32 and npages % ppc == 0
            if fused:
                yd_p, kcp, vcp = nsa_prompt(zp, qr, kcmp, vcmp, kvb, gbias, bp, tp, oo, n_heads_d,
                                            (pt_flat, pool3(cache_nsa_cmp_k), pool3(cache_nsa_cmp_v), cwk_page, cwv_page,
                                             bs, npages, j * n_phys, ppc))
            else:
                yd_p = nsa_prompt(zp, qr, kcmp, vcmp, kvb, gbias, bp, tp, oo, n_heads_d)
                kcp, vcp = cmp_stream(pt_flat, pool3(cache_nsa_cmp_k), pool3(cache_nsa_cmp_v), cwk_page, cwv_page,
                                      bs, npages, j * n_phys)
            hp = out_proj(yc_p, yd_p, w_out_odd_b, j, hp, tm_p, tn_o)
            zp3 = zp.reshape(bp, tp, odd_total)
            kv4 = lambda a: a.reshape(bp, tp, KVH_D, HEAD_DIM)
            keep = min(WIN_D, tp)
            od_p.append((zp3[:, tp - (POOL_MAX - 1):, oo["xc"]:oo["xc"] + d_c],
                         kv4(kcr), kv4(vcf), kv4(ksr), kv4(vsf), kv4(kwr)[:, tp - keep:], kv4(vwf)[:, tp - keep:]))
            zs3 = zs.reshape(bs, SROWS, odd_total)
            zst = zs3.transpose(1, 0, 2)
            yc_t = pool_sample(zst[:, :, oo["xc"]:oo["xc"] + d_c], zst[:, :, oo["gc"]:oo["gc"] + d_c],
                               state_pool[j].transpose(1, 0, 2), wp, pool_scale[j], ts, past)
            yc_s = yc_t.transpose(1, 0, 2).reshape(ms, d_c)
            ocmp, imp, knew = nsa_sample_cmp(zs3, kcp, vcp, ctab_s, stab_s, cwk2, cwv2, oo, n_heads_d, ts, past)
            nbp = imp.shape[2]
            assert nbp >= N_SEL
            imp_t = imp[:, :, :, :ts].transpose(2, 0, 1, 3).reshape(nbp, bs * KVH_D * ts)
            qpos = jnp.tile(past + jnp.arange(ts, dtype=jnp.int32), bs * KVH_D).reshape(1, -1)
            idx = select_sample(imp_t, qpos)
            idx_flat = idx.T.reshape(-1)
            wlen = cache_nsa_win_k.shape[2]
            win3 = lambda a: a.reshape(n_layers_odd * bs, wlen * KVH_D, HEAD_DIM)
            yd_s = nsa_sample_sel(idx_flat, pt_flat, zs3, ocmp, knew, win3(cache_nsa_win_k), win3(cache_nsa_win_v),
                                  ctab_s, stab_s, gbias, pool3(cache_nsa_sel_k), pool3(cache_nsa_sel_v), oo,
                                  n_heads_d, ts, past, j * n_phys, npages, j, bs)
            hs = out_proj(yc_s, yd_s.reshape(ms, d_d), w_out_odd_b, j, hs, ms, tn_o)
            kv4s = lambda a: a[:, :ts].reshape(bs, ts, KVH_D, HEAD_DIM)
            new_pool = jnp.concatenate([state_pool[j], zs3[:, :ts, oo["xc"]:oo["xc"] + d_c]], axis=1)[:, -(POOL_MAX - 1):]
            kwn = kv4s(knew[:, :, 2 * kvw_d:3 * kvw_d])
            vwn = kv4s(zs3[:, :, oo["vw"]:oo["vw"] + kvw_d])
            od_s.append((new_pool,
                         kv4s(knew[:, :, 0:kvw_d]), kv4s(zs3[:, :, oo["vc"]:oo["vc"] + kvw_d]),
                         kv4s(knew[:, :, kvw_d:2 * kvw_d]), kv4s(zs3[:, :, oo["vs"]:oo["vs"] + kvw_d]),
                         kwn, vwn))

    y_prompt = final_norm(hp, final_g, tm_p).reshape(bp, tp, d)
    y_sample = final_norm(hs, final_g, ms).reshape(bs, SROWS, d)[:, :ts]

    def field(lst, k):
        return jnp.stack([st[k] for st in lst], axis=0)

    def appended(cache, new_rows):
        return jnp.concatenate([cache, new_rows], axis=2)[:, :, -cache.shape[2]:]

    return (y_prompt, y_sample,
            field(ev_p, 0), field(ev_p, 1), field(ev_p, 2), field(ev_p, 3),
            field(od_p, 0), field(od_p, 1), field(od_p, 2), field(od_p, 3), field(od_p, 4), field(od_p, 5), field(od_p, 6),
            field(ev_s, 0), field(ev_s, 1), appended(cache_swa_k, field(ev_s, 2)), appended(cache_swa_v, field(ev_s, 3)),
            field(od_s, 0), field(od_s, 1), field(od_s, 2), field(od_s, 3), field(od_s, 4),
            appended(cache_nsa_win_k, field(od_s, 5)), appended(cache_nsa_win_v, field(od_s, 6)))
```
